```python
import math
import jax
import jax.numpy as jnp
from jax import lax
import numpy as np

D_MODEL = 1024
BATCH = 8
SEQ = 2048
DEPTH = 2

MIX_WIDTH = D_MODEL
GDN_HEADS = 4
GDN_DK = D_MODEL // 8
GDN_DV = D_MODEL // 8
GDN_CONV = 4
GDN_CHUNK = 64
GDN_QK = GDN_HEADS * GDN_DK
GDN_V = GDN_HEADS * GDN_DV
SC_WIDTH = D_MODEL // 4
SC_CONV = 3
DIFF_HEADS = 4
DIFF_DV = D_MODEL // 16
DIFF_DQK = DIFF_DV // 2
DIFF_QK = DIFF_HEADS * 2 * DIFF_DQK
DIFF_V = DIFF_HEADS * DIFF_DV
ATTN_BLOCK = 128
REL_BUCKETS = 32
REL_MAX_DIST = 128
PEER_HEADS = 8
PEER_KEYS = 128
PEER_EXPERTS = PEER_KEYS * PEER_KEYS
PEER_TOPK = 16
PEER_DKEY = D_MODEL // 4
PEER_BLOCK = 128
EPS = 1e-6
IN_SPLITS = (GDN_QK, GDN_QK, GDN_V, GDN_V, GDN_HEADS, GDN_HEADS,
             SC_WIDTH, SC_WIDTH, SC_WIDTH,
             DIFF_QK, DIFF_QK, DIFF_V)
IN_WIDTH = 2 * GDN_QK + 2 * GDN_V + 2 * GDN_HEADS + 3 * SC_WIDTH + 2 * DIFF_QK + DIFF_V

kernel_name = 'hybrid_gdn_shortconv_diffattn_peer'


def rms_norm(x, gain):
    xf = x.astype(jnp.float32)
    y = xf * lax.rsqrt(jnp.mean(xf * xf, axis=-1, keepdims=True) + EPS)
    return (y * gain.astype(jnp.float32)).astype(x.dtype)


def l2_norm(x):
    xf = x.astype(jnp.float32)
    return (xf * lax.rsqrt(jnp.sum(xf * xf, axis=-1, keepdims=True) + EPS)).astype(x.dtype)


def causal_depthwise_conv(x, w):
    width, ch = w.shape
    return lax.conv_general_dilated(
        x, w[:, None, :].astype(x.dtype), window_strides=(1,), padding=[(width - 1, 0)],
        dimension_numbers=('NWC', 'WIO', 'NWC'), feature_group_count=ch)


def gated_delta_rule(q, k, v, g, beta):
    b, s, h, dk = q.shape
    dv = v.shape[-1]
    nc = s // GDN_CHUNK
    f32 = jnp.float32

    def chunks(t):
        t = t.reshape((b, nc, GDN_CHUNK, h) + t.shape[3:])
        return jnp.moveaxis(t, (1, 3), (0, 2))

    qc = chunks(q.astype(f32) * (dk ** -0.5))
    kc = chunks(k.astype(f32))
    vc = chunks(v.astype(f32))
    gc = jnp.cumsum(chunks(g.astype(f32)), axis=-1)
    bc = chunks(beta.astype(f32))
    idx = jnp.arange(GDN_CHUNK)
    tril = idx[:, None] >= idx[None, :]
    strict = idx[:, None] > idx[None, :]
    decay = jnp.exp(jnp.where(tril, gc[..., :, None] - gc[..., None, :], -jnp.inf))
    kb = kc * bc[..., None]
    vb = vc * bc[..., None]
    lower = jnp.where(strict, jnp.einsum('nbhid,nbhjd->nbhij', kb, kc) * decay, 0.0)
    eye = jnp.eye(GDN_CHUNK, dtype=f32)
    tmat = lax.linalg.triangular_solve(eye + lower, jnp.broadcast_to(eye, lower.shape),
                                       left_side=True, lower=True, unit_diagonal=True)
    u = jnp.einsum('nbhij,nbhjd->nbhid', tmat, vb)
    w = jnp.einsum('nbhij,nbhjd->nbhid', tmat, kb * jnp.exp(gc)[..., None])
    qk_intra = jnp.where(tril, jnp.einsum('nbhid,nbhjd->nbhij', qc, kc) * decay, 0.0)

    def step(state, inp):
        q_i, k_i, u_i, w_i, g_i, a_i = inp
        v_new = u_i - jnp.einsum('bhcd,bhde->bhce', w_i, state)
        o = (jnp.einsum('bhcd,bhde->bhce', q_i * jnp.exp(g_i)[..., None], state)
             + jnp.einsum('bhij,bhje->bhie', a_i, v_new))
        g_last = g_i[..., -1]
        state = (state * jnp.exp(g_last)[..., None, None]
                 + jnp.einsum('bhcd,bhce->bhde', k_i * jnp.exp(g_last[..., None] - g_i)[..., None], v_new))
        return state, o

    state0 = jnp.zeros((b, h, dk, dv), f32)
    _, o = lax.scan(step, state0, (qc, kc, u, w, gc, qk_intra))
    return jnp.moveaxis(o, (0, 2), (1, 3)).reshape(b, s, h, dv)


def gdn_mixer(q, k, v, z, b_logit, a_logit, conv_w, a_log, dt_bias, out_norm):
    bsz, s, _ = q.shape
    dt = q.dtype
    qkv = jax.nn.silu(causal_depthwise_conv(jnp.concatenate([q, k, v], axis=-1), conv_w))
    q, k, v = jnp.split(qkv, [GDN_QK, 2 * GDN_QK], axis=-1)
    q = l2_norm(q.reshape(bsz, s, GDN_HEADS, GDN_DK))
    k = l2_norm(k.reshape(bsz, s, GDN_HEADS, GDN_DK))
    v = v.reshape(bsz, s, GDN_HEADS, GDN_DV)
    beta = jax.nn.sigmoid(b_logit.astype(jnp.float32))
    g = -jnp.exp(a_log.astype(jnp.float32)) * jax.nn.softplus(a_logit.astype(jnp.float32) + dt_bias.astype(jnp.float32))
    o = gated_delta_rule(q, k, v, g, beta)
    o = rms_norm(o, out_norm) * jax.nn.silu(z.reshape(bsz, s, GDN_HEADS, GDN_DV).astype(jnp.float32))
    return o.reshape(bsz, s, GDN_V).astype(dt)


def short_conv_mixer(gate_b, gate_c, hx, conv_w):
    return gate_b * causal_depthwise_conv(gate_c * hx, conv_w)


def rel_bucket(rel):
    max_exact = REL_BUCKETS // 2
    n = jnp.maximum(rel, 0)
    large = max_exact + (jnp.log(jnp.maximum(n, max_exact).astype(jnp.float32) / max_exact)
                         / math.log(REL_MAX_DIST / max_exact) * (REL_BUCKETS - max_exact)).astype(jnp.int32)
    large = jnp.minimum(large, REL_BUCKETS - 1)
    return jnp.where(n < max_exact, n, large)


def diff_attention(q, k, v, q_gain, k_gain, lam_params, subln, rel_bias, lambda_init):
    bsz, s, _ = q.shape
    f32 = jnp.float32
    q = rms_norm(q.reshape(bsz, s, DIFF_HEADS, 2, DIFF_DQK), q_gain) * (DIFF_DQK ** -0.5)
    k = rms_norm(k.reshape(bsz, s, DIFF_HEADS, 2, DIFF_DQK), k_gain)
    v = v.reshape(bsz, s, DIFF_HEADS, DIFF_DV)
    lp = lam_params.astype(f32)
    lam = jnp.exp(jnp.sum(lp[0] * lp[1])) - jnp.exp(jnp.sum(lp[2] * lp[3])) + lambda_init
    nqb = s // ATTN_BLOCK
    qb = q.reshape(bsz, nqb, ATTN_BLOCK, DIFF_HEADS, 2, DIFF_DQK).transpose(1, 0, 3, 4, 2, 5)
    kt = k.transpose(0, 2, 3, 1, 4)
    vt = v.transpose(0, 2, 1, 3)
    k_pos = jnp.arange(s)

    def block(args):
        q_blk, i = args
        q_pos = i * ATTN_BLOCK + jnp.arange(ATTN_BLOCK)
        rel = q_pos[:, None] - k_pos[None, :]
        bias = jnp.moveaxis(rel_bias[rel_bucket(rel)], -1, 0).astype(f32)
        logits = jnp.einsum('bhmqd,bhmkd->bhmqk', q_blk, kt).astype(f32) + bias[None, :, None]
        logits = jnp.where(rel >= 0, logits, -jnp.inf)
        p = jax.nn.softmax(logits, axis=-1)
        attn = p[:, :, 0] - lam * p[:, :, 1]
        return jnp.einsum('bhqk,bhkd->bhqd', attn.astype(vt.dtype), vt)

    o = lax.map(block, (qb, jnp.arange(nqb)))
    o = o.transpose(1, 0, 3, 2, 4).reshape(bsz, s, DIFF_HEADS, DIFF_DV)
    o = rms_norm(o, subln) * (1.0 - lambda_init)
    return o.reshape(bsz, s, DIFF_V)


def peer_ffn(x, w_q, sub_keys, u_tab, v_tab):
    bsz, s, d = x.shape
    t = bsz * s
    xf = x.reshape(t, d)
    q = (xf @ w_q).reshape(t, PEER_HEADS, 2, PEER_DKEY // 2)
    scores = jnp.einsum('thpd,hpnd->thpn', q, sub_keys).astype(jnp.float32)
    s_top, i_top = lax.top_k(scores, PEER_TOPK)
    cand = (s_top[:, :, 0, :, None] + s_top[:, :, 1, None, :]).reshape(t, PEER_HEADS, PEER_TOPK * PEER_TOPK)
    cand_idx = (i_top[:, :, 0, :, None] * PEER_KEYS + i_top[:, :, 1, None, :]).reshape(t, PEER_HEADS, PEER_TOPK * PEER_TOPK)
    best, pos = lax.top_k(cand, PEER_TOPK)
    expert_idx = jnp.take_along_axis(cand_idx, pos, axis=-1)
    gates = jax.nn.softmax(best, axis=-1)
    nb = t // PEER_BLOCK

    def block(args):
        x_blk, idx_blk, g_blk = args
        hid = jax.nn.gelu(jnp.einsum('td,thkd->thk', x_blk, u_tab[idx_blk]), approximate=False)
        coef = (g_blk * hid.astype(jnp.float32)).astype(x_blk.dtype)
        return jnp.einsum('thk,thkd->td', coef, v_tab[idx_blk])

    out = lax.map(block, (xf.reshape(nb, PEER_BLOCK, d),
                          expert_idx.reshape(nb, PEER_BLOCK, PEER_HEADS, PEER_TOPK),
                          gates.reshape(nb, PEER_BLOCK, PEER_HEADS, PEER_TOPK)))
    return out.reshape(bsz, s, d)


def setup_inputs(seed: int = 0) -> dict:
    key = jax.random.key(seed)
    ks = jax.random.split(key, 20)
    f32 = jnp.float32

    def nrm(k, shape, scale):
        return jax.random.normal(k, shape, f32) * scale

    dt = jnp.exp(jax.random.uniform(ks[6], (DEPTH, GDN_HEADS), f32, math.log(1e-3), math.log(1e-1)))
    return {
        'x': nrm(ks[0], (BATCH, SEQ, D_MODEL), 1.0),
        'rel_bias': nrm(ks[1], (REL_BUCKETS, DIFF_HEADS), 0.5),
        'attn_norm': 1.0 + nrm(ks[2], (DEPTH, D_MODEL), 0.02),
        'w_in': nrm(ks[3], (DEPTH, D_MODEL, IN_WIDTH), D_MODEL ** -0.5),
        'gdn_conv': nrm(ks[4], (DEPTH, GDN_CONV, 2 * GDN_QK + GDN_V), GDN_CONV ** -0.5),
        'gdn_a_log': jnp.log(jax.random.uniform(ks[5], (DEPTH, GDN_HEADS), f32, 1.0, 16.0)),
        'gdn_dt_bias': dt + jnp.log(-jnp.expm1(-dt)),
        'gdn_out_norm': 1.0 + nrm(ks[7], (DEPTH, GDN_DV), 0.02),
        'sc_conv': nrm(ks[8], (DEPTH, SC_CONV, SC_WIDTH), SC_CONV ** -0.5),
        'diff_q_norm': 1.0 + nrm(ks[9], (DEPTH, DIFF_DQK), 0.02),
        'diff_k_norm': 1.0 + nrm(ks[10], (DEPTH, DIFF_DQK), 0.02),
        'diff_lambda': nrm(ks[11], (DEPTH, 4, DIFF_DQK), 0.1),
        'diff_subln': 1.0 + nrm(ks[12], (DEPTH, DIFF_DV), 0.02),
        'w_out': nrm(ks[13], (DEPTH, MIX_WIDTH, D_MODEL), MIX_WIDTH ** -0.5),
        'ffn_norm': 1.0 + nrm(ks[14], (DEPTH, D_MODEL), 0.02),
        'peer_wq': nrm(ks[15], (DEPTH, D_MODEL, PEER_HEADS * PEER_DKEY), D_MODEL ** -0.5),
        'peer_keys': nrm(ks[16], (DEPTH, PEER_HEADS, 2, PEER_KEYS, PEER_DKEY // 2), (PEER_DKEY // 2) ** -0.5),
        'peer_u': nrm(ks[17], (DEPTH, PEER_EXPERTS, D_MODEL), D_MODEL ** -0.5),
        'peer_v': nrm(ks[18], (DEPTH, PEER_EXPERTS, D_MODEL), 0.25),
    }


def reference(x, rel_bias, attn_norm, w_in, gdn_conv, gdn_a_log, gdn_dt_bias, gdn_out_norm,
              sc_conv, diff_q_norm, diff_k_norm, diff_lambda, diff_subln, w_out, ffn_norm,
              peer_wq, peer_keys, peer_u, peer_v):
    offsets = np.cumsum(IN_SPLITS)[:-1].tolist()
    h = x
    for l in range(DEPTH):
        lambda_init = 0.8 - 0.6 * math.exp(-0.3 * l)
        n = rms_norm(h, attn_norm[l])
        proj = n @ w_in[l]
        gq, gk, gv, gz, gb, ga, sb, sc, sh, dq, dk, dv = jnp.split(proj, offsets, axis=-1)
        y_gdn = gdn_mixer(gq, gk, gv, gz, gb, ga, gdn_conv[l], gdn_a_log[l], gdn_dt_bias[l], gdn_out_norm[l])
        y_sc = short_conv_mixer(sb, sc, sh, sc_conv[l])
        y_diff = diff_attention(dq, dk, dv, diff_q_norm[l], diff_k_norm[l], diff_lambda[l],
                                diff_subln[l], rel_bias, lambda_init)
        mixed = jnp.concatenate([y_gdn.astype(h.dtype), y_sc.astype(h.dtype), y_diff.astype(h.dtype)], axis=-1)
        h = h + mixed @ w_out[l]
        h = h + peer_ffn(rms_norm(h, ffn_norm[l]), peer_wq[l], peer_keys[l], peer_u[l], peer_v[l])
    return h
```

```python
import functools
import math

import jax
import jax.numpy as jnp
import numpy as np
from jax import lax
from jax.experimental import pallas as pl
from jax.experimental.pallas import tpu as pltpu

F32 = jnp.float32
BF16 = jnp.bfloat16
EPS = 1e-6
NEG = -1e30

D_MODEL = 1024
GDN_HEADS = 4
GDN_D = 128
GDN_CONV = 4
GDN_CHUNK = 64
SC_WIDTH = 256
SC_CONV = 3
DIFF_HEADS = 4
DIFF_DV = 64
DIFF_DQK = 32
ATT_BLK = 256
REL_BUCKETS = 32
REL_MAX_DIST = 128
PEER_HEADS = 8
PEER_KEYS = 128
PEER_TOPK = 16
PEER_DHALF = 128
LANES = 128
VMEM_LIMIT = 56 * 1024 * 1024

W_QKVZ = 2048
W_SCD = 1536
W_BA = LANES


def _cparams(*sem):
    return pltpu.CompilerParams(dimension_semantics=sem, vmem_limit_bytes=VMEM_LIMIT)


def _nt_dot(a, b):
    return lax.dot_general(a, b, (((1,), (1,)), ((), ())), preferred_element_type=F32)


def _tn_dot(a, b):
    return lax.dot_general(a, b, (((0,), (0,)), ((), ())), preferred_element_type=F32)


def _dot(a, b):
    return jnp.dot(a, b, preferred_element_type=F32)


def _split3(x):
    hi = x.astype(BF16)
    r = x - hi.astype(F32)
    mid = r.astype(BF16)
    lo = (r - mid.astype(F32)).astype(BF16)
    return hi, mid, lo


def _inproj_body(h_ref, g_ref, w_ref, qkvz_ref, scd_ref, ba_ref):
    x = h_ref[...]
    n = x * lax.rsqrt(jnp.mean(x * x, axis=-1, keepdims=True) + EPS) * g_ref[...]
    nb = n.astype(BF16)
    qkvz_ref[...] = _dot(nb, w_ref[:, 0:W_QKVZ]).astype(BF16)
    scd_ref[...] = _dot(nb, w_ref[:, W_QKVZ:W_QKVZ + W_SCD]).astype(BF16)
    ba_ref[...] = _dot(nb, w_ref[:, W_QKVZ + W_SCD:])


def _inproj(h, gain, w):
    t, d = h.shape
    tm = min(512, t)
    nw = w.shape[1]
    return pl.pallas_call(
        _inproj_body,
        grid=(t // tm,),
        in_specs=[pl.BlockSpec((tm, d), lambda i: (i, 0)),
                  pl.BlockSpec((1, d), lambda i: (0, 0)),
                  pl.BlockSpec((d, nw), lambda i: (0, 0))],
        out_specs=[pl.BlockSpec((tm, W_QKVZ), lambda i: (i, 0)),
                   pl.BlockSpec((tm, W_SCD), lambda i: (i, 0)),
                   pl.BlockSpec((tm, W_BA), lambda i: (i, 0))],
        out_shape=[jax.ShapeDtypeStruct((t, W_QKVZ), BF16),
                   jax.ShapeDtypeStruct((t, W_SCD), BF16),
                   jax.ShapeDtypeStruct((t, W_BA), F32)],
        compiler_params=_cparams("parallel"),
    )(h, gain, w)


def _gdn_body(qkvz_ref, ba_ref, conv_ref, prm_ref, gain_ref, y_ref, q_s, k_s, v_s, gc_s, bt_s, st_s):
    s = qkvz_ref.shape[0]
    c_sz = GDN_CHUNK
    row = lax.broadcasted_iota(jnp.int32, (s, LANES), 0)

    ba = ba_ref[...]
    bt_s[...] = jax.nn.sigmoid(ba)
    xg = ba + prm_ref[1:2, :]
    softplus = jnp.maximum(xg, 0.0) + jnp.log(1.0 + jnp.exp(-jnp.abs(xg)))
    g = -jnp.exp(prm_ref[0:1, :]) * softplus
    pos = row % c_sz
    for sh in (1, 2, 4, 8, 16, 32):
        g = g + jnp.where(pos >= sh, pltpu.roll(g, sh, 0), 0.0)
    gc_s[...] = g

    for cb in range(3 * GDN_HEADS):
        x = qkvz_ref[:, cb * LANES:(cb + 1) * LANES].astype(F32)
        w = conv_ref[:, cb * LANES:(cb + 1) * LANES]
        acc = x * w[GDN_CONV - 1:GDN_CONV, :]
        for j in range(GDN_CONV - 1):
            sh = GDN_CONV - 1 - j
            acc = acc + jnp.where(row >= sh, pltpu.roll(x, sh, 0), 0.0) * w[j:j + 1, :]
        y = acc * jax.nn.sigmoid(acc)
        kind, hh = divmod(cb, GDN_HEADS)
        if kind < 2:
            y = y * lax.rsqrt(jnp.sum(y * y, axis=-1, keepdims=True) + EPS)
        (q_s, k_s, v_s)[kind][hh] = y

    st_s[...] = jnp.zeros_like(st_s)
    ii = lax.broadcasted_iota(jnp.int32, (c_sz, c_sz), 0)
    jj = lax.broadcasted_iota(jnp.int32, (c_sz, c_sz), 1)
    tril = ii >= jj
    strict = ii > jj
    eye = (ii == jj).astype(F32)
    lane = lax.broadcasted_iota(jnp.int32, (c_sz, LANES), 1)
    pick3 = (lane < 3).astype(BF16)
    scale = GDN_D ** -0.5
    gain = gain_ref[...]

    def chunk(c, carry):
        r0 = pl.multiple_of(c * c_sz, c_sz)
        gcb = gc_s[pl.ds(r0, c_sz), :]
        btb = bt_s[pl.ds(r0, c_sz), :]
        for hh in range(GDN_HEADS):
            gcol = gcb[:, GDN_HEADS + hh:GDN_HEADS + hh + 1]
            bcol = btb[:, hh:hh + 1]
            glast = gcol[c_sz - 1:c_sz, :]
            qc = q_s[hh, pl.ds(r0, c_sz), :] * scale
            kc = k_s[hh, pl.ds(r0, c_sz), :]
            vc = v_s[hh, pl.ds(r0, c_sz), :]
            hi, mid, lo = _split3(gcol)
            x3 = jnp.where(lane == 0, hi.astype(F32),
                           jnp.where(lane == 1, mid.astype(F32),
                                     jnp.where(lane == 2, lo.astype(F32), 0.0))).astype(BF16)
            grow = _nt_dot(pick3, x3)
            decay = jnp.where(tril, jnp.exp(jnp.minimum(gcol - grow, 0.0)), 0.0)
            kb = kc * bcol
            kcb = kc.astype(BF16)
            lower = jnp.where(strict, _nt_dot(kb.astype(BF16), kcb) * decay, 0.0)
            pw = -lower
            tm = eye + pw
            for _ in range(5):
                pwb = pw.astype(BF16)
                pw = _dot(pwb, pwb)
                tm = tm + _dot(tm.astype(BF16), pw.astype(BF16))
            tmb = tm.astype(BF16)
            u = _dot(tmb, (vc * bcol).astype(BF16))
            w = _dot(tmb, (kb * jnp.exp(gcol)).astype(BF16))
            a_in = jnp.where(tril, _nt_dot(qc.astype(BF16), kcb) * decay, 0.0)
            state = st_s[hh]
            stb = state.astype(BF16)
            v_new = u - _dot(w.astype(BF16), stb)
            vnb = v_new.astype(BF16)
            o = _dot((qc * jnp.exp(gcol)).astype(BF16), stb) + _dot(a_in.astype(BF16), vnb)
            st_s[hh] = state * jnp.exp(glast) + _tn_dot((kc * jnp.exp(glast - gcol)).astype(BF16), vnb)
            z = qkvz_ref[pl.ds(r0, c_sz), (12 + hh) * LANES:(13 + hh) * LANES].astype(F32)
            on = o * lax.rsqrt(jnp.mean(o * o, axis=-1, keepdims=True) + EPS) * gain
            y_ref[pl.ds(r0, c_sz), hh * LANES:(hh + 1) * LANES] = (on * (z * jax.nn.sigmoid(z))).astype(BF16)
        return carry

    lax.fori_loop(0, s // c_sz, chunk, 0)


def _gdn(qkvz, ba, conv_w, prm, gain):
    b, s, _ = qkvz.shape
    hd = GDN_HEADS * GDN_D
    return pl.pallas_call(
        _gdn_body,
        grid=(b,),
        in_specs=[pl.BlockSpec((None, s, W_QKVZ), lambda i: (i, 0, 0)),
                  pl.BlockSpec((None, s, W_BA), lambda i: (i, 0, 0)),
                  pl.BlockSpec((GDN_CONV, 3 * hd), lambda i: (0, 0)),
                  pl.BlockSpec((8, LANES), lambda i: (0, 0)),
                  pl.BlockSpec((1, GDN_D), lambda i: (0, 0))],
        out_specs=pl.BlockSpec((None, s, hd), lambda i: (i, 0, 0)),
        out_shape=jax.ShapeDtypeStruct((b, s, hd), BF16),
        scratch_shapes=[pltpu.VMEM((GDN_HEADS, s, GDN_D), F32),
                        pltpu.VMEM((GDN_HEADS, s, GDN_D), F32),
                        pltpu.VMEM((GDN_HEADS, s, GDN_D), F32),
                        pltpu.VMEM((s, LANES), F32),
                        pltpu.VMEM((s, LANES), F32),
                        pltpu.VMEM((GDN_HEADS, GDN_D, GDN_D), F32)],
        compiler_params=_cparams("parallel"),
    )(qkvz, ba, conv_w, prm, gain)


def _attn_body(scd_ref, scw_ref, qg_ref, kg_ref, lam_ref, sub_ref, nb_ref, far_ref, y_ref,
               qn_s, kn_s, ysc_s, m_s, l_s, acc_s, *, lambda_init):
    i = pl.program_id(1)
    s = scd_ref.shape[0]
    blk = ATT_BLK
    o_q, o_k, o_v = 3 * SC_WIDTH, 3 * SC_WIDTH + 256, 3 * SC_WIDTH + 512

    @pl.when(i == 0)
    def _prep():
        row = lax.broadcasted_iota(jnp.int32, (s, SC_WIDTH), 0)
        gate_b = scd_ref[:, 0:SC_WIDTH].astype(F32)
        x = scd_ref[:, SC_WIDTH:2 * SC_WIDTH].astype(F32) * scd_ref[:, 2 * SC_WIDTH:3 * SC_WIDTH].astype(F32)
        w = scw_ref[...]
        acc = x * w[SC_CONV - 1:SC_CONV, :]
        for j in range(SC_CONV - 1):
            sh = SC_CONV - 1 - j
            acc = acc + jnp.where(row >= sh, pltpu.roll(x, sh, 0), 0.0) * w[j:j + 1, :]
        ysc_s[...] = (gate_b * acc).astype(BF16)
        gi = lax.broadcasted_iota(jnp.int32, (256, 256), 0) // DIFF_DQK
        gj = lax.broadcasted_iota(jnp.int32, (256, 256), 1) // DIFF_DQK
        bd = (gi == gj).astype(BF16)
        for off, g_ref, dst, sc in ((o_q, qg_ref, qn_s, DIFF_DQK ** -0.5), (o_k, kg_ref, kn_s, 1.0)):
            xx = scd_ref[:, off:off + 256].astype(F32)
            hi, mid, lo = _split3(xx * xx)
            ss = _dot(hi, bd) + _dot(mid, bd) + _dot(lo, bd)
            dst[...] = (xx * lax.rsqrt(ss * (1.0 / DIFF_DQK) + EPS) * g_ref[...] * sc).astype(BF16)

    lp = lam_ref[...]
    lam = (jnp.exp(jnp.sum(lp[0:1, :] * lp[1:2, :], axis=-1, keepdims=True))
           - jnp.exp(jnp.sum(lp[2:3, :] * lp[3:4, :], axis=-1, keepdims=True)) + lambda_init)

    r0 = pl.multiple_of(i * blk, blk)
    lane = lax.broadcasted_iota(jnp.int32, (blk, LANES), 1)
    grp = lane // DIFF_DQK

    m_s[...] = jnp.full_like(m_s, NEG)
    l_s[...] = jnp.zeros_like(l_s)
    acc_s[...] = jnp.zeros_like(acc_s)

    def q_masked(hp):
        qp = qn_s[pl.ds(r0, blk), hp * LANES:(hp + 1) * LANES]
        return [jnp.where(grp == sidx, qp, jnp.zeros_like(qp)) for sidx in range(4)]

    def update(sidx, logits, vp):
        m_old = m_s[sidx]
        m_new = jnp.maximum(m_old, jnp.max(logits, axis=-1, keepdims=True))
        alpha = jnp.exp(m_old - m_new)
        p = jnp.exp(logits - m_new)
        l_s[sidx] = alpha * l_s[sidx] + jnp.sum(p, axis=-1, keepdims=True)
        acc_s[sidx] = alpha * acc_s[sidx] + _dot(p.astype(BF16), vp)
        m_s[sidx] = m_new

    def kv(c0, hp):
        kp = kn_s[pl.ds(c0, blk), hp * LANES:(hp + 1) * LANES]
        vp = scd_ref[pl.ds(c0, blk), o_v + hp * LANES:o_v + (hp + 1) * LANES]
        return kp, vp

    def far_body(kb, carry):
        c0 = pl.multiple_of(kb * blk, blk)
        for hp in range(2):
            kp, vp = kv(c0, hp)
            qs = q_masked(hp)
            for sidx in range(4):
                head = hp * 2 + sidx // 2
                update(hp * 4 + sidx, _nt_dot(qs[sidx], kp) + far_ref[head], vp)
        return carry

    lax.fori_loop(0, jnp.maximum(i - 1, 0), far_body, 0)

    @pl.when(i >= 1)
    def _prev():
        c0 = pl.multiple_of((i - 1) * blk, blk)
        for hp in range(2):
            kp, vp = kv(c0, hp)
            qs = q_masked(hp)
            for sidx in range(4):
                head = hp * 2 + sidx // 2
                update(hp * 4 + sidx, _nt_dot(qs[sidx], kp) + nb_ref[head, :, 0:blk], vp)

    for hp in range(2):
        kp, vp = kv(r0, hp)
        qs = q_masked(hp)
        for sidx in range(4):
            head = hp * 2 + sidx // 2
            update(hp * 4 + sidx, _nt_dot(qs[sidx], kp) + nb_ref[head, :, blk:2 * blk], vp)

    low = lane < DIFF_DV
    for hp in range(2):
        outs = []
        for hh in range(2):
            s0 = hp * 4 + hh * 2
            outs.append(acc_s[s0] / l_s[s0] - lam * (acc_s[s0 + 1] / l_s[s0 + 1]))
        o = jnp.where(low, outs[0], outs[1])
        sq = o * o
        ss0 = jnp.sum(jnp.where(low, sq, 0.0), axis=-1, keepdims=True)
        ss1 = jnp.sum(jnp.where(low, 0.0, sq), axis=-1, keepdims=True)
        ms = jnp.where(low, ss0, ss1) * (1.0 / DIFF_DV)
        y = o * lax.rsqrt(ms + EPS) * sub_ref[...] * (1.0 - lambda_init)
        y_ref[:, SC_WIDTH + hp * LANES:SC_WIDTH + (hp + 1) * LANES] = y.astype(BF16)
    y_ref[:, 0:SC_WIDTH] = ysc_s[pl.ds(r0, blk), :]


def _attn(scd, sc_w, q_gain, k_gain, lam_p, sub_gain, near_bias, far_bias, lambda_init):
    b, s, _ = scd.shape
    blk = ATT_BLK
    const2 = lambda bi, i: (0, 0)
    const3 = lambda bi, i: (0, 0, 0)
    return pl.pallas_call(
        functools.partial(_attn_body, lambda_init=lambda_init),
        grid=(b, s // blk),
        in_specs=[pl.BlockSpec((None, s, W_SCD), lambda bi, i: (bi, 0, 0)),
                  pl.BlockSpec((SC_CONV, SC_WIDTH), const2),
                  pl.BlockSpec((1, 256), const2),
                  pl.BlockSpec((1, 256), const2),
                  pl.BlockSpec((4, DIFF_DQK), const2),
                  pl.BlockSpec((1, LANES), const2),
                  pl.BlockSpec((DIFF_HEADS, blk, 2 * blk), const3),
                  pl.BlockSpec((DIFF_HEADS, 1, 1), const3)],
        out_specs=pl.BlockSpec((None, blk, 512), lambda bi, i: (bi, i, 0)),
        out_shape=jax.ShapeDtypeStruct((b, s, 512), BF16),
        scratch_shapes=[pltpu.VMEM((s, 256), BF16),
                        pltpu.VMEM((s, 256), BF16),
                        pltpu.VMEM((s, SC_WIDTH), BF16),
                        pltpu.VMEM((8, blk, 1), F32),
                        pltpu.VMEM((8, blk, 1), F32),
                        pltpu.VMEM((8, blk, LANES), F32)],
        compiler_params=_cparams("parallel", "arbitrary"),
    )(scd, sc_w, q_gain, k_gain, lam_p, sub_gain, near_bias, far_bias)


def _rel_bucket(rel):
    max_exact = REL_BUCKETS // 2
    n = jnp.maximum(rel, 0)
    large = max_exact + (jnp.log(jnp.maximum(n, max_exact).astype(F32) / max_exact)
                         / math.log(REL_MAX_DIST / max_exact) * (REL_BUCKETS - max_exact)).astype(jnp.int32)
    large = jnp.minimum(large, REL_BUCKETS - 1)
    return jnp.where(n < max_exact, n, large)


def _bias_tables(rel_bias):
    blk = ATT_BLK
    rel = jnp.arange(blk)[:, None] + blk - jnp.arange(2 * blk)[None, :]
    near = jnp.moveaxis(rel_bias[_rel_bucket(rel)], -1, 0).astype(F32)
    near = jnp.where(rel[None] >= 0, near, NEG)
    far = rel_bias[_rel_bucket(jnp.full((1, 1), 2 * blk, jnp.int32))]
    return near, jnp.moveaxis(far, -1, 0).astype(F32)


def _outproj_body(yg_ref, ya_ref, w_ref, h_ref, g_ref, h2_ref, xt_ref):
    hd = yg_ref.shape[1]
    h2 = h_ref[...] + _dot(yg_ref[...], w_ref[0:hd, :]) + _dot(ya_ref[...], w_ref[hd:, :])
    h2_ref[...] = h2
    n = h2 * lax.rsqrt(jnp.mean(h2 * h2, axis=-1, keepdims=True) + EPS) * g_ref[...]
    xt_ref[...] = n.T.astype(BF16)


def _outproj(yg, ya, w, h, gain):
    t, d = h.shape
    tm = min(512, t)
    return pl.pallas_call(
        _outproj_body,
        grid=(t // tm,),
        in_specs=[pl.BlockSpec((tm, yg.shape[1]), lambda i: (i, 0)),
                  pl.BlockSpec((tm, ya.shape[1]), lambda i: (i, 0)),
                  pl.BlockSpec((d, d), lambda i: (0, 0)),
                  pl.BlockSpec((tm, d), lambda i: (i, 0)),
                  pl.BlockSpec((1, d), lambda i: (0, 0))],
        out_specs=[pl.BlockSpec((tm, d), lambda i: (i, 0)),
                   pl.BlockSpec((d, tm), lambda i: (0, i))],
        out_shape=[jax.ShapeDtypeStruct((t, d), F32),
                   jax.ShapeDtypeStruct((d, t), BF16)],
        compiler_params=_cparams("parallel"),
    )(yg, ya, w, h, gain)


def _route_body(xt_ref, wqt_ref, keys_ref, s1_ref, s2_ref, e1_ref, e2_ref, tau_ref,
                q_s, top_s, mul_s, cv_s, cw_s):
    k = PEER_TOPK
    q_s[...] = _dot(wqt_ref[...], xt_ref[...]).astype(BF16)
    for hh in range(PEER_HEADS):
        sc = []
        for p in range(2):
            r = (hh * 2 + p) * PEER_DHALF
            s = _dot(keys_ref[hh, p], q_s[r:r + PEER_DHALF, :])
            sc.append(s)
            for it in range(k):
                m = jnp.max(s, axis=0, keepdims=True)
                eq = s == m
                top_s[p, it:it + 1, :] = m
                mul_s[p, it:it + 1, :] = jnp.sum(eq.astype(F32), axis=0, keepdims=True)
                s = jnp.where(eq, -jnp.inf, s)
        a, b = top_s[0], top_s[1]
        ma, mb = mul_s[0], mul_s[1]
        for ia in range(k):
            cv_s[ia * k:(ia + 1) * k, :] = a[ia:ia + 1, :] + b
            cw_s[ia * k:(ia + 1) * k, :] = ma[ia:ia + 1, :] * mb
        v = cv_s[...]
        w = cw_s[...]
        tau = jnp.full_like(a[0:1, :], -jnp.inf)
        cnt = jnp.zeros_like(tau)
        vw = v
        for it in range(k):
            m = jnp.max(vw, axis=0, keepdims=True)
            eq = vw == m
            tau = jnp.where(cnt < k, m, tau)
            cnt = cnt + jnp.sum(jnp.where(eq, w, 0.0), axis=0, keepdims=True)
            vw = jnp.where(eq, -jnp.inf, vw)
        v0 = a[0:1, :] + b[0:1, :]
        z = jnp.sum(jnp.where(v >= tau, w * jnp.exp(v - v0), 0.0), axis=0, keepdims=True)
        s1_ref[hh] = sc[0]
        s2_ref[hh] = sc[1]
        e1_ref[hh] = jnp.exp(sc[0] - a[0:1, :])
        e2_ref[hh] = jnp.exp(sc[1] - b[0:1, :]) / z
        tau_ref[hh:hh + 1, :] = tau


def _route(xt, wqt, keys):
    d, t = xt.shape
    tn = min(512, t)
    nq = wqt.shape[0]
    hk = (PEER_HEADS, PEER_KEYS, tn)
    spec = pl.BlockSpec(hk, lambda i: (0, 0, i))
    shp = jax.ShapeDtypeStruct((PEER_HEADS, PEER_KEYS, t), F32)
    return pl.pallas_call(
        _route_body,
        grid=(t // tn,),
        in_specs=[pl.BlockSpec((d, tn), lambda i: (0, i)),
                  pl.BlockSpec((nq, d), lambda i: (0, 0)),
                  pl.BlockSpec((PEER_HEADS, 2, PEER_KEYS, PEER_DHALF), lambda i: (0, 0, 0, 0))],
        out_specs=[spec, spec, spec, spec, pl.BlockSpec((PEER_HEADS, tn), lambda i: (0, i))],
        out_shape=[shp, shp, shp, shp, jax.ShapeDtypeStruct((PEER_HEADS, t), F32)],
        scratch_shapes=[pltpu.VMEM((nq, tn), BF16),
                        pltpu.VMEM((2, PEER_TOPK, tn), F32),
                        pltpu.VMEM((2, PEER_TOPK, tn), F32),
                        pltpu.VMEM((PEER_TOPK * PEER_TOPK, tn), F32),
                        pltpu.VMEM((PEER_TOPK * PEER_TOPK, tn), F32)],
        compiler_params=_cparams("parallel"),
    )(xt, wqt, keys)


def _gelu(x):
    return 0.5 * x * (1.0 + lax.erf(x * (0.5 ** 0.5)))


def _peer_body(xt_ref, u_ref, vt_ref, s1_ref, s2_ref, e1_ref, e2_ref, tau_ref, h2_ref, o_ref,
               hid_s, coef_s, acc_s):
    e = pl.program_id(1)
    eb, tn = hid_s.shape
    n_i = eb // PEER_KEYS

    @pl.when(e == 0)
    def _init():
        acc_s[...] = jnp.zeros_like(acc_s)

    hid_s[...] = _dot(u_ref[...], xt_ref[...])

    i0 = pl.multiple_of(e * n_i, n_i)

    def cbody(cb, carry):
        cs = pl.ds(pl.multiple_of(cb * LANES, LANES), LANES)
        tau = tau_ref[:, cs]
        for j in range(n_i):
            g = jnp.zeros((PEER_KEYS, LANES), F32)
            for hh in range(PEER_HEADS):
                rs = s1_ref[hh, pl.ds(i0, n_i), cs][j:j + 1, :]
                ra = e1_ref[hh, pl.ds(i0, n_i), cs][j:j + 1, :]
                sel = (s2_ref[hh, :, cs] + rs) >= tau[hh:hh + 1, :]
                g = g + jnp.where(sel, e2_ref[hh, :, cs], 0.0) * ra
            rows = slice(j * PEER_KEYS, (j + 1) * PEER_KEYS)
            coef_s[rows, cs] = (g * _gelu(hid_s[rows, cs])).astype(BF16)
        return carry

    lax.fori_loop(0, tn // LANES, cbody, 0)
    acc_s[...] += _dot(vt_ref[...], coef_s[...])

    @pl.when(e == pl.num_programs(1) - 1)
    def _fin():
        o_ref[...] = h2_ref[...] + acc_s[...].T


def _peer(xt, u, vt, s1, s2, e1, e2, tau, h2):
    d, t = xt.shape
    n_exp = u.shape[0]
    tn = min(512, t)
    eb = 1024
    hk = pl.BlockSpec((PEER_HEADS, PEER_KEYS, tn), lambda i, e: (0, 0, i))
    return pl.pallas_call(
        _peer_body,
        grid=(t // tn, n_exp // eb),
        in_specs=[pl.BlockSpec((d, tn), lambda i, e: (0, i)),
                  pl.BlockSpec((eb, d), lambda i, e: (e, 0)),
                  pl.BlockSpec((d, eb), lambda i, e: (0, e)),
                  hk, hk, hk, hk,
                  pl.BlockSpec((PEER_HEADS, tn), lambda i, e: (0, i)),
                  pl.BlockSpec((tn, d), lambda i, e: (i, 0))],
        out_specs=pl.BlockSpec((tn, d), lambda i, e: (i, 0)),
        out_shape=jax.ShapeDtypeStruct((t, d), F32),
        scratch_shapes=[pltpu.VMEM((eb, tn), F32),
                        pltpu.VMEM((eb, tn), BF16),
                        pltpu.VMEM((d, tn), F32)],
        compiler_params=_cparams("parallel", "arbitrary"),
    )(xt, u, vt, s1, s2, e1, e2, tau, h2)


def _pad_lanes(v, offset):
    return jnp.zeros((1, LANES), F32).at[0, offset:offset + v.shape[0]].set(v.astype(F32))


def _layer(h, l, near_bias, far_bias, attn_norm, w_in, gdn_conv, gdn_a_log, gdn_dt_bias, gdn_out_norm,
           sc_conv, diff_q_norm, diff_k_norm, diff_lambda, diff_subln, w_out, ffn_norm,
           peer_wq, peer_keys, peer_u, peer_v, batch):
    t, d = h.shape
    s = t // batch
    lambda_init = 0.8 - 0.6 * math.exp(-0.3 * l)
    n_main = W_QKVZ + W_SCD
    wi = w_in[l]
    w_r = jnp.concatenate([wi[:, 0:W_QKVZ], wi[:, W_QKVZ + 2 * GDN_HEADS:], wi[:, W_QKVZ:W_QKVZ + 2 * GDN_HEADS],
                           jnp.zeros((d, W_BA - 2 * GDN_HEADS), wi.dtype)], axis=1).astype(BF16)
    assert w_r.shape[1] == n_main + W_BA
    qkvz, scd, ba = _inproj(h, attn_norm[l][None, :], w_r)

    prm = jnp.concatenate([_pad_lanes(gdn_a_log[l], GDN_HEADS), _pad_lanes(gdn_dt_bias[l], GDN_HEADS),
                           jnp.zeros((6, LANES), F32)], axis=0)
    y_gdn = _gdn(qkvz.reshape(batch, s, W_QKVZ), ba.reshape(batch, s, W_BA), gdn_conv[l].astype(F32), prm,
                 gdn_out_norm[l][None, :].astype(F32))

    y_att = _attn(scd.reshape(batch, s, W_SCD), sc_conv[l].astype(F32),
                  jnp.tile(diff_q_norm[l], 256 // DIFF_DQK)[None, :].astype(F32),
                  jnp.tile(diff_k_norm[l], 256 // DIFF_DQK)[None, :].astype(F32),
                  diff_lambda[l].astype(F32),
                  jnp.tile(diff_subln[l], LANES // DIFF_DV)[None, :].astype(F32),
                  near_bias, far_bias, lambda_init)

    h2, xt = _outproj(y_gdn.reshape(t, -1), y_att.reshape(t, -1), w_out[l].astype(BF16), h, ffn_norm[l][None, :])

    s1, s2, e1, e2, tau = _route(xt, peer_wq[l].T.astype(BF16), peer_keys[l].astype(BF16))
    return _peer(xt, peer_u[l].astype(BF16), peer_v[l].T.astype(BF16), s1, s2, e1, e2, tau, h2)


def kernel(x, rel_bias, attn_norm, w_in, gdn_conv, gdn_a_log, gdn_dt_bias, gdn_out_norm, sc_conv,
           diff_q_norm, diff_k_norm, diff_lambda, diff_subln, w_out, ffn_norm, peer_wq, peer_keys,
           peer_u, peer_v):
    batch, s, d = x.shape
    near_bias, far_bias = _bias_tables(rel_bias)
    h = x.reshape(batch * s, d)
    for l in range(w_in.shape[0]):
        h = _layer(h, l, near_bias, far_bias, attn_norm, w_in, gdn_conv, gdn_a_log, gdn_dt_bias,
                   gdn_out_norm, sc_conv, diff_q_norm, diff_k_norm, diff_lambda, diff_subln, w_out,
                   ffn_norm, peer_wq, peer_keys, peer_u, peer_v, batch)
    return h.reshape(batch, s, d)
```

```python
import functools
import math

import jax
import jax.numpy as jnp
import numpy as np
from jax import lax
from jax.experimental import pallas as pl
from jax.experimental.pallas import tpu as pltpu

F32 = jnp.float32
BF16 = jnp.bfloat16
EPS = 1e-6
NEG = -1e30

D_MODEL = 1024
GDN_HEADS = 4
GDN_D = 128
GDN_CONV = 4
GDN_CHUNK = 64
SC_WIDTH = 256
SC_CONV = 3
DIFF_HEADS = 4
DIFF_DV = 64
DIFF_DQK = 32
ATT_BLK = 256
REL_BUCKETS = 32
REL_MAX_DIST = 128
PEER_HEADS = 8
PEER_KEYS = 128
PEER_TOPK = 16
PEER_DHALF = 128
LANES = 128
VMEM_LIMIT = 56 * 1024 * 1024

W_QKVZ = 2048
W_SCD = 1536
W_BA = LANES


def _cparams(*sem):
    return pltpu.CompilerParams(dimension_semantics=sem, vmem_limit_bytes=VMEM_LIMIT)


def _nt_dot(a, b):
    return lax.dot_general(a, b, (((1,), (1,)), ((), ())), preferred_element_type=F32)


def _tn_dot(a, b):
    return lax.dot_general(a, b, (((0,), (0,)), ((), ())), preferred_element_type=F32)


def _dot(a, b):
    return jnp.dot(a, b, preferred_element_type=F32)


def _split3(x):
    hi = x.astype(BF16)
    r = x - hi.astype(F32)
    mid = r.astype(BF16)
    lo = (r - mid.astype(F32)).astype(BF16)
    return hi, mid, lo


def _inproj_body(h_ref, g_ref, w_ref, qkvz_ref, scd_ref, ba_ref):
    x = h_ref[...]
    n = x * lax.rsqrt(jnp.mean(x * x, axis=-1, keepdims=True) + EPS) * g_ref[...]
    nb = n.astype(BF16)
    qkvz_ref[...] = _dot(nb, w_ref[:, 0:W_QKVZ]).astype(BF16)
    scd_ref[...] = _dot(nb, w_ref[:, W_QKVZ:W_QKVZ + W_SCD]).astype(BF16)
    ba_ref[...] = _dot(nb, w_ref[:, W_QKVZ + W_SCD:])


def _inproj(h, gain, w):
    t, d = h.shape
    tm = min(512, t)
    nw = w.shape[1]
    return pl.pallas_call(
        _inproj_body,
        grid=(t // tm,),
        in_specs=[pl.BlockSpec((tm, d), lambda i: (i, 0)),
                  pl.BlockSpec((1, d), lambda i: (0, 0)),
                  pl.BlockSpec((d, nw), lambda i: (0, 0))],
        out_specs=[pl.BlockSpec((tm, W_QKVZ), lambda i: (i, 0)),
                   pl.BlockSpec((tm, W_SCD), lambda i: (i, 0)),
                   pl.BlockSpec((tm, W_BA), lambda i: (i, 0))],
        out_shape=[jax.ShapeDtypeStruct((t, W_QKVZ), BF16),
                   jax.ShapeDtypeStruct((t, W_SCD), BF16),
                   jax.ShapeDtypeStruct((t, W_BA), F32)],
        compiler_params=_cparams("parallel"),
    )(h, gain, w)


def _gdn_body(qkvz_ref, ba_ref, conv_ref, prm_ref, gain_ref, y_ref, q_s, k_s, kb_s, vb_s, gb_s, *st_s):
    s = qkvz_ref.shape[0]
    c_sz = GDN_CHUNK
    row = lax.broadcasted_iota(jnp.int32, (s, LANES), 0)

    ba = ba_ref[...]
    beta = jax.nn.sigmoid(ba)
    xg = ba + prm_ref[1:2, :]
    softplus = jnp.maximum(xg, 0.0) + jnp.log(1.0 + jnp.exp(-jnp.abs(xg)))
    g = -jnp.exp(prm_ref[0:1, :]) * softplus
    pos = row % c_sz
    for sh in (1, 2, 4, 8, 16, 32):
        g = g + jnp.where(pos >= sh, pltpu.roll(g, sh, 0), 0.0)
    for hh in range(GDN_HEADS):
        gb_s[hh] = jnp.broadcast_to(g[:, GDN_HEADS + hh:GDN_HEADS + hh + 1], (s, LANES))

    scale = GDN_D ** -0.5
    for cb in range(3 * GDN_HEADS):
        x = qkvz_ref[:, cb * LANES:(cb + 1) * LANES].astype(F32)
        w = conv_ref[:, cb * LANES:(cb + 1) * LANES]
        acc = x * w[GDN_CONV - 1:GDN_CONV, :]
        for j in range(GDN_CONV - 1):
            sh = GDN_CONV - 1 - j
            acc = acc + jnp.where(row >= sh, pltpu.roll(x, sh, 0), 0.0) * w[j:j + 1, :]
        y = acc * jax.nn.sigmoid(acc)
        kind, hh = divmod(cb, GDN_HEADS)
        if kind < 2:
            y = y * lax.rsqrt(jnp.sum(y * y, axis=-1, keepdims=True) + EPS)
        if kind == 0:
            q_s[hh] = y * scale
        elif kind == 1:
            k_s[hh] = y
            kb_s[hh] = y * beta[:, hh:hh + 1]
        else:
            vb_s[hh] = (y * beta[:, hh:hh + 1]).astype(BF16)

    for st in st_s:
        st[...] = jnp.zeros_like(st)
    ii = lax.broadcasted_iota(jnp.int32, (c_sz, c_sz), 0)
    jj = lax.broadcasted_iota(jnp.int32, (c_sz, c_sz), 1)
    tril = ii >= jj
    strict = ii > jj
    eye = (ii == jj).astype(F32)
    lane = lax.broadcasted_iota(jnp.int32, (c_sz, LANES), 1)
    pick3 = (lane < 3).astype(BF16)

    def chunk(c, carry):
        r0 = pl.multiple_of(c * c_sz, c_sz)
        for hh in range(GDN_HEADS):
            gc = gb_s[hh, pl.ds(r0, c_sz), :]
            glast = gc[c_sz - 1:c_sz, :]
            eg = jnp.exp(gc)
            qc = q_s[hh, pl.ds(r0, c_sz), :]
            kc = k_s[hh, pl.ds(r0, c_sz), :]
            kb = kb_s[hh, pl.ds(r0, c_sz), :]
            vb = vb_s[hh, pl.ds(r0, c_sz), :]
            hi, mid, lo = _split3(gc)
            x3 = jnp.where(lane == 0, hi, jnp.where(lane == 1, mid, jnp.where(lane == 2, lo, jnp.zeros_like(lo))))
            grow = _nt_dot(pick3, x3)
            decay = jnp.where(tril, jnp.exp(jnp.minimum(gc[:, 0:c_sz] - grow, 0.0)), 0.0)
            kcb = kc.astype(BF16)
            lower = jnp.where(strict, _nt_dot(kb.astype(BF16), kcb) * decay, 0.0)
            pw = -lower
            tm = eye + pw
            for _ in range(5):
                pwb = pw.astype(BF16)
                pw = _dot(pwb, pwb)
                tm = tm + _dot(tm.astype(BF16), pw.astype(BF16))
            tmb = tm.astype(BF16)
            u = _dot(tmb, vb)
            w = _dot(tmb, (kb * eg).astype(BF16))
            a_in = jnp.where(tril, _nt_dot(qc.astype(BF16), kcb) * decay, 0.0)
            state = st_s[hh][...]
            stb = state.astype(BF16)
            v_new = u - _dot(w.astype(BF16), stb)
            vnb = v_new.astype(BF16)
            o = _dot((qc * eg).astype(BF16), stb) + _dot(a_in.astype(BF16), vnb)
            st_s[hh][...] = state * jnp.exp(glast) + _tn_dot((kc * jnp.exp(glast - gc)).astype(BF16), vnb)
            q_s[hh, pl.ds(r0, c_sz), :] = o
        return carry

    lax.fori_loop(0, s // c_sz, chunk, 0)

    gain = gain_ref[...]
    for hh in range(GDN_HEADS):
        o = q_s[hh]
        z = qkvz_ref[:, (3 * GDN_HEADS + hh) * LANES:(3 * GDN_HEADS + hh + 1) * LANES].astype(F32)
        on = o * lax.rsqrt(jnp.mean(o * o, axis=-1, keepdims=True) + EPS) * gain
        y_ref[:, hh * LANES:(hh + 1) * LANES] = (on * (z * jax.nn.sigmoid(z))).astype(BF16)


def _gdn(qkvz, ba, conv_w, prm, gain):
    b, s, _ = qkvz.shape
    hd = GDN_HEADS * GDN_D
    return pl.pallas_call(
        _gdn_body,
        grid=(b,),
        in_specs=[pl.BlockSpec((None, s, W_QKVZ), lambda i: (i, 0, 0)),
                  pl.BlockSpec((None, s, W_BA), lambda i: (i, 0, 0)),
                  pl.BlockSpec((GDN_CONV, 3 * hd), lambda i: (0, 0)),
                  pl.BlockSpec((8, LANES), lambda i: (0, 0)),
                  pl.BlockSpec((1, GDN_D), lambda i: (0, 0))],
        out_specs=pl.BlockSpec((None, s, hd), lambda i: (i, 0, 0)),
        out_shape=jax.ShapeDtypeStruct((b, s, hd), BF16),
        scratch_shapes=[pltpu.VMEM((GDN_HEADS, s, GDN_D), F32),
                        pltpu.VMEM((GDN_HEADS, s, GDN_D), F32),
                        pltpu.VMEM((GDN_HEADS, s, GDN_D), F32),
                        pltpu.VMEM((GDN_HEADS, s, GDN_D), BF16),
                        pltpu.VMEM((GDN_HEADS, s, LANES), F32),
                        ] + [pltpu.VMEM((GDN_D, GDN_D), F32) for _ in range(GDN_HEADS)],
        compiler_params=_cparams("parallel"),
    )(qkvz, ba, conv_w, prm, gain)


def _attn_body(far_ref, scd_ref, scw_ref, qg_ref, kg_ref, lam_ref, sub_ref, nbt_ref, y_ref,
               qt_s, kn_s, vt_s, ysc_s, m_s, l_s, acc_s, *, lambda_init):
    i = pl.program_id(1)
    s = scd_ref.shape[0]
    blk = ATT_BLK
    o_q, o_k, o_v = 3 * SC_WIDTH, 3 * SC_WIDTH + 256, 3 * SC_WIDTH + 512

    @pl.when(i == 0)
    def _prep():
        row = lax.broadcasted_iota(jnp.int32, (s, SC_WIDTH), 0)
        gate_b = scd_ref[:, 0:SC_WIDTH].astype(F32)
        x = scd_ref[:, SC_WIDTH:2 * SC_WIDTH].astype(F32) * scd_ref[:, 2 * SC_WIDTH:3 * SC_WIDTH].astype(F32)
        w = scw_ref[...]
        acc = x * w[SC_CONV - 1:SC_CONV, :]
        for j in range(SC_CONV - 1):
            sh = SC_CONV - 1 - j
            acc = acc + jnp.where(row >= sh, pltpu.roll(x, sh, 0), 0.0) * w[j:j + 1, :]
        ysc_s[...] = (gate_b * acc).astype(BF16)
        gi = lax.broadcasted_iota(jnp.int32, (256, 256), 0) // DIFF_DQK
        gj = lax.broadcasted_iota(jnp.int32, (256, 256), 1) // DIFF_DQK
        bd = (gi == gj).astype(BF16)

        def normed(off, g_ref, sc):
            xx = scd_ref[:, off:off + 256].astype(F32)
            hi, mid, lo = _split3(xx * xx)
            ss = _dot(hi, bd) + _dot(mid, bd) + _dot(lo, bd)
            return xx * lax.rsqrt(ss * (1.0 / DIFF_DQK) + EPS) * g_ref[...] * sc

        qt_s[...] = normed(o_q, qg_ref, DIFF_DQK ** -0.5).T.astype(BF16)
        kn_s[...] = normed(o_k, kg_ref, 1.0).astype(BF16)
        vt_s[...] = scd_ref[:, o_v:o_v + 256].astype(F32).T.astype(BF16)

    lp = lam_ref[...]
    lam = (jnp.exp(jnp.sum(lp[0:1, :] * lp[1:2, :], axis=-1, keepdims=True))
           - jnp.exp(jnp.sum(lp[2:3, :] * lp[3:4, :], axis=-1, keepdims=True)) + lambda_init)

    r0 = pl.multiple_of(i * blk, blk)
    rowi = lax.broadcasted_iota(jnp.int32, (LANES, blk), 0)
    grp = rowi // DIFF_DQK

    m_s[...] = jnp.full_like(m_s, NEG)
    l_s[...] = jnp.zeros_like(l_s)
    acc_s[...] = jnp.zeros_like(acc_s)

    def q_masked(hp):
        qp = qt_s[hp * LANES:(hp + 1) * LANES, pl.ds(r0, blk)]
        return [jnp.where(grp == sidx, qp, jnp.zeros_like(qp)) for sidx in range(4)]

    def update(st, logits, vt):
        m_old = m_s[st:st + 1, :]
        m_new = jnp.maximum(m_old, jnp.max(logits, axis=0, keepdims=True))
        alpha = jnp.exp(m_old - m_new)
        p = jnp.exp(logits - m_new)
        l_s[st:st + 1, :] = alpha * l_s[st:st + 1, :] + jnp.sum(p, axis=0, keepdims=True)
        acc_s[st] = alpha * acc_s[st] + _dot(vt, p.astype(BF16))
        m_s[st:st + 1, :] = m_new

    def block(c0, bias_of):
        for hp in range(2):
            kp = kn_s[pl.ds(c0, blk), hp * LANES:(hp + 1) * LANES]
            vt = vt_s[hp * LANES:(hp + 1) * LANES, pl.ds(c0, blk)]
            qs = q_masked(hp)
            for sidx in range(4):
                update(hp * 4 + sidx, _dot(kp, qs[sidx]) + bias_of(hp * 2 + sidx // 2), vt)

    def far_body(kb, carry):
        block(pl.multiple_of(kb * blk, blk), lambda head: far_ref[head])
        return carry

    lax.fori_loop(0, jnp.maximum(i - 1, 0), far_body, 0)

    @pl.when(i >= 1)
    def _prev():
        block(pl.multiple_of((i - 1) * blk, blk), lambda head: nbt_ref[head, 0:blk, :])

    block(r0, lambda head: nbt_ref[head, blk:2 * blk, :])

    low = rowi < DIFF_DV
    for hp in range(2):
        outs = []
        for hh in range(2):
            s0 = hp * 4 + hh * 2
            outs.append(acc_s[s0] / l_s[s0:s0 + 1, :] - lam * (acc_s[s0 + 1] / l_s[s0 + 1:s0 + 2, :]))
        o = jnp.where(low, outs[0], outs[1])
        sq = o * o
        ss0 = jnp.sum(jnp.where(low, sq, 0.0), axis=0, keepdims=True)
        ss1 = jnp.sum(jnp.where(low, 0.0, sq), axis=0, keepdims=True)
        ms = jnp.where(low, ss0, ss1) * (1.0 / DIFF_DV)
        y = (o * lax.rsqrt(ms + EPS)).T * sub_ref[...] * (1.0 - lambda_init)
        y_ref[:, SC_WIDTH + hp * LANES:SC_WIDTH + (hp + 1) * LANES] = y.astype(BF16)
    y_ref[:, 0:SC_WIDTH] = ysc_s[pl.ds(r0, blk), :]


def _attn(scd, sc_w, q_gain, k_gain, lam_p, sub_gain, near_bias_t, far_bias, lambda_init):
    b, s, _ = scd.shape
    blk = ATT_BLK
    const2 = lambda bi, i: (0, 0)
    const3 = lambda bi, i: (0, 0, 0)
    return pl.pallas_call(
        functools.partial(_attn_body, lambda_init=lambda_init),
        grid=(b, s // blk),
        in_specs=[pl.BlockSpec(memory_space=pltpu.SMEM),
                  pl.BlockSpec((None, s, W_SCD), lambda bi, i: (bi, 0, 0)),
                  pl.BlockSpec((SC_CONV, SC_WIDTH), const2),
                  pl.BlockSpec((1, 256), const2),
                  pl.BlockSpec((1, 256), const2),
                  pl.BlockSpec((4, DIFF_DQK), const2),
                  pl.BlockSpec((1, LANES), const2),
                  pl.BlockSpec((DIFF_HEADS, 2 * blk, blk), const3)],
        out_specs=pl.BlockSpec((None, blk, 512), lambda bi, i: (bi, i, 0)),
        out_shape=jax.ShapeDtypeStruct((b, s, 512), BF16),
        scratch_shapes=[pltpu.VMEM((256, s), BF16),
                        pltpu.VMEM((s, 256), BF16),
                        pltpu.VMEM((256, s), BF16),
                        pltpu.VMEM((s, SC_WIDTH), BF16),
                        pltpu.VMEM((8, blk), F32),
                        pltpu.VMEM((8, blk), F32),
                        pltpu.VMEM((8, LANES, blk), F32)],
        compiler_params=_cparams("parallel", "arbitrary"),
    )(far_bias, scd, sc_w, q_gain, k_gain, lam_p, sub_gain, near_bias_t)


def _rel_bucket(rel):
    max_exact = REL_BUCKETS // 2
    n = jnp.maximum(rel, 0)
    large = max_exact + (jnp.log(jnp.maximum(n, max_exact).astype(F32) / max_exact)
                         / math.log(REL_MAX_DIST / max_exact) * (REL_BUCKETS - max_exact)).astype(jnp.int32)
    large = jnp.minimum(large, REL_BUCKETS - 1)
    return jnp.where(n < max_exact, n, large)


def _bias_tables(rel_bias):
    blk = ATT_BLK
    rel = jnp.arange(blk)[:, None] + blk - jnp.arange(2 * blk)[None, :]
    onehot = (_rel_bucket(rel)[None] == jnp.arange(REL_BUCKETS)[:, None, None]).astype(F32)
    near = jnp.einsum("brc,bh->hrc", onehot, rel_bias.astype(F32), precision=lax.Precision.HIGHEST)
    near = jnp.where(rel[None] >= 0, near, NEG)
    return jnp.swapaxes(near, 1, 2), rel_bias[REL_BUCKETS - 1].astype(F32)


def _outproj_body(yg_ref, ya_ref, w_ref, h_ref, g_ref, h2_ref, xt_ref):
    hd = yg_ref.shape[1]
    h2 = h_ref[...] + _dot(yg_ref[...], w_ref[0:hd, :]) + _dot(ya_ref[...], w_ref[hd:, :])
    h2_ref[...] = h2
    n = h2 * lax.rsqrt(jnp.mean(h2 * h2, axis=-1, keepdims=True) + EPS) * g_ref[...]
    xt_ref[...] = n.T.astype(BF16)


def _outproj(yg, ya, w, h, gain):
    t, d = h.shape
    tm = min(512, t)
    return pl.pallas_call(
        _outproj_body,
        grid=(t // tm,),
        in_specs=[pl.BlockSpec((tm, yg.shape[1]), lambda i: (i, 0)),
                  pl.BlockSpec((tm, ya.shape[1]), lambda i: (i, 0)),
                  pl.BlockSpec((d, d), lambda i: (0, 0)),
                  pl.BlockSpec((tm, d), lambda i: (i, 0)),
                  pl.BlockSpec((1, d), lambda i: (0, 0))],
        out_specs=[pl.BlockSpec((tm, d), lambda i: (i, 0)),
                   pl.BlockSpec((d, tm), lambda i: (0, i))],
        out_shape=[jax.ShapeDtypeStruct((t, d), F32),
                   jax.ShapeDtypeStruct((d, t), BF16)],
        compiler_params=_cparams("parallel"),
    )(yg, ya, w, h, gain)


def _cmpx(lst, i, j):
    a, b = lst[i], lst[j]
    lst[i] = jnp.maximum(a, b)
    lst[j] = jnp.minimum(a, b)


def _bitonic_clean(lst, lo, n):
    d = n // 2
    while d >= 1:
        for k in range(n):
            if (k // d) % 2 == 0:
                _cmpx(lst, lo + k, lo + k + d)
        d //= 2


def _sort_desc(lst, lo, n):
    if n == 1:
        return
    h = n // 2
    _sort_desc(lst, lo, h)
    _sort_desc(lst, lo + h, h)
    for k in range(h):
        _cmpx(lst, lo + k, lo + n - 1 - k)
    _bitonic_clean(lst, lo, h)
    _bitonic_clean(lst, lo + h, h)


def _merge_sublanes(lst):
    n = len(lst)
    for d in (4, 2, 1):
        lst = [jnp.maximum(lst[k], pltpu.roll(lst[n - 1 - k], d, 0)) for k in range(n)]
        _bitonic_clean(lst, 0, n)
    return lst


def _top_sorted(s):
    lst = [s[k * 8:(k + 1) * 8, :] for k in range(s.shape[0] // 8)]
    _sort_desc(lst, 0, len(lst))
    return _merge_sublanes(lst)


def _route_body(xt_ref, wqt_ref, keys_ref, s1_ref, s2_ref, e1_ref, e2_ref, tau_ref, q_s):
    k = PEER_TOPK
    tn = xt_ref.shape[1]
    q_s[...] = _dot(wqt_ref[...], xt_ref[...]).astype(BF16)
    sub = lax.broadcasted_iota(jnp.int32, (8, tn), 0)
    for hh in range(PEER_HEADS):
        sc = []
        for p in range(2):
            r = (hh * 2 + p) * PEER_DHALF
            sc.append(_dot(keys_ref[hh, p], q_s[r:r + PEER_DHALF, :]))
        a = _top_sorted(sc[0])
        b = _top_sorted(sc[1])
        apack, bpack = a[0], b[0]
        for r in range(1, 8):
            apack = jnp.where(sub == r, a[r], apack)
            bpack = jnp.where(sub == r, b[r], bpack)
        cand = [apack + b[i] for i in range(k)]
        extra = [a[8 + i] + bpack for i in range(k - 8)]
        for i in range(8, k):
            cand[i] = jnp.maximum(cand[i], extra[k - 1 - i])
        _bitonic_clean(cand, 0, k)
        best = _merge_sublanes(cand)
        z = jnp.zeros_like(best[0])
        for i in range(k):
            z = z + jnp.exp(best[i] - best[0])
        s1_ref[hh] = sc[0]
        s2_ref[hh] = sc[1]
        e1_ref[hh] = jnp.exp(sc[0] - a[0][0:1, :])
        e2_ref[hh] = jnp.exp(sc[1] - b[0][0:1, :]) / z[0:1, :]
        tau_ref[hh:hh + 1, :] = best[k - 1][0:1, :]


def _route(xt, wqt, keys):
    d, t = xt.shape
    tn = min(512, t)
    nq = wqt.shape[0]
    hk = (PEER_HEADS, PEER_KEYS, tn)
    spec = pl.BlockSpec(hk, lambda i: (0, 0, i))
    shp = jax.ShapeDtypeStruct((PEER_HEADS, PEER_KEYS, t), F32)
    return pl.pallas_call(
        _route_body,
        grid=(t // tn,),
        in_specs=[pl.BlockSpec((d, tn), lambda i: (0, i)),
                  pl.BlockSpec((nq, d), lambda i: (0, 0)),
                  pl.BlockSpec((PEER_HEADS, 2, PEER_KEYS, PEER_DHALF), lambda i: (0, 0, 0, 0))],
        out_specs=[spec, spec, spec, spec, pl.BlockSpec((PEER_HEADS, tn), lambda i: (0, i))],
        out_shape=[shp, shp, shp, shp, jax.ShapeDtypeStruct((PEER_HEADS, t), F32)],
        scratch_shapes=[pltpu.VMEM((nq, tn), BF16)],
        compiler_params=_cparams("parallel"),
    )(xt, wqt, keys)


def _gelu(x):
    return 0.5 * x * (1.0 + lax.erf(x * (0.5 ** 0.5)))


def _peer_body(xt_ref, u_ref, vt_ref, s1_ref, s2_ref, e1_ref, e2_ref, tau_ref, h2_ref, o_ref,
               hid_s, coef_s, acc_s):
    e = pl.program_id(1)
    eb, tn = hid_s.shape
    n_i = eb // PEER_KEYS

    @pl.when(e == 0)
    def _init():
        acc_s[...] = jnp.zeros_like(acc_s)

    hid_s[...] = _dot(u_ref[...], xt_ref[...])

    i0 = pl.multiple_of(e * n_i, n_i)

    def cbody(cb, carry):
        cs = pl.ds(pl.multiple_of(cb * LANES, LANES), LANES)
        tau = tau_ref[:, cs]
        for j in range(n_i):
            g = jnp.zeros((PEER_KEYS, LANES), F32)
            for hh in range(PEER_HEADS):
                rs = s1_ref[hh, pl.ds(i0, n_i), cs][j:j + 1, :]
                ra = e1_ref[hh, pl.ds(i0, n_i), cs][j:j + 1, :]
                sel = (s2_ref[hh, :, cs] + rs) >= tau[hh:hh + 1, :]
                g = g + jnp.where(sel, e2_ref[hh, :, cs], 0.0) * ra
            rows = slice(j * PEER_KEYS, (j + 1) * PEER_KEYS)
            coef_s[rows, cs] = (g * _gelu(hid_s[rows, cs])).astype(BF16)
        return carry

    lax.fori_loop(0, tn // LANES, cbody, 0)
    acc_s[...] += _dot(vt_ref[...], coef_s[...])

    @pl.when(e == pl.num_programs(1) - 1)
    def _fin():
        o_ref[...] = h2_ref[...] + acc_s[...].T


def _peer(xt, u, vt, s1, s2, e1, e2, tau, h2):
    d, t = xt.shape
    n_exp = u.shape[0]
    tn = min(512, t)
    eb = 1024
    hk = pl.BlockSpec((PEER_HEADS, PEER_KEYS, tn), lambda i, e: (0, 0, i))
    return pl.pallas_call(
        _peer_body,
        grid=(t // tn, n_exp // eb),
        in_specs=[pl.BlockSpec((d, tn), lambda i, e: (0, i)),
                  pl.BlockSpec((eb, d), lambda i, e: (e, 0)),
                  pl.BlockSpec((d, eb), lambda i, e: (0, e)),
                  hk, hk, hk, hk,
                  pl.BlockSpec((PEER_HEADS, tn), lambda i, e: (0, i)),
                  pl.BlockSpec((tn, d), lambda i, e: (i, 0))],
        out_specs=pl.BlockSpec((tn, d), lambda i, e: (i, 0)),
        out_shape=jax.ShapeDtypeStruct((t, d), F32),
        scratch_shapes=[pltpu.VMEM((eb, tn), F32),
                        pltpu.VMEM((eb, tn), BF16),
                        pltpu.VMEM((d, tn), F32)],
        compiler_params=_cparams("parallel", "arbitrary"),
    )(xt, u, vt, s1, s2, e1, e2, tau, h2)


def _pad_lanes(v, offset):
    return jnp.zeros((1, LANES), F32).at[0, offset:offset + v.shape[0]].set(v.astype(F32))


def _layer(h, l, near_bias, far_bias, attn_norm, w_in, gdn_conv, gdn_a_log, gdn_dt_bias, gdn_out_norm,
           sc_conv, diff_q_norm, diff_k_norm, diff_lambda, diff_subln, w_out, ffn_norm,
           peer_wq, peer_keys, peer_u, peer_v, batch):
    t, d = h.shape
    s = t // batch
    lambda_init = 0.8 - 0.6 * math.exp(-0.3 * l)
    n_main = W_QKVZ + W_SCD
    wi = w_in[l]
    w_r = jnp.concatenate([wi[:, 0:W_QKVZ], wi[:, W_QKVZ + 2 * GDN_HEADS:], wi[:, W_QKVZ:W_QKVZ + 2 * GDN_HEADS],
                           jnp.zeros((d, W_BA - 2 * GDN_HEADS), wi.dtype)], axis=1).astype(BF16)
    assert w_r.shape[1] == n_main + W_BA
    qkvz, scd, ba = _inproj(h, attn_norm[l][None, :], w_r)

    prm = jnp.concatenate([_pad_lanes(gdn_a_log[l], GDN_HEADS), _pad_lanes(gdn_dt_bias[l], GDN_HEADS),
                           jnp.zeros((6, LANES), F32)], axis=0)
    y_gdn = _gdn(qkvz.reshape(batch, s, W_QKVZ), ba.reshape(batch, s, W_BA), gdn_conv[l].astype(F32), prm,
                 gdn_out_norm[l][None, :].astype(F32))

    y_att = _attn(scd.reshape(batch, s, W_SCD), sc_conv[l].astype(F32),
                  jnp.tile(diff_q_norm[l], 256 // DIFF_DQK)[None, :].astype(F32),
                  jnp.tile(diff_k_norm[l], 256 // DIFF_DQK)[None, :].astype(F32),
                  diff_lambda[l].astype(F32),
                  jnp.tile(diff_subln[l], LANES // DIFF_DV)[None, :].astype(F32),
                  near_bias, far_bias, lambda_init)

    h2, xt = _outproj(y_gdn.reshape(t, -1), y_att.reshape(t, -1), w_out[l].astype(BF16), h, ffn_norm[l][None, :])

    s1, s2, e1, e2, tau = _route(xt, peer_wq[l].T.astype(BF16), peer_keys[l].astype(BF16))
    return _peer(xt, peer_u[l].astype(BF16), peer_v[l].T.astype(BF16), s1, s2, e1, e2, tau, h2)


def kernel(x, rel_bias, attn_norm, w_in, gdn_conv, gdn_a_log, gdn_dt_bias, gdn_out_norm, sc_conv,
           diff_q_norm, diff_k_norm, diff_lambda, diff_subln, w_out, ffn_norm, peer_wq, peer_keys,
           peer_u, peer_v):
    batch, s, d = x.shape
    near_bias, far_bias = _bias_tables(rel_bias)
    h = x.reshape(batch * s, d)
    for l in range(w_in.shape[0]):
        h = _layer(h, l, near_bias, far_bias, attn_norm, w_in, gdn_conv, gdn_a_log, gdn_dt_bias,
                   gdn_out_norm, sc_conv, diff_q_norm, diff_k_norm, diff_lambda, diff_subln, w_out,
                   ffn_norm, peer_wq, peer_keys, peer_u, peer_v, batch)
    return h.reshape(batch, s, d)
```

```python
import functools
import math

import jax
import jax.numpy as jnp
import numpy as np
from jax import lax
from jax.experimental import pallas as pl
from jax.experimental.pallas import tpu as pltpu

F32 = jnp.float32
BF16 = jnp.bfloat16
EPS = 1e-6
NEG = -1e30

D_MODEL = 1024
GDN_HEADS = 4
GDN_D = 128
GDN_CONV = 4
GDN_CHUNK = 64
SC_WIDTH = 256
SC_CONV = 3
DIFF_HEADS = 4
DIFF_DV = 64
DIFF_DQK = 32
ATT_BLK = 256
REL_BUCKETS = 32
REL_MAX_DIST = 128
PEER_HEADS = 8
PEER_KEYS = 128
PEER_TOPK = 16
PEER_DHALF = 128
LANES = 128
BF16_ROWS = 16
VMEM_LIMIT = 56 * 1024 * 1024

W_QKVZ = 2048
W_SCD = 1536
W_BA = LANES


def _cparams(*sem):
    return pltpu.CompilerParams(dimension_semantics=sem, vmem_limit_bytes=VMEM_LIMIT)


def _nt_dot(a, b):
    return lax.dot_general(a, b, (((1,), (1,)), ((), ())), preferred_element_type=F32)


def _tn_dot(a, b):
    return lax.dot_general(a, b, (((0,), (0,)), ((), ())), preferred_element_type=F32)


def _dot(a, b):
    return jnp.dot(a, b, preferred_element_type=F32)


def _split3(x):
    hi = x.astype(BF16)
    r = x - hi.astype(F32)
    mid = r.astype(BF16)
    lo = (r - mid.astype(F32)).astype(BF16)
    return hi, mid, lo


def _inproj_body(h_ref, g_ref, w_ref, qkvz_ref, scd_ref, ba_ref):
    x = h_ref[...]
    n = x * lax.rsqrt(jnp.mean(x * x, axis=-1, keepdims=True) + EPS) * g_ref[...]
    nb = n.astype(BF16)
    qkvz_ref[...] = _dot(nb, w_ref[:, 0:W_QKVZ]).astype(BF16)
    scd_ref[...] = _dot(nb, w_ref[:, W_QKVZ:W_QKVZ + W_SCD]).astype(BF16)
    ba_ref[...] = _dot(nb, w_ref[:, W_QKVZ + W_SCD:])


def _inproj(h, gain, w):
    t, d = h.shape
    tm = min(512, t)
    nw = w.shape[1]
    return pl.pallas_call(
        _inproj_body,
        grid=(t // tm,),
        in_specs=[pl.BlockSpec((tm, d), lambda i: (i, 0)),
                  pl.BlockSpec((1, d), lambda i: (0, 0)),
                  pl.BlockSpec((d, nw), lambda i: (0, 0))],
        out_specs=[pl.BlockSpec((tm, W_QKVZ), lambda i: (i, 0)),
                   pl.BlockSpec((tm, W_SCD), lambda i: (i, 0)),
                   pl.BlockSpec((tm, W_BA), lambda i: (i, 0))],
        out_shape=[jax.ShapeDtypeStruct((t, W_QKVZ), BF16),
                   jax.ShapeDtypeStruct((t, W_SCD), BF16),
                   jax.ShapeDtypeStruct((t, W_BA), F32)],
        compiler_params=_cparams("parallel"),
    )(h, gain, w)


def _gdn_body(qkvz_ref, ba_ref, conv_ref, prm_ref, gain_ref, y_ref, q_s, k_s, kb_s, vb_s, gb_s, *st_s):
    s = qkvz_ref.shape[0]
    c_sz = GDN_CHUNK
    row = lax.broadcasted_iota(jnp.int32, (s, LANES), 0)

    ba = ba_ref[...]
    beta = jax.nn.sigmoid(ba)
    xg = ba + prm_ref[1:2, :]
    softplus = jnp.maximum(xg, 0.0) + jnp.log(1.0 + jnp.exp(-jnp.abs(xg)))
    g = -jnp.exp(prm_ref[0:1, :]) * softplus
    pos = row % c_sz
    for sh in (1, 2, 4, 8, 16, 32):
        g = g + jnp.where(pos >= sh, pltpu.roll(g, sh, 0), 0.0)
    for hh in range(GDN_HEADS):
        gb_s[hh] = jnp.broadcast_to(g[:, GDN_HEADS + hh:GDN_HEADS + hh + 1], (s, LANES))

    scale = GDN_D ** -0.5
    for cb in range(3 * GDN_HEADS):
        x = qkvz_ref[:, cb * LANES:(cb + 1) * LANES].astype(F32)
        w = conv_ref[:, cb * LANES:(cb + 1) * LANES]
        acc = x * w[GDN_CONV - 1:GDN_CONV, :]
        for j in range(GDN_CONV - 1):
            sh = GDN_CONV - 1 - j
            acc = acc + jnp.where(row >= sh, pltpu.roll(x, sh, 0), 0.0) * w[j:j + 1, :]
        y = acc * jax.nn.sigmoid(acc)
        kind, hh = divmod(cb, GDN_HEADS)
        if kind < 2:
            y = y * lax.rsqrt(jnp.sum(y * y, axis=-1, keepdims=True) + EPS)
        if kind == 0:
            q_s[hh] = y * scale
        elif kind == 1:
            k_s[hh] = y
            kb_s[hh] = y * beta[:, hh:hh + 1]
        else:
            vb_s[hh] = (y * beta[:, hh:hh + 1]).astype(BF16)

    for st in st_s:
        st[...] = jnp.zeros_like(st)
    ii = lax.broadcasted_iota(jnp.int32, (c_sz, c_sz), 0)
    jj = lax.broadcasted_iota(jnp.int32, (c_sz, c_sz), 1)
    tril = ii >= jj
    strict = ii > jj
    eye = (ii == jj).astype(F32)
    lane = lax.broadcasted_iota(jnp.int32, (c_sz, LANES), 1)
    pick3 = (lane < 3).astype(BF16)

    def chunk(c, carry):
        r0 = pl.multiple_of(c * c_sz, c_sz)
        for hh in range(GDN_HEADS):
            gc = gb_s[hh, pl.ds(r0, c_sz), :]
            glast = gc[c_sz - 1:c_sz, :]
            eg = jnp.exp(gc)
            qc = q_s[hh, pl.ds(r0, c_sz), :]
            kc = k_s[hh, pl.ds(r0, c_sz), :]
            kb = kb_s[hh, pl.ds(r0, c_sz), :]
            vb = vb_s[hh, pl.ds(r0, c_sz), :]
            hi, mid, lo = _split3(gc)
            x3 = jnp.where(lane == 0, hi, jnp.where(lane == 1, mid, jnp.where(lane == 2, lo, jnp.zeros_like(lo))))
            grow = _nt_dot(pick3, x3)
            decay = jnp.where(tril, jnp.exp(jnp.minimum(gc[:, 0:c_sz] - grow, 0.0)), 0.0)
            kcb = kc.astype(BF16)
            lower = jnp.where(strict, _nt_dot(kb.astype(BF16), kcb) * decay, 0.0)
            pw = -lower
            tm = eye + pw
            for _ in range(5):
                pwb = pw.astype(BF16)
                pw = _dot(pwb, pwb)
                tm = tm + _dot(tm.astype(BF16), pw.astype(BF16))
            tmb = tm.astype(BF16)
            u = _dot(tmb, vb)
            w = _dot(tmb, (kb * eg).astype(BF16))
            a_in = jnp.where(tril, _nt_dot(qc.astype(BF16), kcb) * decay, 0.0)
            state = st_s[hh][...]
            stb = state.astype(BF16)
            v_new = u - _dot(w.astype(BF16), stb)
            vnb = v_new.astype(BF16)
            o = _dot((qc * eg).astype(BF16), stb) + _dot(a_in.astype(BF16), vnb)
            st_s[hh][...] = state * jnp.exp(glast) + _tn_dot((kc * jnp.exp(glast - gc)).astype(BF16), vnb)
            q_s[hh, pl.ds(r0, c_sz), :] = o
        return carry

    lax.fori_loop(0, s // c_sz, chunk, 0)

    gain = gain_ref[...]
    for hh in range(GDN_HEADS):
        o = q_s[hh]
        z = qkvz_ref[:, (3 * GDN_HEADS + hh) * LANES:(3 * GDN_HEADS + hh + 1) * LANES].astype(F32)
        on = o * lax.rsqrt(jnp.mean(o * o, axis=-1, keepdims=True) + EPS) * gain
        y_ref[:, hh * LANES:(hh + 1) * LANES] = (on * (z * jax.nn.sigmoid(z))).astype(BF16)


def _gdn(qkvz, ba, conv_w, prm, gain):
    b, s, _ = qkvz.shape
    hd = GDN_HEADS * GDN_D
    return pl.pallas_call(
        _gdn_body,
        grid=(b,),
        in_specs=[pl.BlockSpec((None, s, W_QKVZ), lambda i: (i, 0, 0)),
                  pl.BlockSpec((None, s, W_BA), lambda i: (i, 0, 0)),
                  pl.BlockSpec((GDN_CONV, 3 * hd), lambda i: (0, 0)),
                  pl.BlockSpec((8, LANES), lambda i: (0, 0)),
                  pl.BlockSpec((1, GDN_D), lambda i: (0, 0))],
        out_specs=pl.BlockSpec((None, s, hd), lambda i: (i, 0, 0)),
        out_shape=jax.ShapeDtypeStruct((b, s, hd), BF16),
        scratch_shapes=[pltpu.VMEM((GDN_HEADS, s, GDN_D), F32),
                        pltpu.VMEM((GDN_HEADS, s, GDN_D), F32),
                        pltpu.VMEM((GDN_HEADS, s, GDN_D), F32),
                        pltpu.VMEM((GDN_HEADS, s, GDN_D), BF16),
                        pltpu.VMEM((GDN_HEADS, s, LANES), F32),
                        ] + [pltpu.VMEM((GDN_D, GDN_D), F32) for _ in range(GDN_HEADS)],
        compiler_params=_cparams("parallel"),
    )(qkvz, ba, conv_w, prm, gain)


def _attn_body(far_ref, scd_ref, scw_ref, qg_ref, kg_ref, lam_ref, sub_ref, nbt_ref, y_ref,
               qt_s, kn_s, vt_s, ysc_s, m_s, l_s, acc_s, *, lambda_init):
    i = pl.program_id(1)
    s = scd_ref.shape[0]
    blk = ATT_BLK
    o_q, o_k, o_v = 3 * SC_WIDTH, 3 * SC_WIDTH + 256, 3 * SC_WIDTH + 512

    @pl.when(i == 0)
    def _prep():
        row = lax.broadcasted_iota(jnp.int32, (s, SC_WIDTH), 0)
        gate_b = scd_ref[:, 0:SC_WIDTH].astype(F32)
        x = scd_ref[:, SC_WIDTH:2 * SC_WIDTH].astype(F32) * scd_ref[:, 2 * SC_WIDTH:3 * SC_WIDTH].astype(F32)
        w = scw_ref[...]
        acc = x * w[SC_CONV - 1:SC_CONV, :]
        for j in range(SC_CONV - 1):
            sh = SC_CONV - 1 - j
            acc = acc + jnp.where(row >= sh, pltpu.roll(x, sh, 0), 0.0) * w[j:j + 1, :]
        ysc_s[...] = (gate_b * acc).astype(BF16)
        gi = lax.broadcasted_iota(jnp.int32, (256, 256), 0) // DIFF_DQK
        gj = lax.broadcasted_iota(jnp.int32, (256, 256), 1) // DIFF_DQK
        bd = (gi == gj).astype(BF16)

        def normed(off, g_ref, sc):
            xx = scd_ref[:, off:off + 256].astype(F32)
            hi, mid, lo = _split3(xx * xx)
            ss = _dot(hi, bd) + _dot(mid, bd) + _dot(lo, bd)
            return xx * lax.rsqrt(ss * (1.0 / DIFF_DQK) + EPS) * g_ref[...] * sc

        qt_s[...] = normed(o_q, qg_ref, DIFF_DQK ** -0.5).T.astype(BF16)
        kn_s[...] = normed(o_k, kg_ref, 1.0).astype(BF16)
        vt_s[...] = scd_ref[:, o_v:o_v + 256].astype(F32).T.astype(BF16)

    lp = lam_ref[...]
    lam = (jnp.exp(jnp.sum(lp[0:1, :] * lp[1:2, :], axis=-1, keepdims=True))
           - jnp.exp(jnp.sum(lp[2:3, :] * lp[3:4, :], axis=-1, keepdims=True)) + lambda_init)

    r0 = pl.multiple_of(i * blk, blk)
    rowi = lax.broadcasted_iota(jnp.int32, (LANES, blk), 0)
    grp = rowi // DIFF_DQK

    m_s[...] = jnp.full_like(m_s, NEG)
    l_s[...] = jnp.zeros_like(l_s)
    acc_s[...] = jnp.zeros_like(acc_s)

    def q_masked(hp):
        qp = qt_s[hp * LANES:(hp + 1) * LANES, pl.ds(r0, blk)]
        return [jnp.where(grp == sidx, qp, jnp.zeros_like(qp)) for sidx in range(4)]

    def update(st, logits, vt):
        m_old = m_s[st:st + 1, :]
        m_new = jnp.maximum(m_old, jnp.max(logits, axis=0, keepdims=True))
        alpha = jnp.exp(m_old - m_new)
        p = jnp.exp(logits - m_new)
        l_s[st:st + 1, :] = alpha * l_s[st:st + 1, :] + jnp.sum(p, axis=0, keepdims=True)
        acc_s[st] = alpha * acc_s[st] + _dot(vt, p.astype(BF16))
        m_s[st:st + 1, :] = m_new

    def block(c0, bias_of):
        for hp in range(2):
            kp = kn_s[pl.ds(c0, blk), hp * LANES:(hp + 1) * LANES]
            vt = vt_s[hp * LANES:(hp + 1) * LANES, pl.ds(c0, blk)]
            qs = q_masked(hp)
            for sidx in range(4):
                update(hp * 4 + sidx, _dot(kp, qs[sidx]) + bias_of(hp * 2 + sidx // 2), vt)

    def far_body(kb, carry):
        block(pl.multiple_of(kb * blk, blk), lambda head: far_ref[head])
        return carry

    lax.fori_loop(0, jnp.maximum(i - 1, 0), far_body, 0)

    @pl.when(i >= 1)
    def _prev():
        block(pl.multiple_of((i - 1) * blk, blk), lambda head: nbt_ref[head, 0:blk, :])

    block(r0, lambda head: nbt_ref[head, blk:2 * blk, :])

    low = rowi < DIFF_DV
    for hp in range(2):
        outs = []
        for hh in range(2):
            s0 = hp * 4 + hh * 2
            outs.append(acc_s[s0] / l_s[s0:s0 + 1, :] - lam * (acc_s[s0 + 1] / l_s[s0 + 1:s0 + 2, :]))
        o = jnp.where(low, outs[0], outs[1])
        sq = o * o
        ss0 = jnp.sum(jnp.where(low, sq, 0.0), axis=0, keepdims=True)
        ss1 = jnp.sum(jnp.where(low, 0.0, sq), axis=0, keepdims=True)
        ms = jnp.where(low, ss0, ss1) * (1.0 / DIFF_DV)
        y = (o * lax.rsqrt(ms + EPS)).T * sub_ref[...] * (1.0 - lambda_init)
        y_ref[:, SC_WIDTH + hp * LANES:SC_WIDTH + (hp + 1) * LANES] = y.astype(BF16)
    y_ref[:, 0:SC_WIDTH] = ysc_s[pl.ds(r0, blk), :]


def _attn(scd, sc_w, q_gain, k_gain, lam_p, sub_gain, near_bias_t, far_bias, lambda_init):
    b, s, _ = scd.shape
    blk = ATT_BLK
    const2 = lambda bi, i: (0, 0)
    const3 = lambda bi, i: (0, 0, 0)
    return pl.pallas_call(
        functools.partial(_attn_body, lambda_init=lambda_init),
        grid=(b, s // blk),
        in_specs=[pl.BlockSpec(memory_space=pltpu.SMEM),
                  pl.BlockSpec((None, s, W_SCD), lambda bi, i: (bi, 0, 0)),
                  pl.BlockSpec((SC_CONV, SC_WIDTH), const2),
                  pl.BlockSpec((1, 256), const2),
                  pl.BlockSpec((1, 256), const2),
                  pl.BlockSpec((4, DIFF_DQK), const2),
                  pl.BlockSpec((1, LANES), const2),
                  pl.BlockSpec((DIFF_HEADS, 2 * blk, blk), const3)],
        out_specs=pl.BlockSpec((None, blk, 512), lambda bi, i: (bi, i, 0)),
        out_shape=jax.ShapeDtypeStruct((b, s, 512), BF16),
        scratch_shapes=[pltpu.VMEM((256, s), BF16),
                        pltpu.VMEM((s, 256), BF16),
                        pltpu.VMEM((256, s), BF16),
                        pltpu.VMEM((s, SC_WIDTH), BF16),
                        pltpu.VMEM((8, blk), F32),
                        pltpu.VMEM((8, blk), F32),
                        pltpu.VMEM((8, LANES, blk), F32)],
        compiler_params=_cparams("parallel", "arbitrary"),
    )(far_bias, scd, sc_w, q_gain, k_gain, lam_p, sub_gain, near_bias_t)


def _rel_bucket(rel):
    max_exact = REL_BUCKETS // 2
    n = jnp.maximum(rel, 0)
    large = max_exact + (jnp.log(jnp.maximum(n, max_exact).astype(F32) / max_exact)
                         / math.log(REL_MAX_DIST / max_exact) * (REL_BUCKETS - max_exact)).astype(jnp.int32)
    large = jnp.minimum(large, REL_BUCKETS - 1)
    return jnp.where(n < max_exact, n, large)


def _bias_tables(rel_bias):
    blk = ATT_BLK
    rel = jnp.arange(blk)[:, None] + blk - jnp.arange(2 * blk)[None, :]
    onehot = (_rel_bucket(rel)[None] == jnp.arange(REL_BUCKETS)[:, None, None]).astype(F32)
    near = jnp.einsum("brc,bh->hrc", onehot, rel_bias.astype(F32), precision=lax.Precision.HIGHEST)
    near = jnp.where(rel[None] >= 0, near, NEG)
    return jnp.swapaxes(near, 1, 2), rel_bias[REL_BUCKETS - 1].astype(F32)


def _outproj_body(yg_ref, ya_ref, w_ref, h_ref, g_ref, h2_ref, xt_ref):
    hd = yg_ref.shape[1]
    h2 = h_ref[...] + _dot(yg_ref[...], w_ref[0:hd, :]) + _dot(ya_ref[...], w_ref[hd:, :])
    h2_ref[...] = h2
    n = h2 * lax.rsqrt(jnp.mean(h2 * h2, axis=-1, keepdims=True) + EPS) * g_ref[...]
    xt_ref[...] = n.T.astype(BF16)


def _outproj(yg, ya, w, h, gain):
    t, d = h.shape
    tm = min(512, t)
    return pl.pallas_call(
        _outproj_body,
        grid=(t // tm,),
        in_specs=[pl.BlockSpec((tm, yg.shape[1]), lambda i: (i, 0)),
                  pl.BlockSpec((tm, ya.shape[1]), lambda i: (i, 0)),
                  pl.BlockSpec((d, d), lambda i: (0, 0)),
                  pl.BlockSpec((tm, d), lambda i: (i, 0)),
                  pl.BlockSpec((1, d), lambda i: (0, 0))],
        out_specs=[pl.BlockSpec((tm, d), lambda i: (i, 0)),
                   pl.BlockSpec((d, tm), lambda i: (0, i))],
        out_shape=[jax.ShapeDtypeStruct((t, d), F32),
                   jax.ShapeDtypeStruct((d, t), BF16)],
        compiler_params=_cparams("parallel"),
    )(yg, ya, w, h, gain)


def _cmpx(lst, i, j):
    a, b = lst[i], lst[j]
    lst[i] = jnp.maximum(a, b)
    lst[j] = jnp.minimum(a, b)


def _bitonic_clean(lst, lo, n):
    d = n // 2
    while d >= 1:
        for k in range(n):
            if (k // d) % 2 == 0:
                _cmpx(lst, lo + k, lo + k + d)
        d //= 2


def _sort_desc(lst, lo, n):
    if n == 1:
        return
    h = n // 2
    _sort_desc(lst, lo, h)
    _sort_desc(lst, lo + h, h)
    for k in range(h):
        _cmpx(lst, lo + k, lo + n - 1 - k)
    _bitonic_clean(lst, lo, h)
    _bitonic_clean(lst, lo + h, h)


def _merge_sublanes(lst):
    n = len(lst)
    for d in (4, 2, 1):
        lst = [jnp.maximum(lst[k], pltpu.roll(lst[n - 1 - k], d, 0)) for k in range(n)]
        _bitonic_clean(lst, 0, n)
    return lst


def _top_sorted(s):
    lst = [s[k * 8:(k + 1) * 8, :] for k in range(s.shape[0] // 8)]
    _sort_desc(lst, 0, len(lst))
    return _merge_sublanes(lst)


def _route_body(xt_ref, wqt_ref, keys_ref, cn_ref, e1_ref, rk_ref, e2_ref, q_s):
    k = PEER_TOPK
    tn = xt_ref.shape[1]
    q_s[...] = _dot(wqt_ref[...], xt_ref[...]).astype(BF16)
    sub = lax.broadcasted_iota(jnp.int32, (8, tn), 0)
    for hh in range(PEER_HEADS):
        sc = []
        for p in range(2):
            r = (hh * 2 + p) * PEER_DHALF
            sc.append(_dot(keys_ref[hh, p], q_s[r:r + PEER_DHALF, :]))
        a = _top_sorted(sc[0])
        b = _top_sorted(sc[1])
        apack, bpack = a[0], b[0]
        for r in range(1, 8):
            apack = jnp.where(sub == r, a[r], apack)
            bpack = jnp.where(sub == r, b[r], bpack)
        cand = [apack + b[i] for i in range(k)]
        extra = [a[8 + i] + bpack for i in range(k - 8)]
        for i in range(8, k):
            cand[i] = jnp.maximum(cand[i], extra[k - 1 - i])
        _bitonic_clean(cand, 0, k)
        best = _merge_sublanes(cand)
        z = jnp.zeros_like(best[0])
        for i in range(k):
            z = z + jnp.exp(best[i] - best[0])
        tau = best[k - 1]
        for r in range(PEER_KEYS // 8):
            rows = slice(r * 8, (r + 1) * 8)
            s1r, s2r = sc[0][rows, :], sc[1][rows, :]
            rank = jnp.zeros_like(s2r)
            cnt = jnp.zeros_like(s1r)
            for i in range(k):
                rank = rank + jnp.where(b[i] > s2r, 1.0, 0.0)
                cnt = cnt + jnp.where(s1r + b[i] >= tau, 1.0, 0.0)
            rk_ref[hh, rows, :] = rank.astype(BF16)
            cn_ref[hh, rows, :] = cnt
        e1_ref[hh] = jnp.exp(sc[0] - a[0][0:1, :])
        e2_ref[hh] = (jnp.exp(sc[1] - b[0][0:1, :]) / z[0:1, :]).astype(BF16)


def _route(xt, wqt, keys):
    d, t = xt.shape
    tn = min(512, t)
    nq = wqt.shape[0]
    hk = (PEER_HEADS, PEER_KEYS, tn)
    spec = pl.BlockSpec(hk, lambda i: (0, 0, i))
    shp = jax.ShapeDtypeStruct((PEER_HEADS, PEER_KEYS, t), BF16)
    shp32 = jax.ShapeDtypeStruct((PEER_HEADS, PEER_KEYS, t), F32)
    return pl.pallas_call(
        _route_body,
        grid=(t // tn,),
        in_specs=[pl.BlockSpec((d, tn), lambda i: (0, i)),
                  pl.BlockSpec((nq, d), lambda i: (0, 0)),
                  pl.BlockSpec((PEER_HEADS, 2, PEER_KEYS, PEER_DHALF), lambda i: (0, 0, 0, 0))],
        out_specs=[spec, spec, spec, spec],
        out_shape=[shp32, shp32, shp, shp],
        scratch_shapes=[pltpu.VMEM((nq, tn), BF16)],
        compiler_params=_cparams("parallel"),
    )(xt, wqt, keys)


def _gelu(x):
    return 0.5 * x * (1.0 + lax.erf(x * (0.5 ** 0.5)))


def _peer_body(xt_ref, u_ref, vt_ref, cn_ref, e1_ref, rk_ref, e2_ref, h2_ref, o_ref, hid_s, coef_s, acc_s):
    e = pl.program_id(1)
    eb, tn = hid_s.shape
    n_i = eb // PEER_KEYS

    @pl.when(e == 0)
    def _init():
        acc_s[...] = jnp.zeros_like(acc_s)

    hid_s[...] = _dot(u_ref[...], xt_ref[...])

    i0 = pl.multiple_of(e * n_i, n_i)

    for cb in range(tn // LANES):
        cs = slice(cb * LANES, (cb + 1) * LANES)
        for j in range(n_i):
            g = jnp.zeros((PEER_KEYS // BF16_ROWS, BF16_ROWS, LANES), BF16)
            for hh in range(PEER_HEADS):
                cnt = jnp.broadcast_to(cn_ref[hh, pl.ds(i0, n_i), cs][j:j + 1, :], (BF16_ROWS, LANES)).astype(BF16)
                ra = jnp.broadcast_to(e1_ref[hh, pl.ds(i0, n_i), cs][j:j + 1, :], (BF16_ROWS, LANES)).astype(BF16)
                e2 = e2_ref[hh, :, :, cs]
                g = g + jnp.where(rk_ref[hh, :, :, cs] < cnt[None], e2, jnp.zeros_like(e2)) * ra[None]
            rows = slice(j * PEER_KEYS, (j + 1) * PEER_KEYS)
            coef_s[rows, cs] = g.reshape(PEER_KEYS, LANES) * _gelu(hid_s[rows, cs]).astype(BF16)

    acc_s[...] += _dot(vt_ref[...], coef_s[...])

    @pl.when(e == pl.num_programs(1) - 1)
    def _fin():
        o_ref[...] = h2_ref[...] + acc_s[...].T


def _peer(xt, u, vt, cn, e1, rk, e2, h2):
    d, t = xt.shape
    n_exp = u.shape[0]
    tn = min(512, t)
    eb = 1024
    hk = pl.BlockSpec((PEER_HEADS, PEER_KEYS, tn), lambda i, e: (0, 0, i))
    hk16 = pl.BlockSpec((PEER_HEADS, PEER_KEYS // BF16_ROWS, BF16_ROWS, tn), lambda i, e: (0, 0, 0, i))
    rk = rk.reshape(PEER_HEADS, PEER_KEYS // BF16_ROWS, BF16_ROWS, t)
    e2 = e2.reshape(PEER_HEADS, PEER_KEYS // BF16_ROWS, BF16_ROWS, t)
    return pl.pallas_call(
        _peer_body,
        grid=(t // tn, n_exp // eb),
        in_specs=[pl.BlockSpec((d, tn), lambda i, e: (0, i)),
                  pl.BlockSpec((eb, d), lambda i, e: (e, 0)),
                  pl.BlockSpec((d, eb), lambda i, e: (0, e)),
                  hk, hk, hk16, hk16,
                  pl.BlockSpec((tn, d), lambda i, e: (i, 0))],
        out_specs=pl.BlockSpec((tn, d), lambda i, e: (i, 0)),
        out_shape=jax.ShapeDtypeStruct((t, d), F32),
        scratch_shapes=[pltpu.VMEM((eb, tn), F32),
                        pltpu.VMEM((eb, tn), BF16),
                        pltpu.VMEM((d, tn), F32)],
        compiler_params=_cparams("parallel", "arbitrary"),
    )(xt, u, vt, cn, e1, rk, e2, h2)


def _pad_lanes(v, offset):
    return jnp.zeros((1, LANES), F32).at[0, offset:offset + v.shape[0]].set(v.astype(F32))


def _layer(h, l, near_bias, far_bias, attn_norm, w_in, gdn_conv, gdn_a_log, gdn_dt_bias, gdn_out_norm,
           sc_conv, diff_q_norm, diff_k_norm, diff_lambda, diff_subln, w_out, ffn_norm,
           peer_wq, peer_keys, peer_u, peer_v, batch):
    t, d = h.shape
    s = t // batch
    lambda_init = 0.8 - 0.6 * math.exp(-0.3 * l)
    n_main = W_QKVZ + W_SCD
    wi = w_in[l]
    w_r = jnp.concatenate([wi[:, 0:W_QKVZ], wi[:, W_QKVZ + 2 * GDN_HEADS:], wi[:, W_QKVZ:W_QKVZ + 2 * GDN_HEADS],
                           jnp.zeros((d, W_BA - 2 * GDN_HEADS), wi.dtype)], axis=1).astype(BF16)
    assert w_r.shape[1] == n_main + W_BA
    qkvz, scd, ba = _inproj(h, attn_norm[l][None, :], w_r)

    prm = jnp.concatenate([_pad_lanes(gdn_a_log[l], GDN_HEADS), _pad_lanes(gdn_dt_bias[l], GDN_HEADS),
                           jnp.zeros((6, LANES), F32)], axis=0)
    y_gdn = _gdn(qkvz.reshape(batch, s, W_QKVZ), ba.reshape(batch, s, W_BA), gdn_conv[l].astype(F32), prm,
                 gdn_out_norm[l][None, :].astype(F32))

    y_att = _attn(scd.reshape(batch, s, W_SCD), sc_conv[l].astype(F32),
                  jnp.tile(diff_q_norm[l], 256 // DIFF_DQK)[None, :].astype(F32),
                  jnp.tile(diff_k_norm[l], 256 // DIFF_DQK)[None, :].astype(F32),
                  diff_lambda[l].astype(F32),
                  jnp.tile(diff_subln[l], LANES // DIFF_DV)[None, :].astype(F32),
                  near_bias, far_bias, lambda_init)

    h2, xt = _outproj(y_gdn.reshape(t, -1), y_att.reshape(t, -1), w_out[l].astype(BF16), h, ffn_norm[l][None, :])

    cn, e1, rk, e2 = _route(xt, peer_wq[l].T.astype(BF16), peer_keys[l].astype(BF16))
    return _peer(xt, peer_u[l].astype(BF16), peer_v[l].T.astype(BF16), cn, e1, rk, e2, h2)


def kernel(x, rel_bias, attn_norm, w_in, gdn_conv, gdn_a_log, gdn_dt_bias, gdn_out_norm, sc_conv,
           diff_q_norm, diff_k_norm, diff_lambda, diff_subln, w_out, ffn_norm, peer_wq, peer_keys,
           peer_u, peer_v):
    batch, s, d = x.shape
    near_bias, far_bias = _bias_tables(rel_bias)
    h = x.reshape(batch * s, d)
    for l in range(w_in.shape[0]):
        h = _layer(h, l, near_bias, far_bias, attn_norm, w_in, gdn_conv, gdn_a_log, gdn_dt_bias,
                   gdn_out_norm, sc_conv, diff_q_norm, diff_k_norm, diff_lambda, diff_subln, w_out,
                   ffn_norm, peer_wq, peer_keys, peer_u, peer_v, batch)
    return h.reshape(batch, s, d)
```

```python
import functools
import math

import jax
import jax.numpy as jnp
import numpy as np
from jax import lax
from jax.experimental import pallas as pl
from jax.experimental.pallas import tpu as pltpu

F32 = jnp.float32
BF16 = jnp.bfloat16
EPS = 1e-6
NEG = -1e30

D_MODEL = 1024
GDN_HEADS = 4
GDN_D = 128
GDN_CONV = 4
GDN_CHUNK = 64
SC_WIDTH = 256
SC_CONV = 3
DIFF_HEADS = 4
DIFF_DV = 64
DIFF_DQK = 32
ATT_BLK = 256
REL_BUCKETS = 32
REL_MAX_DIST = 128
PEER_HEADS = 8
PEER_KEYS = 128
PEER_TOPK = 16
PEER_DHALF = 128
LANES = 128
BF16_ROWS = 16
VMEM_LIMIT = 56 * 1024 * 1024

W_QKVZ = 2048
W_SCD = 1536
W_BA = LANES


def _cparams(*sem):
    return pltpu.CompilerParams(dimension_semantics=sem, vmem_limit_bytes=VMEM_LIMIT)


def _nt_dot(a, b):
    return lax.dot_general(a, b, (((1,), (1,)), ((), ())), preferred_element_type=F32)


def _tn_dot(a, b):
    return lax.dot_general(a, b, (((0,), (0,)), ((), ())), preferred_element_type=F32)


def _dot(a, b):
    return jnp.dot(a, b, preferred_element_type=F32)


def _split3(x):
    hi = x.astype(BF16)
    r = x - hi.astype(F32)
    mid = r.astype(BF16)
    lo = (r - mid.astype(F32)).astype(BF16)
    return hi, mid, lo


def _inproj_body(h_ref, g_ref, w_ref, qkvz_ref, scd_ref, ba_ref):
    x = h_ref[...]
    n = x * lax.rsqrt(jnp.mean(x * x, axis=-1, keepdims=True) + EPS) * g_ref[...]
    nb = n.astype(BF16)
    qkvz_ref[...] = _dot(nb, w_ref[:, 0:W_QKVZ]).astype(BF16)
    scd_ref[...] = _dot(nb, w_ref[:, W_QKVZ:W_QKVZ + W_SCD]).astype(BF16)
    ba_ref[...] = _dot(nb, w_ref[:, W_QKVZ + W_SCD:])


def _inproj(h, gain, w):
    t, d = h.shape
    tm = min(512, t)
    nw = w.shape[1]
    return pl.pallas_call(
        _inproj_body,
        grid=(t // tm,),
        in_specs=[pl.BlockSpec((tm, d), lambda i: (i, 0)),
                  pl.BlockSpec((1, d), lambda i: (0, 0)),
                  pl.BlockSpec((d, nw), lambda i: (0, 0))],
        out_specs=[pl.BlockSpec((tm, W_QKVZ), lambda i: (i, 0)),
                   pl.BlockSpec((tm, W_SCD), lambda i: (i, 0)),
                   pl.BlockSpec((tm, W_BA), lambda i: (i, 0))],
        out_shape=[jax.ShapeDtypeStruct((t, W_QKVZ), BF16),
                   jax.ShapeDtypeStruct((t, W_SCD), BF16),
                   jax.ShapeDtypeStruct((t, W_BA), F32)],
        compiler_params=_cparams("parallel"),
    )(h, gain, w)


def _gdn_body(qkvz_ref, ba_ref, conv_ref, prm_ref, gain_ref, y_ref,
              q_s, k_s, kb_s, vb_s, gb_s, o_s, st_s, u_b, w_b, a_b, qg_b, kd_b):
    s = qkvz_ref.shape[0]
    c_sz, nh = GDN_CHUNK, GDN_HEADS
    nc, rr = s // c_sz, GDN_HEADS * GDN_CHUNK
    row = lax.broadcasted_iota(jnp.int32, (s, LANES), 0)

    ba = ba_ref[...]
    beta = jax.nn.sigmoid(ba)
    xg = ba + prm_ref[1:2, :]
    softplus = jnp.maximum(xg, 0.0) + jnp.log(1.0 + jnp.exp(-jnp.abs(xg)))
    g = -jnp.exp(prm_ref[0:1, :]) * softplus
    pos = row % c_sz
    for sh in (1, 2, 4, 8, 16, 32):
        g = g + jnp.where(pos >= sh, pltpu.roll(g, sh, 0), 0.0)

    def chunked(x):
        return x.reshape(nc, c_sz, LANES)

    for hh in range(nh):
        gb_s[:, hh] = chunked(jnp.broadcast_to(g[:, nh + hh:nh + hh + 1], (s, LANES)))

    scale = GDN_D ** -0.5
    for cb in range(3 * nh):
        x = qkvz_ref[:, cb * LANES:(cb + 1) * LANES].astype(F32)
        w = conv_ref[:, cb * LANES:(cb + 1) * LANES]
        acc = x * w[GDN_CONV - 1:GDN_CONV, :]
        for j in range(GDN_CONV - 1):
            sh = GDN_CONV - 1 - j
            acc = acc + jnp.where(row >= sh, pltpu.roll(x, sh, 0), 0.0) * w[j:j + 1, :]
        y = acc * jax.nn.sigmoid(acc)
        kind, hh = divmod(cb, nh)
        if kind < 2:
            y = y * lax.rsqrt(jnp.sum(y * y, axis=-1, keepdims=True) + EPS)
        if kind == 0:
            q_s[:, hh] = chunked(y * scale)
        elif kind == 1:
            k_s[:, hh] = chunked(y)
            kb_s[:, hh] = chunked(y * beta[:, hh:hh + 1])
        else:
            vb_s[:, hh] = chunked((y * beta[:, hh:hh + 1]).astype(BF16))

    st_s[...] = jnp.zeros_like(st_s)
    ii = lax.broadcasted_iota(jnp.int32, (rr, rr), 0)
    jj = lax.broadcasted_iota(jnp.int32, (rr, rr), 1)
    same_head = (ii // c_sz) == (jj // c_sz)
    tril = jnp.logical_and(same_head, ii >= jj)
    strict = jnp.logical_and(same_head, ii > jj)
    eye = (ii == jj).astype(F32)
    lane = lax.broadcasted_iota(jnp.int32, (rr, LANES), 1)
    pick3 = (lane < 3).astype(BF16)
    row_head = lax.broadcasted_iota(jnp.int32, (rr, LANES), 0) // c_sz

    def stacked(ref, c):
        return ref[c].reshape(rr, LANES)

    def phase_a(c, slot):
        gc = stacked(gb_s, c)
        eg = jnp.exp(gc)
        qc, kc, kb = stacked(q_s, c), stacked(k_s, c), stacked(kb_s, c)
        glast = jnp.broadcast_to(gb_s[c][:, c_sz - 1:c_sz, :], (nh, c_sz, LANES)).reshape(rr, LANES)
        hi, mid, lo = _split3(gc)
        x3 = jnp.where(lane == 0, hi, jnp.where(lane == 1, mid, jnp.where(lane == 2, lo, jnp.zeros_like(lo))))
        grow = _nt_dot(pick3, x3)
        decay = jnp.where(tril, jnp.exp(jnp.minimum(jnp.concatenate([gc, gc], axis=1) - grow, 0.0)), 0.0)
        kq = _nt_dot(jnp.concatenate([kb.astype(BF16), qc.astype(BF16)], axis=0), kc.astype(BF16))
        lower = jnp.where(strict, kq[0:rr] * decay, 0.0)
        a_in = jnp.where(tril, kq[rr:2 * rr] * decay, 0.0)
        pw = -lower
        tm = eye + pw
        pw = _dot(pw.astype(BF16), pw.astype(BF16))
        for _ in range(4):
            pwb = pw.astype(BF16)
            both = _dot(jnp.concatenate([tm.astype(BF16), pwb], axis=0), pwb)
            tm = tm + both[0:rr]
            pw = both[rr:2 * rr]
        tm = tm + _dot(tm.astype(BF16), pw.astype(BF16))
        uw = _dot(tm.astype(BF16), jnp.concatenate([stacked(vb_s, c), (kb * eg).astype(BF16)], axis=1))
        u_b[slot] = uw[:, 0:LANES]
        w_b[slot] = uw[:, LANES:2 * LANES].astype(BF16)
        a_b[slot] = a_in.astype(BF16)
        qg_b[slot] = (qc * eg).astype(BF16)
        kd_b[slot] = (kc * jnp.exp(glast - gc)).astype(BF16)

    def phase_b(c, slot):
        stb = st_s[...].astype(BF16)
        wq = _dot(jnp.concatenate([w_b[slot], qg_b[slot]], axis=0), stb)
        u = u_b[slot]

        def head_blocks(r0):
            return jnp.concatenate([wq[r0 + hh * c_sz:r0 + (hh + 1) * c_sz, hh * LANES:(hh + 1) * LANES]
                                    for hh in range(nh)], axis=0)

        vnb = (u - head_blocks(0)).astype(BF16)
        o_s[c] = (head_blocks(rr) + _dot(a_b[slot], vnb)).reshape(nh, c_sz, LANES)
        vbd = jnp.concatenate([jnp.where(row_head == hh, vnb, jnp.zeros_like(vnb)) for hh in range(nh)], axis=1)
        egl = jnp.concatenate([jnp.exp(gb_s[c][hh, c_sz - 1:c_sz, :]) for hh in range(nh)], axis=1)
        st_s[...] = st_s[...] * egl + _tn_dot(kd_b[slot], vbd)

    phase_a(0, 0)

    def pair(kk, carry):
        c = 2 * kk
        phase_b(c, 0)
        phase_a(c + 1, 1)
        phase_b(c + 1, 1)
        phase_a(jnp.minimum(c + 2, nc - 1), 0)
        return carry

    lax.fori_loop(0, nc // 2, pair, 0)

    gain = gain_ref[...]
    for hh in range(nh):
        o = o_s[:, hh].reshape(s, LANES)
        z = qkvz_ref[:, (3 * nh + hh) * LANES:(3 * nh + hh + 1) * LANES].astype(F32)
        on = o * lax.rsqrt(jnp.mean(o * o, axis=-1, keepdims=True) + EPS) * gain
        y_ref[:, hh * LANES:(hh + 1) * LANES] = (on * (z * jax.nn.sigmoid(z))).astype(BF16)


def _gdn(qkvz, ba, conv_w, prm, gain):
    b, s, _ = qkvz.shape
    hd = GDN_HEADS * GDN_D
    nc, rr = s // GDN_CHUNK, GDN_HEADS * GDN_CHUNK
    per_chunk = (nc, GDN_HEADS, GDN_CHUNK, GDN_D)
    return pl.pallas_call(
        _gdn_body,
        grid=(b,),
        in_specs=[pl.BlockSpec((None, s, W_QKVZ), lambda i: (i, 0, 0)),
                  pl.BlockSpec((None, s, W_BA), lambda i: (i, 0, 0)),
                  pl.BlockSpec((GDN_CONV, 3 * hd), lambda i: (0, 0)),
                  pl.BlockSpec((8, LANES), lambda i: (0, 0)),
                  pl.BlockSpec((1, GDN_D), lambda i: (0, 0))],
        out_specs=pl.BlockSpec((None, s, hd), lambda i: (i, 0, 0)),
        out_shape=jax.ShapeDtypeStruct((b, s, hd), BF16),
        scratch_shapes=[pltpu.VMEM(per_chunk, F32),
                        pltpu.VMEM(per_chunk, F32),
                        pltpu.VMEM(per_chunk, F32),
                        pltpu.VMEM(per_chunk, BF16),
                        pltpu.VMEM(per_chunk, F32),
                        pltpu.VMEM(per_chunk, F32),
                        pltpu.VMEM((GDN_D, hd), F32),
                        pltpu.VMEM((2, rr, GDN_D), F32),
                        pltpu.VMEM((2, rr, GDN_D), BF16),
                        pltpu.VMEM((2, rr, rr), BF16),
                        pltpu.VMEM((2, rr, GDN_D), BF16),
                        pltpu.VMEM((2, rr, GDN_D), BF16)],
        compiler_params=_cparams("parallel"),
    )(qkvz, ba, conv_w, prm, gain)


def _attn_body(far_ref, scd_ref, scw_ref, qg_ref, kg_ref, lam_ref, sub_ref, nbt_ref, y_ref,
               qt_s, kn_s, vt_s, ysc_s, qc_s, m_s, l_s, acc_s, *, lambda_init):
    i = pl.program_id(1)
    s = scd_ref.shape[0]
    blk = ATT_BLK
    o_q, o_k, o_v = 3 * SC_WIDTH, 3 * SC_WIDTH + 256, 3 * SC_WIDTH + 512

    @pl.when(i == 0)
    def _prep():
        row = lax.broadcasted_iota(jnp.int32, (s, SC_WIDTH), 0)
        gate_b = scd_ref[:, 0:SC_WIDTH].astype(F32)
        x = scd_ref[:, SC_WIDTH:2 * SC_WIDTH].astype(F32) * scd_ref[:, 2 * SC_WIDTH:3 * SC_WIDTH].astype(F32)
        w = scw_ref[...]
        acc = x * w[SC_CONV - 1:SC_CONV, :]
        for j in range(SC_CONV - 1):
            sh = SC_CONV - 1 - j
            acc = acc + jnp.where(row >= sh, pltpu.roll(x, sh, 0), 0.0) * w[j:j + 1, :]
        ysc_s[...] = (gate_b * acc).astype(BF16)
        gi = lax.broadcasted_iota(jnp.int32, (256, 256), 0) // DIFF_DQK
        gj = lax.broadcasted_iota(jnp.int32, (256, 256), 1) // DIFF_DQK
        bd = (gi == gj).astype(BF16)

        def normed(off, g_ref, sc):
            xx = scd_ref[:, off:off + 256].astype(F32)
            hi, mid, lo = _split3(xx * xx)
            ss = _dot(hi, bd) + _dot(mid, bd) + _dot(lo, bd)
            return xx * lax.rsqrt(ss * (1.0 / DIFF_DQK) + EPS) * g_ref[...] * sc

        qt_s[...] = normed(o_q, qg_ref, DIFF_DQK ** -0.5).T.astype(BF16)
        kn_s[...] = normed(o_k, kg_ref, 1.0).astype(BF16)
        vt_s[...] = scd_ref[:, o_v:o_v + 256].astype(F32).T.astype(BF16)

    lp = lam_ref[...]
    lam = (jnp.exp(jnp.sum(lp[0:1, :] * lp[1:2, :], axis=-1, keepdims=True))
           - jnp.exp(jnp.sum(lp[2:3, :] * lp[3:4, :], axis=-1, keepdims=True)) + lambda_init)

    r0 = pl.multiple_of(i * blk, blk)
    rowi = lax.broadcasted_iota(jnp.int32, (LANES, blk), 0)
    grp = rowi // DIFF_DQK

    m_s[...] = jnp.full_like(m_s, NEG)
    l_s[...] = jnp.zeros_like(l_s)
    acc_s[...] = jnp.zeros_like(acc_s)
    for hp in range(2):
        qp = qt_s[hp * LANES:(hp + 1) * LANES, pl.ds(r0, blk)]
        for sidx in range(4):
            qc_s[hp, :, sidx * blk:(sidx + 1) * blk] = jnp.where(grp == sidx, qp, jnp.zeros_like(qp))

    def block(c0, bias_of):
        for hp in range(2):
            kp = kn_s[pl.ds(c0, blk), hp * LANES:(hp + 1) * LANES]
            vt = vt_s[hp * LANES:(hp + 1) * LANES, pl.ds(c0, blk)]
            logits = _dot(kp, qc_s[hp])
            b0, b1 = bias_of(2 * hp), bias_of(2 * hp + 1)
            if b0.ndim == 2:
                bias = jnp.concatenate([b0, b0, b1, b1], axis=1)
                logits = logits + bias
            else:
                two = 2 * blk
                logits = jnp.concatenate([logits[:, 0:two] + b0, logits[:, two:2 * two] + b1], axis=1)
            m_old = m_s[hp:hp + 1, :]
            m_new = jnp.maximum(m_old, jnp.max(logits, axis=0, keepdims=True))
            alpha = jnp.exp(m_old - m_new)
            p = jnp.exp(logits - m_new)
            l_s[hp:hp + 1, :] = alpha * l_s[hp:hp + 1, :] + jnp.sum(p, axis=0, keepdims=True)
            acc_s[hp] = alpha * acc_s[hp] + _dot(vt, p.astype(BF16))
            m_s[hp:hp + 1, :] = m_new

    def far_body(kb, carry):
        block(pl.multiple_of(kb * blk, blk), lambda head: far_ref[head])
        return carry

    lax.fori_loop(0, jnp.maximum(i - 1, 0), far_body, 0)

    @pl.when(i >= 1)
    def _prev():
        block(pl.multiple_of((i - 1) * blk, blk), lambda head: nbt_ref[head, 0:blk, :])

    block(r0, lambda head: nbt_ref[head, blk:2 * blk, :])

    low = rowi < DIFF_DV
    for hp in range(2):
        outs = []
        for hh in range(2):
            c0, c1 = (2 * hh) * blk, (2 * hh + 1) * blk
            outs.append(acc_s[hp, :, c0:c0 + blk] / l_s[hp:hp + 1, c0:c0 + blk]
                        - lam * (acc_s[hp, :, c1:c1 + blk] / l_s[hp:hp + 1, c1:c1 + blk]))
        o = jnp.where(low, outs[0], outs[1])
        sq = o * o
        ss0 = jnp.sum(jnp.where(low, sq, 0.0), axis=0, keepdims=True)
        ss1 = jnp.sum(jnp.where(low, 0.0, sq), axis=0, keepdims=True)
        ms = jnp.where(low, ss0, ss1) * (1.0 / DIFF_DV)
        y = (o * lax.rsqrt(ms + EPS)).T * sub_ref[...] * (1.0 - lambda_init)
        y_ref[:, SC_WIDTH + hp * LANES:SC_WIDTH + (hp + 1) * LANES] = y.astype(BF16)
    y_ref[:, 0:SC_WIDTH] = ysc_s[pl.ds(r0, blk), :]


def _attn(scd, sc_w, q_gain, k_gain, lam_p, sub_gain, near_bias_t, far_bias, lambda_init):
    b, s, _ = scd.shape
    blk = ATT_BLK
    const2 = lambda bi, i: (0, 0)
    const3 = lambda bi, i: (0, 0, 0)
    return pl.pallas_call(
        functools.partial(_attn_body, lambda_init=lambda_init),
        grid=(b, s // blk),
        in_specs=[pl.BlockSpec(memory_space=pltpu.SMEM),
                  pl.BlockSpec((None, s, W_SCD), lambda bi, i: (bi, 0, 0)),
                  pl.BlockSpec((SC_CONV, SC_WIDTH), const2),
                  pl.BlockSpec((1, 256), const2),
                  pl.BlockSpec((1, 256), const2),
                  pl.BlockSpec((4, DIFF_DQK), const2),
                  pl.BlockSpec((1, LANES), const2),
                  pl.BlockSpec((DIFF_HEADS, 2 * blk, blk), const3)],
        out_specs=pl.BlockSpec((None, blk, 512), lambda bi, i: (bi, i, 0)),
        out_shape=jax.ShapeDtypeStruct((b, s, 512), BF16),
        scratch_shapes=[pltpu.VMEM((256, s), BF16),
                        pltpu.VMEM((s, 256), BF16),
                        pltpu.VMEM((256, s), BF16),
                        pltpu.VMEM((s, SC_WIDTH), BF16),
                        pltpu.VMEM((2, LANES, 4 * blk), BF16),
                        pltpu.VMEM((2, 4 * blk), F32),
                        pltpu.VMEM((2, 4 * blk), F32),
                        pltpu.VMEM((2, LANES, 4 * blk), F32)],
        compiler_params=_cparams("parallel", "arbitrary"),
    )(far_bias, scd, sc_w, q_gain, k_gain, lam_p, sub_gain, near_bias_t)


def _rel_bucket(rel):
    max_exact = REL_BUCKETS // 2
    n = jnp.maximum(rel, 0)
    large = max_exact + (jnp.log(jnp.maximum(n, max_exact).astype(F32) / max_exact)
                         / math.log(REL_MAX_DIST / max_exact) * (REL_BUCKETS - max_exact)).astype(jnp.int32)
    large = jnp.minimum(large, REL_BUCKETS - 1)
    return jnp.where(n < max_exact, n, large)


def _bias_tables(rel_bias):
    blk = ATT_BLK
    rel = jnp.arange(blk)[:, None] + blk - jnp.arange(2 * blk)[None, :]
    onehot = (_rel_bucket(rel)[None] == jnp.arange(REL_BUCKETS)[:, None, None]).astype(F32)
    near = jnp.einsum("brc,bh->hrc", onehot, rel_bias.astype(F32), precision=lax.Precision.HIGHEST)
    near = jnp.where(rel[None] >= 0, near, NEG)
    return jnp.swapaxes(near, 1, 2), rel_bias[REL_BUCKETS - 1].astype(F32)


def _outproj_body(yg_ref, ya_ref, w_ref, h_ref, g_ref, h2_ref, xt_ref):
    hd = yg_ref.shape[1]
    h2 = h_ref[...] + _dot(yg_ref[...], w_ref[0:hd, :]) + _dot(ya_ref[...], w_ref[hd:, :])
    h2_ref[...] = h2
    n = h2 * lax.rsqrt(jnp.mean(h2 * h2, axis=-1, keepdims=True) + EPS) * g_ref[...]
    xt_ref[...] = n.T.astype(BF16)


def _outproj(yg, ya, w, h, gain):
    t, d = h.shape
    tm = min(512, t)
    return pl.pallas_call(
        _outproj_body,
        grid=(t // tm,),
        in_specs=[pl.BlockSpec((tm, yg.shape[1]), lambda i: (i, 0)),
                  pl.BlockSpec((tm, ya.shape[1]), lambda i: (i, 0)),
                  pl.BlockSpec((d, d), lambda i: (0, 0)),
                  pl.BlockSpec((tm, d), lambda i: (i, 0)),
                  pl.BlockSpec((1, d), lambda i: (0, 0))],
        out_specs=[pl.BlockSpec((tm, d), lambda i: (i, 0)),
                   pl.BlockSpec((d, tm), lambda i: (0, i))],
        out_shape=[jax.ShapeDtypeStruct((t, d), F32),
                   jax.ShapeDtypeStruct((d, t), BF16)],
        compiler_params=_cparams("parallel"),
    )(yg, ya, w, h, gain)


def _cmpx(lst, i, j):
    a, b = lst[i], lst[j]
    lst[i] = jnp.maximum(a, b)
    lst[j] = jnp.minimum(a, b)


def _bitonic_clean(lst, lo, n):
    d = n // 2
    while d >= 1:
        for k in range(n):
            if (k // d) % 2 == 0:
                _cmpx(lst, lo + k, lo + k + d)
        d //= 2


def _sort_desc(lst, lo, n):
    if n == 1:
        return
    h = n // 2
    _sort_desc(lst, lo, h)
    _sort_desc(lst, lo + h, h)
    for k in range(h):
        _cmpx(lst, lo + k, lo + n - 1 - k)
    _bitonic_clean(lst, lo, h)
    _bitonic_clean(lst, lo + h, h)


def _merge_sublanes(lst):
    n = len(lst)
    for d in (4, 2, 1):
        lst = [jnp.maximum(lst[k], pltpu.roll(lst[n - 1 - k], d, 0)) for k in range(n)]
        _bitonic_clean(lst, 0, n)
    return lst


def _top_sorted(s):
    lst = [s[k * 8:(k + 1) * 8, :] for k in range(s.shape[0] // 8)]
    _sort_desc(lst, 0, len(lst))
    return _merge_sublanes(lst)


def _route_body(xt_ref, wqt_ref, keys_ref, cn_ref, e1_ref, rk_ref, e2_ref, q_s):
    k = PEER_TOPK
    tn = xt_ref.shape[1]
    q_s[...] = _dot(wqt_ref[...], xt_ref[...]).astype(BF16)
    sub = lax.broadcasted_iota(jnp.int32, (8, tn), 0)
    for hh in range(PEER_HEADS):
        sc = []
        for p in range(2):
            r = (hh * 2 + p) * PEER_DHALF
            sc.append(_dot(keys_ref[hh, p], q_s[r:r + PEER_DHALF, :]))
        a = _top_sorted(sc[0])
        b = _top_sorted(sc[1])
        apack, bpack = a[0], b[0]
        for r in range(1, 8):
            apack = jnp.where(sub == r, a[r], apack)
            bpack = jnp.where(sub == r, b[r], bpack)
        cand = [apack + b[i] for i in range(k)]
        extra = [a[8 + i] + bpack for i in range(k - 8)]
        for i in range(8, k):
            cand[i] = jnp.maximum(cand[i], extra[k - 1 - i])
        _bitonic_clean(cand, 0, k)
        best = _merge_sublanes(cand)
        z = jnp.zeros_like(best[0])
        for i in range(k):
            z = z + jnp.exp(best[i] - best[0])
        tau = best[k - 1]
        for r in range(PEER_KEYS // 8):
            rows = slice(r * 8, (r + 1) * 8)
            s1r, s2r = sc[0][rows, :], sc[1][rows, :]
            rank = jnp.zeros_like(s2r)
            cnt = jnp.zeros_like(s1r)
            for i in range(k):
                rank = rank + jnp.where(b[i] > s2r, 1.0, 0.0)
                cnt = cnt + jnp.where(s1r + b[i] >= tau, 1.0, 0.0)
            rk_ref[hh, rows, :] = rank.astype(BF16)
            cn_ref[hh, rows, :] = cnt
        e1_ref[hh] = jnp.exp(sc[0] - a[0][0:1, :])
        e2_ref[hh] = (jnp.exp(sc[1] - b[0][0:1, :]) / z[0:1, :]).astype(BF16)


def _route(xt, wqt, keys):
    d, t = xt.shape
    tn = min(512, t)
    nq = wqt.shape[0]
    hk = (PEER_HEADS, PEER_KEYS, tn)
    spec = pl.BlockSpec(hk, lambda i: (0, 0, i))
    shp = jax.ShapeDtypeStruct((PEER_HEADS, PEER_KEYS, t), BF16)
    shp32 = jax.ShapeDtypeStruct((PEER_HEADS, PEER_KEYS, t), F32)
    return pl.pallas_call(
        _route_body,
        grid=(t // tn,),
        in_specs=[pl.BlockSpec((d, tn), lambda i: (0, i)),
                  pl.BlockSpec((nq, d), lambda i: (0, 0)),
                  pl.BlockSpec((PEER_HEADS, 2, PEER_KEYS, PEER_DHALF), lambda i: (0, 0, 0, 0))],
        out_specs=[spec, spec, spec, spec],
        out_shape=[shp32, shp32, shp, shp],
        scratch_shapes=[pltpu.VMEM((nq, tn), BF16)],
        compiler_params=_cparams("parallel"),
    )(xt, wqt, keys)


def _gelu(x):
    return 0.5 * x * (1.0 + lax.erf(x * (0.5 ** 0.5)))


def _peer_body(xt_ref, u_ref, vt_ref, cn_ref, e1_ref, rk_ref, e2_ref, h2_ref, o_ref, hid_s, coef_s, acc_s):
    e = pl.program_id(1)
    eb, tn = hid_s.shape
    n_i = eb // PEER_KEYS

    @pl.when(e == 0)
    def _init():
        acc_s[...] = jnp.zeros_like(acc_s)

    hid_s[...] = _dot(u_ref[...], xt_ref[...])

    i0 = pl.multiple_of(e * n_i, n_i)
    for cb in range(tn // LANES):
        cs = slice(cb * LANES, (cb + 1) * LANES)
        for j in range(n_i):
            g = jnp.zeros((PEER_KEYS // BF16_ROWS, BF16_ROWS, LANES), BF16)
            for hh in range(PEER_HEADS):
                cnt = jnp.broadcast_to(cn_ref[hh, pl.ds(i0, n_i), cs][j:j + 1, :], (BF16_ROWS, LANES)).astype(BF16)
                ra = jnp.broadcast_to(e1_ref[hh, pl.ds(i0, n_i), cs][j:j + 1, :], (BF16_ROWS, LANES)).astype(BF16)
                e2 = e2_ref[hh, :, :, cs]
                g = g + jnp.where(rk_ref[hh, :, :, cs] < cnt[None], e2, jnp.zeros_like(e2)) * ra[None]
            rows = slice(j * PEER_KEYS, (j + 1) * PEER_KEYS)
            coef_s[rows, cs] = g.reshape(PEER_KEYS, LANES) * _gelu(hid_s[rows, cs]).astype(BF16)

    acc_s[...] += _dot(vt_ref[...], coef_s[...])

    @pl.when(e == pl.num_programs(1) - 1)
    def _fin():
        o_ref[...] = h2_ref[...] + acc_s[...].T


def _peer(xt, u, vt, cn, e1, rk, e2, h2):
    d, t = xt.shape
    n_exp = u.shape[0]
    tn = min(512, t)
    eb = 1024
    hk = pl.BlockSpec((PEER_HEADS, PEER_KEYS, tn), lambda i, e: (0, 0, i))
    hk16 = pl.BlockSpec((PEER_HEADS, PEER_KEYS // BF16_ROWS, BF16_ROWS, tn), lambda i, e: (0, 0, 0, i))
    rk = rk.reshape(PEER_HEADS, PEER_KEYS // BF16_ROWS, BF16_ROWS, t)
    e2 = e2.reshape(PEER_HEADS, PEER_KEYS // BF16_ROWS, BF16_ROWS, t)
    return pl.pallas_call(
        _peer_body,
        grid=(t // tn, n_exp // eb),
        in_specs=[pl.BlockSpec((d, tn), lambda i, e: (0, i)),
                  pl.BlockSpec((eb, d), lambda i, e: (e, 0)),
                  pl.BlockSpec((d, eb), lambda i, e: (0, e)),
                  hk, hk, hk16, hk16,
                  pl.BlockSpec((tn, d), lambda i, e: (i, 0))],
        out_specs=pl.BlockSpec((tn, d), lambda i, e: (i, 0)),
        out_shape=jax.ShapeDtypeStruct((t, d), F32),
        scratch_shapes=[pltpu.VMEM((eb, tn), F32),
                        pltpu.VMEM((eb, tn), BF16),
                        pltpu.VMEM((d, tn), F32)],
        compiler_params=_cparams("parallel", "arbitrary"),
    )(xt, u, vt, cn, e1, rk, e2, h2)


def _pad_lanes(v, offset):
    return jnp.zeros((1, LANES), F32).at[0, offset:offset + v.shape[0]].set(v.astype(F32))


def _layer(h, l, near_bias, far_bias, attn_norm, w_in, gdn_conv, gdn_a_log, gdn_dt_bias, gdn_out_norm,
           sc_conv, diff_q_norm, diff_k_norm, diff_lambda, diff_subln, w_out, ffn_norm,
           peer_wq, peer_keys, peer_u, peer_v, batch):
    t, d = h.shape
    s = t // batch
    lambda_init = 0.8 - 0.6 * math.exp(-0.3 * l)
    n_main = W_QKVZ + W_SCD
    wi = w_in[l]
    w_r = jnp.concatenate([wi[:, 0:W_QKVZ], wi[:, W_QKVZ + 2 * GDN_HEADS:], wi[:, W_QKVZ:W_QKVZ + 2 * GDN_HEADS],
                           jnp.zeros((d, W_BA - 2 * GDN_HEADS), wi.dtype)], axis=1).astype(BF16)
    assert w_r.shape[1] == n_main + W_BA
    qkvz, scd, ba = _inproj(h, attn_norm[l][None, :], w_r)

    prm = jnp.concatenate([_pad_lanes(gdn_a_log[l], GDN_HEADS), _pad_lanes(gdn_dt_bias[l], GDN_HEADS),
                           jnp.zeros((6, LANES), F32)], axis=0)
    y_gdn = _gdn(qkvz.reshape(batch, s, W_QKVZ), ba.reshape(batch, s, W_BA), gdn_conv[l].astype(F32), prm,
                 gdn_out_norm[l][None, :].astype(F32))

    y_att = _attn(scd.reshape(batch, s, W_SCD), sc_conv[l].astype(F32),
                  jnp.tile(diff_q_norm[l], 256 // DIFF_DQK)[None, :].astype(F32),
                  jnp.tile(diff_k_norm[l], 256 // DIFF_DQK)[None, :].astype(F32),
                  diff_lambda[l].astype(F32),
                  jnp.tile(diff_subln[l], LANES // DIFF_DV)[None, :].astype(F32),
                  near_bias, far_bias, lambda_init)

    h2, xt = _outproj(y_gdn.reshape(t, -1), y_att.reshape(t, -1), w_out[l].astype(BF16), h, ffn_norm[l][None, :])

    cn, e1, rk, e2 = _route(xt, peer_wq[l].T.astype(BF16), peer_keys[l].astype(BF16))
    return _peer(xt, peer_u[l].astype(BF16), peer_v[l].T.astype(BF16), cn, e1, rk, e2, h2)


def kernel(x, rel_bias, attn_norm, w_in, gdn_conv, gdn_a_log, gdn_dt_bias, gdn_out_norm, sc_conv,
           diff_q_norm, diff_k_norm, diff_lambda, diff_subln, w_out, ffn_norm, peer_wq, peer_keys,
           peer_u, peer_v):
    batch, s, d = x.shape
    near_bias, far_bias = _bias_tables(rel_bias)
    h = x.reshape(batch * s, d)
    for l in range(w_in.shape[0]):
        h = _layer(h, l, near_bias, far_bias, attn_norm, w_in, gdn_conv, gdn_a_log, gdn_dt_bias,
                   gdn_out_norm, sc_conv, diff_q_norm, diff_k_norm, diff_lambda, diff_subln, w_out,
                   ffn_norm, peer_wq, peer_keys, peer_u, peer_v, batch)
    return h.reshape(batch, s, d)
```

```python
import functools
import math

import jax
import jax.numpy as jnp
import numpy as np
from jax import lax
from jax.experimental import pallas as pl
from jax.experimental.pallas import tpu as pltpu

F32 = jnp.float32
BF16 = jnp.bfloat16
EPS = 1e-6
NEG = -1e30

D_MODEL = 1024
GDN_HEADS = 4
GDN_D = 128
GDN_CONV = 4
GDN_CHUNK = 64
SC_WIDTH = 256
SC_CONV = 3
DIFF_HEADS = 4
DIFF_DV = 64
DIFF_DQK = 32
ATT_BLK = 256
REL_BUCKETS = 32
REL_MAX_DIST = 128
PEER_HEADS = 8
PEER_KEYS = 128
PEER_TOPK = 16
PEER_DHALF = 128
LANES = 128
BF16_ROWS = 16
VMEM_LIMIT = 56 * 1024 * 1024

W_QKVZ = 2048
W_SCD = 1536
W_BA = LANES


def _cparams(*sem):
    return pltpu.CompilerParams(dimension_semantics=sem, vmem_limit_bytes=VMEM_LIMIT)


def _nt_dot(a, b):
    return lax.dot_general(a, b, (((1,), (1,)), ((), ())), preferred_element_type=F32)


def _tn_dot(a, b):
    return lax.dot_general(a, b, (((0,), (0,)), ((), ())), preferred_element_type=F32)


def _dot(a, b):
    return jnp.dot(a, b, preferred_element_type=F32)


def _split3(x):
    hi = x.astype(BF16)
    r = x - hi.astype(F32)
    mid = r.astype(BF16)
    lo = (r - mid.astype(F32)).astype(BF16)
    return hi, mid, lo


def _inproj_body(h_ref, g_ref, w_ref, qkvz_ref, scd_ref, ba_ref):
    x = h_ref[...]
    n = x * lax.rsqrt(jnp.mean(x * x, axis=-1, keepdims=True) + EPS) * g_ref[...]
    nb = n.astype(BF16)
    qkvz_ref[...] = _dot(nb, w_ref[:, 0:W_QKVZ]).astype(BF16)
    scd_ref[...] = _dot(nb, w_ref[:, W_QKVZ:W_QKVZ + W_SCD]).astype(BF16)
    ba_ref[...] = _dot(nb, w_ref[:, W_QKVZ + W_SCD:])


def _inproj(h, gain, w):
    t, d = h.shape
    tm = min(512, t)
    nw = w.shape[1]
    return pl.pallas_call(
        _inproj_body,
        grid=(t // tm,),
        in_specs=[pl.BlockSpec((tm, d), lambda i: (i, 0)),
                  pl.BlockSpec((1, d), lambda i: (0, 0)),
                  pl.BlockSpec((d, nw), lambda i: (0, 0))],
        out_specs=[pl.BlockSpec((tm, W_QKVZ), lambda i: (i, 0)),
                   pl.BlockSpec((tm, W_SCD), lambda i: (i, 0)),
                   pl.BlockSpec((tm, W_BA), lambda i: (i, 0))],
        out_shape=[jax.ShapeDtypeStruct((t, W_QKVZ), BF16),
                   jax.ShapeDtypeStruct((t, W_SCD), BF16),
                   jax.ShapeDtypeStruct((t, W_BA), F32)],
        compiler_params=_cparams("parallel"),
    )(h, gain, w)


def _gdn_body(qkvz_ref, ba_ref, conv_ref, prm_ref, gain_ref, y_ref,
              q_s, k_s, kb_s, vb_s, gb_s, o_s, st_s, u_b, w_b, a_b, qg_b, kd_b):
    s = qkvz_ref.shape[0]
    c_sz, nh = GDN_CHUNK, GDN_HEADS
    nc, rr = s // c_sz, GDN_HEADS * GDN_CHUNK
    row = lax.broadcasted_iota(jnp.int32, (s, LANES), 0)

    ba = ba_ref[...]
    beta = jax.nn.sigmoid(ba)
    xg = ba + prm_ref[1:2, :]
    softplus = jnp.maximum(xg, 0.0) + jnp.log(1.0 + jnp.exp(-jnp.abs(xg)))
    g = -jnp.exp(prm_ref[0:1, :]) * softplus
    pos = row % c_sz
    for sh in (1, 2, 4, 8, 16, 32):
        g = g + jnp.where(pos >= sh, pltpu.roll(g, sh, 0), 0.0)

    def chunked(x):
        return x.reshape(nc, c_sz, LANES)

    for hh in range(nh):
        gb_s[:, hh] = chunked(jnp.broadcast_to(g[:, nh + hh:nh + hh + 1], (s, LANES)))

    scale = GDN_D ** -0.5
    for cb in range(3 * nh):
        x = qkvz_ref[:, cb * LANES:(cb + 1) * LANES].astype(F32)
        w = conv_ref[:, cb * LANES:(cb + 1) * LANES]
        acc = x * w[GDN_CONV - 1:GDN_CONV, :]
        for j in range(GDN_CONV - 1):
            sh = GDN_CONV - 1 - j
            acc = acc + jnp.where(row >= sh, pltpu.roll(x, sh, 0), 0.0) * w[j:j + 1, :]
        y = acc * jax.nn.sigmoid(acc)
        kind, hh = divmod(cb, nh)
        if kind < 2:
            y = y * lax.rsqrt(jnp.sum(y * y, axis=-1, keepdims=True) + EPS)
        if kind == 0:
            q_s[:, hh] = chunked(y * scale)
        elif kind == 1:
            k_s[:, hh] = chunked(y)
            kb_s[:, hh] = chunked(y * beta[:, hh:hh + 1])
        else:
            vb_s[:, hh] = chunked((y * beta[:, hh:hh + 1]).astype(BF16))

    st_s[...] = jnp.zeros_like(st_s)
    ii = lax.broadcasted_iota(jnp.int32, (rr, rr), 0)
    jj = lax.broadcasted_iota(jnp.int32, (rr, rr), 1)
    same_head = (ii // c_sz) == (jj // c_sz)
    tril = jnp.logical_and(same_head, ii >= jj)
    strict = jnp.logical_and(same_head, ii > jj)
    eye = (ii == jj).astype(F32)
    lane = lax.broadcasted_iota(jnp.int32, (rr, LANES), 1)
    pick3 = (lane < 3).astype(BF16)
    row_head = lax.broadcasted_iota(jnp.int32, (rr, LANES), 0) // c_sz

    def stacked(ref, c):
        return ref[c].reshape(rr, LANES)

    def phase_a(c, slot):
        gc = stacked(gb_s, c)
        eg = jnp.exp(gc)
        qc, kc, kb = stacked(q_s, c), stacked(k_s, c), stacked(kb_s, c)
        glast = jnp.broadcast_to(gb_s[c][:, c_sz - 1:c_sz, :], (nh, c_sz, LANES)).reshape(rr, LANES)
        hi, mid, lo = _split3(gc)
        x3 = jnp.where(lane == 0, hi, jnp.where(lane == 1, mid, jnp.where(lane == 2, lo, jnp.zeros_like(lo))))
        grow = _nt_dot(pick3, x3)
        decay = jnp.where(tril, jnp.exp(jnp.minimum(jnp.concatenate([gc, gc], axis=1) - grow, 0.0)), 0.0)
        kq = _nt_dot(jnp.concatenate([kb.astype(BF16), qc.astype(BF16)], axis=0), kc.astype(BF16))
        lower = jnp.where(strict, kq[0:rr] * decay, 0.0)
        a_in = jnp.where(tril, kq[rr:2 * rr] * decay, 0.0)
        pw = -lower
        tm = eye + pw
        pw = _dot(pw.astype(BF16), pw.astype(BF16))
        for _ in range(4):
            pwb = pw.astype(BF16)
            both = _dot(jnp.concatenate([tm.astype(BF16), pwb], axis=0), pwb)
            tm = tm + both[0:rr]
            pw = both[rr:2 * rr]
        tm = tm + _dot(tm.astype(BF16), pw.astype(BF16))
        uw = _dot(tm.astype(BF16), jnp.concatenate([stacked(vb_s, c), (kb * eg).astype(BF16)], axis=1))
        u_b[slot] = uw[:, 0:LANES]
        w_b[slot] = uw[:, LANES:2 * LANES].astype(BF16)
        a_b[slot] = a_in.astype(BF16)
        qg_b[slot] = (qc * eg).astype(BF16)
        kd_b[slot] = (kc * jnp.exp(glast - gc)).astype(BF16)

    def phase_b(c, slot):
        stb = st_s[...].astype(BF16)
        wq = _dot(jnp.concatenate([w_b[slot], qg_b[slot]], axis=0), stb)
        u = u_b[slot]

        def head_blocks(r0):
            return jnp.concatenate([wq[r0 + hh * c_sz:r0 + (hh + 1) * c_sz, hh * LANES:(hh + 1) * LANES]
                                    for hh in range(nh)], axis=0)

        vnb = (u - head_blocks(0)).astype(BF16)
        o_s[c] = (head_blocks(rr) + _dot(a_b[slot], vnb)).reshape(nh, c_sz, LANES)
        vbd = jnp.concatenate([jnp.where(row_head == hh, vnb, jnp.zeros_like(vnb)) for hh in range(nh)], axis=1)
        egl = jnp.concatenate([jnp.exp(gb_s[c][hh, c_sz - 1:c_sz, :]) for hh in range(nh)], axis=1)
        st_s[...] = st_s[...] * egl + _tn_dot(kd_b[slot], vbd)

    phase_a(0, 0)

    def pair(kk, carry):
        c = 2 * kk
        phase_b(c, 0)
        phase_a(c + 1, 1)
        phase_b(c + 1, 1)
        phase_a(jnp.minimum(c + 2, nc - 1), 0)
        return carry

    lax.fori_loop(0, nc // 2, pair, 0)

    gain = gain_ref[...]
    for hh in range(nh):
        o = o_s[:, hh].reshape(s, LANES)
        z = qkvz_ref[:, (3 * nh + hh) * LANES:(3 * nh + hh + 1) * LANES].astype(F32)
        on = o * lax.rsqrt(jnp.mean(o * o, axis=-1, keepdims=True) + EPS) * gain
        y_ref[:, hh * LANES:(hh + 1) * LANES] = (on * (z * jax.nn.sigmoid(z))).astype(BF16)


def _gdn(qkvz, ba, conv_w, prm, gain):
    b, s, _ = qkvz.shape
    hd = GDN_HEADS * GDN_D
    nc, rr = s // GDN_CHUNK, GDN_HEADS * GDN_CHUNK
    per_chunk = (nc, GDN_HEADS, GDN_CHUNK, GDN_D)
    return pl.pallas_call(
        _gdn_body,
        grid=(b,),
        in_specs=[pl.BlockSpec((None, s, W_QKVZ), lambda i: (i, 0, 0)),
                  pl.BlockSpec((None, s, W_BA), lambda i: (i, 0, 0)),
                  pl.BlockSpec((GDN_CONV, 3 * hd), lambda i: (0, 0)),
                  pl.BlockSpec((8, LANES), lambda i: (0, 0)),
                  pl.BlockSpec((1, GDN_D), lambda i: (0, 0))],
        out_specs=pl.BlockSpec((None, s, hd), lambda i: (i, 0, 0)),
        out_shape=jax.ShapeDtypeStruct((b, s, hd), BF16),
        scratch_shapes=[pltpu.VMEM(per_chunk, F32),
                        pltpu.VMEM(per_chunk, F32),
                        pltpu.VMEM(per_chunk, F32),
                        pltpu.VMEM(per_chunk, BF16),
                        pltpu.VMEM(per_chunk, F32),
                        pltpu.VMEM(per_chunk, F32),
                        pltpu.VMEM((GDN_D, hd), F32),
                        pltpu.VMEM((2, rr, GDN_D), F32),
                        pltpu.VMEM((2, rr, GDN_D), BF16),
                        pltpu.VMEM((2, rr, rr), BF16),
                        pltpu.VMEM((2, rr, GDN_D), BF16),
                        pltpu.VMEM((2, rr, GDN_D), BF16)],
        compiler_params=_cparams("parallel"),
    )(qkvz, ba, conv_w, prm, gain)


def _attn_body(far_ref, scd_ref, scw_ref, qg_ref, kg_ref, lam_ref, sub_ref, nbt_ref, y_ref,
               qt_s, kn_s, vt_s, ysc_s, qc_s, m_s, l_s, acc_s, *, lambda_init):
    i = pl.program_id(1)
    s = scd_ref.shape[0]
    blk = ATT_BLK
    o_q, o_k, o_v = 3 * SC_WIDTH, 3 * SC_WIDTH + 256, 3 * SC_WIDTH + 512

    @pl.when(i == 0)
    def _prep():
        row = lax.broadcasted_iota(jnp.int32, (s, SC_WIDTH), 0)
        gate_b = scd_ref[:, 0:SC_WIDTH].astype(F32)
        x = scd_ref[:, SC_WIDTH:2 * SC_WIDTH].astype(F32) * scd_ref[:, 2 * SC_WIDTH:3 * SC_WIDTH].astype(F32)
        w = scw_ref[...]
        acc = x * w[SC_CONV - 1:SC_CONV, :]
        for j in range(SC_CONV - 1):
            sh = SC_CONV - 1 - j
            acc = acc + jnp.where(row >= sh, pltpu.roll(x, sh, 0), 0.0) * w[j:j + 1, :]
        ysc_s[...] = (gate_b * acc).astype(BF16)
        gi = lax.broadcasted_iota(jnp.int32, (256, 256), 0) // DIFF_DQK
        gj = lax.broadcasted_iota(jnp.int32, (256, 256), 1) // DIFF_DQK
        bd = (gi == gj).astype(BF16)

        def normed(off, g_ref, sc):
            xx = scd_ref[:, off:off + 256].astype(F32)
            hi, mid, lo = _split3(xx * xx)
            ss = _dot(hi, bd) + _dot(mid, bd) + _dot(lo, bd)
            return xx * lax.rsqrt(ss * (1.0 / DIFF_DQK) + EPS) * g_ref[...] * sc

        qt_s[...] = normed(o_q, qg_ref, DIFF_DQK ** -0.5).T.astype(BF16)
        kn_s[...] = normed(o_k, kg_ref, 1.0).astype(BF16)
        vt_s[...] = scd_ref[:, o_v:o_v + 256].astype(F32).T.astype(BF16)

    lp = lam_ref[...]
    lam = (jnp.exp(jnp.sum(lp[0:1, :] * lp[1:2, :], axis=-1, keepdims=True))
           - jnp.exp(jnp.sum(lp[2:3, :] * lp[3:4, :], axis=-1, keepdims=True)) + lambda_init)

    r0 = pl.multiple_of(i * blk, blk)
    rowi = lax.broadcasted_iota(jnp.int32, (LANES, blk), 0)
    grp = rowi // DIFF_DQK

    m_s[...] = jnp.full_like(m_s, NEG)
    l_s[...] = jnp.zeros_like(l_s)
    acc_s[...] = jnp.zeros_like(acc_s)
    for hp in range(2):
        qp = qt_s[hp * LANES:(hp + 1) * LANES, pl.ds(r0, blk)]
        for sidx in range(4):
            qc_s[hp, :, sidx * blk:(sidx + 1) * blk] = jnp.where(grp == sidx, qp, jnp.zeros_like(qp))

    def block(c0, bias_of):
        for hp in range(2):
            kp = kn_s[pl.ds(c0, blk), hp * LANES:(hp + 1) * LANES]
            vt = vt_s[hp * LANES:(hp + 1) * LANES, pl.ds(c0, blk)]
            logits = _dot(kp, qc_s[hp])
            b0, b1 = bias_of(2 * hp), bias_of(2 * hp + 1)
            if b0.ndim == 2:
                bias = jnp.concatenate([b0, b0, b1, b1], axis=1)
                logits = logits + bias
            else:
                two = 2 * blk
                logits = jnp.concatenate([logits[:, 0:two] + b0, logits[:, two:2 * two] + b1], axis=1)
            m_old = m_s[hp:hp + 1, :]
            m_new = jnp.maximum(m_old, jnp.max(logits, axis=0, keepdims=True))
            alpha = jnp.exp(m_old - m_new)
            p = jnp.exp(logits - m_new)
            l_s[hp:hp + 1, :] = alpha * l_s[hp:hp + 1, :] + jnp.sum(p, axis=0, keepdims=True)
            acc_s[hp] = alpha * acc_s[hp] + _dot(vt, p.astype(BF16))
            m_s[hp:hp + 1, :] = m_new

    def far_body(kb, carry):
        block(pl.multiple_of(kb * blk, blk), lambda head: far_ref[head])
        return carry

    lax.fori_loop(0, jnp.maximum(i - 1, 0), far_body, 0)

    @pl.when(i >= 1)
    def _prev():
        block(pl.multiple_of((i - 1) * blk, blk), lambda head: nbt_ref[head, 0:blk, :])

    block(r0, lambda head: nbt_ref[head, blk:2 * blk, :])

    low = rowi < DIFF_DV
    for hp in range(2):
        outs = []
        for hh in range(2):
            c0, c1 = (2 * hh) * blk, (2 * hh + 1) * blk
            outs.append(acc_s[hp, :, c0:c0 + blk] / l_s[hp:hp + 1, c0:c0 + blk]
                        - lam * (acc_s[hp, :, c1:c1 + blk] / l_s[hp:hp + 1, c1:c1 + blk]))
        o = jnp.where(low, outs[0], outs[1])
        sq = o * o
        ss0 = jnp.sum(jnp.where(low, sq, 0.0), axis=0, keepdims=True)
        ss1 = jnp.sum(jnp.where(low, 0.0, sq), axis=0, keepdims=True)
        ms = jnp.where(low, ss0, ss1) * (1.0 / DIFF_DV)
        y = (o * lax.rsqrt(ms + EPS)).T * sub_ref[...] * (1.0 - lambda_init)
        y_ref[:, SC_WIDTH + hp * LANES:SC_WIDTH + (hp + 1) * LANES] = y.astype(BF16)
    y_ref[:, 0:SC_WIDTH] = ysc_s[pl.ds(r0, blk), :]


def _attn(scd, sc_w, q_gain, k_gain, lam_p, sub_gain, near_bias_t, far_bias, lambda_init):
    b, s, _ = scd.shape
    blk = ATT_BLK
    const2 = lambda bi, i: (0, 0)
    const3 = lambda bi, i: (0, 0, 0)
    return pl.pallas_call(
        functools.partial(_attn_body, lambda_init=lambda_init),
        grid=(b, s // blk),
        in_specs=[pl.BlockSpec(memory_space=pltpu.SMEM),
                  pl.BlockSpec((None, s, W_SCD), lambda bi, i: (bi, 0, 0)),
                  pl.BlockSpec((SC_CONV, SC_WIDTH), const2),
                  pl.BlockSpec((1, 256), const2),
                  pl.BlockSpec((1, 256), const2),
                  pl.BlockSpec((4, DIFF_DQK), const2),
                  pl.BlockSpec((1, LANES), const2),
                  pl.BlockSpec((DIFF_HEADS, 2 * blk, blk), const3)],
        out_specs=pl.BlockSpec((None, blk, 512), lambda bi, i: (bi, i, 0)),
        out_shape=jax.ShapeDtypeStruct((b, s, 512), BF16),
        scratch_shapes=[pltpu.VMEM((256, s), BF16),
                        pltpu.VMEM((s, 256), BF16),
                        pltpu.VMEM((256, s), BF16),
                        pltpu.VMEM((s, SC_WIDTH), BF16),
                        pltpu.VMEM((2, LANES, 4 * blk), BF16),
                        pltpu.VMEM((2, 4 * blk), F32),
                        pltpu.VMEM((2, 4 * blk), F32),
                        pltpu.VMEM((2, LANES, 4 * blk), F32)],
        compiler_params=_cparams("parallel", "arbitrary"),
    )(far_bias, scd, sc_w, q_gain, k_gain, lam_p, sub_gain, near_bias_t)


def _rel_bucket(rel):
    max_exact = REL_BUCKETS // 2
    n = jnp.maximum(rel, 0)
    large = max_exact + (jnp.log(jnp.maximum(n, max_exact).astype(F32) / max_exact)
                         / math.log(REL_MAX_DIST / max_exact) * (REL_BUCKETS - max_exact)).astype(jnp.int32)
    large = jnp.minimum(large, REL_BUCKETS - 1)
    return jnp.where(n < max_exact, n, large)


def _bias_tables(rel_bias):
    blk = ATT_BLK
    rel = jnp.arange(blk)[:, None] + blk - jnp.arange(2 * blk)[None, :]
    onehot = (_rel_bucket(rel)[None] == jnp.arange(REL_BUCKETS)[:, None, None]).astype(F32)
    near = jnp.einsum("brc,bh->hrc", onehot, rel_bias.astype(F32), precision=lax.Precision.HIGHEST)
    near = jnp.where(rel[None] >= 0, near, NEG)
    return jnp.swapaxes(near, 1, 2), rel_bias[REL_BUCKETS - 1].astype(F32)


def _outproj_body(yg_ref, ya_ref, w_ref, h_ref, g_ref, h2_ref, xt_ref):
    hd = yg_ref.shape[1]
    h2 = h_ref[...] + _dot(yg_ref[...], w_ref[0:hd, :]) + _dot(ya_ref[...], w_ref[hd:, :])
    h2_ref[...] = h2
    n = h2 * lax.rsqrt(jnp.mean(h2 * h2, axis=-1, keepdims=True) + EPS) * g_ref[...]
    xt_ref[...] = n.T.astype(BF16)


def _outproj(yg, ya, w, h, gain):
    t, d = h.shape
    tm = min(512, t)
    return pl.pallas_call(
        _outproj_body,
        grid=(t // tm,),
        in_specs=[pl.BlockSpec((tm, yg.shape[1]), lambda i: (i, 0)),
                  pl.BlockSpec((tm, ya.shape[1]), lambda i: (i, 0)),
                  pl.BlockSpec((d, d), lambda i: (0, 0)),
                  pl.BlockSpec((tm, d), lambda i: (i, 0)),
                  pl.BlockSpec((1, d), lambda i: (0, 0))],
        out_specs=[pl.BlockSpec((tm, d), lambda i: (i, 0)),
                   pl.BlockSpec((d, tm), lambda i: (0, i))],
        out_shape=[jax.ShapeDtypeStruct((t, d), F32),
                   jax.ShapeDtypeStruct((d, t), BF16)],
        compiler_params=_cparams("parallel"),
    )(yg, ya, w, h, gain)


def _cmpx(lst, i, j):
    a, b = lst[i], lst[j]
    lst[i] = jnp.maximum(a, b)
    lst[j] = jnp.minimum(a, b)


def _bitonic_clean(lst, lo, n):
    d = n // 2
    while d >= 1:
        for k in range(n):
            if (k // d) % 2 == 0:
                _cmpx(lst, lo + k, lo + k + d)
        d //= 2


def _sort_desc(lst, lo, n):
    if n == 1:
        return
    h = n // 2
    _sort_desc(lst, lo, h)
    _sort_desc(lst, lo + h, h)
    for k in range(h):
        _cmpx(lst, lo + k, lo + n - 1 - k)
    _bitonic_clean(lst, lo, h)
    _bitonic_clean(lst, lo + h, h)


def _merge_sublanes(lst):
    n = len(lst)
    for d in (4, 2, 1):
        lst = [jnp.maximum(lst[k], pltpu.roll(lst[n - 1 - k], d, 0)) for k in range(n)]
        _bitonic_clean(lst, 0, n)
    return lst


def _top_sorted(s):
    lst = [s[k * 8:(k + 1) * 8, :] for k in range(s.shape[0] // 8)]
    _sort_desc(lst, 0, len(lst))
    return _merge_sublanes(lst)


def _route_body(xt_ref, wqt_ref, keys_ref, cn_ref, e1_ref, rk_ref, e2_ref, q_s):
    k = PEER_TOPK
    tn = xt_ref.shape[1]
    q_s[...] = _dot(wqt_ref[...], xt_ref[...]).astype(BF16)
    sub = lax.broadcasted_iota(jnp.int32, (8, tn), 0)
    for hh in range(PEER_HEADS):
        sc = []
        for p in range(2):
            r = (hh * 2 + p) * PEER_DHALF
            sc.append(_dot(keys_ref[hh, p], q_s[r:r + PEER_DHALF, :]))
        a = _top_sorted(sc[0])
        b = _top_sorted(sc[1])
        apack, bpack = a[0], b[0]
        for r in range(1, 8):
            apack = jnp.where(sub == r, a[r], apack)
            bpack = jnp.where(sub == r, b[r], bpack)
        cand = [apack + b[i] for i in range(k)]
        extra = [a[8 + i] + bpack for i in range(k - 8)]
        for i in range(8, k):
            cand[i] = jnp.maximum(cand[i], extra[k - 1 - i])
        _bitonic_clean(cand, 0, k)
        best = _merge_sublanes(cand)
        z = jnp.zeros_like(best[0])
        for i in range(k):
            z = z + jnp.exp(best[i] - best[0])
        tau = best[k - 1]
        for r in range(PEER_KEYS // 8):
            rows = slice(r * 8, (r + 1) * 8)
            s1r, s2r = sc[0][rows, :], sc[1][rows, :]
            rank = jnp.zeros_like(s2r)
            cnt = jnp.zeros_like(s1r)
            for i in range(k):
                rank = rank + jnp.where(b[i] > s2r, 1.0, 0.0)
                cnt = cnt + jnp.where(s1r + b[i] >= tau, 1.0, 0.0)
            rk_ref[hh, rows, :] = rank.astype(BF16)
            cn_ref[hh, rows, :] = cnt
        e1_ref[hh] = jnp.exp(sc[0] - a[0][0:1, :])
        e2_ref[hh] = (jnp.exp(sc[1] - b[0][0:1, :]) / z[0:1, :]).astype(BF16)


def _route(xt, wqt, keys):
    d, t = xt.shape
    tn = min(512, t)
    nq = wqt.shape[0]
    hk = (PEER_HEADS, PEER_KEYS, tn)
    spec = pl.BlockSpec(hk, lambda i: (0, 0, i))
    shp = jax.ShapeDtypeStruct((PEER_HEADS, PEER_KEYS, t), BF16)
    shp32 = jax.ShapeDtypeStruct((PEER_HEADS, PEER_KEYS, t), F32)
    return pl.pallas_call(
        _route_body,
        grid=(t // tn,),
        in_specs=[pl.BlockSpec((d, tn), lambda i: (0, i)),
                  pl.BlockSpec((nq, d), lambda i: (0, 0)),
                  pl.BlockSpec((PEER_HEADS, 2, PEER_KEYS, PEER_DHALF), lambda i: (0, 0, 0, 0))],
        out_specs=[spec, spec, spec, spec],
        out_shape=[shp32, shp32, shp, shp],
        scratch_shapes=[pltpu.VMEM((nq, tn), BF16)],
        compiler_params=_cparams("parallel"),
    )(xt, wqt, keys)


def _gelu(x):
    return 0.5 * x * (1.0 + lax.erf(x * (0.5 ** 0.5)))


def _peer_body(xt_ref, u_ref, vt_ref, cn_ref, e1_ref, rk_ref, e2_ref, h2_ref, o_ref,
               hid_s, coef_s, acc_s, rk_s, e2_s):
    e = pl.program_id(1)
    eb, tn = hid_s.shape
    n_i = eb // PEER_KEYS

    @pl.when(e == 0)
    def _init():
        acc_s[...] = jnp.zeros_like(acc_s)
        rk_s[...] = rk_ref[...]
        e2_s[...] = e2_ref[...]

    hid_s[...] = _dot(u_ref[...], xt_ref[...])

    i0 = pl.multiple_of(e * n_i, n_i)
    for cb in range(tn // LANES):
        cs = slice(cb * LANES, (cb + 1) * LANES)
        for j in range(n_i):
            g = jnp.zeros((PEER_KEYS // BF16_ROWS, BF16_ROWS, LANES), BF16)
            for hh in range(PEER_HEADS):
                cnt = jnp.broadcast_to(cn_ref[hh, pl.ds(i0, n_i), cs][j:j + 1, :], (BF16_ROWS, LANES)).astype(BF16)
                ra = jnp.broadcast_to(e1_ref[hh, pl.ds(i0, n_i), cs][j:j + 1, :], (BF16_ROWS, LANES)).astype(BF16)
                e2 = e2_s[hh, :, :, cs]
                g = g + jnp.where(rk_s[hh, :, :, cs] < cnt[None], e2, jnp.zeros_like(e2)) * ra[None]
            rows = slice(j * PEER_KEYS, (j + 1) * PEER_KEYS)
            coef_s[rows, cs] = g.reshape(PEER_KEYS, LANES) * _gelu(hid_s[rows, cs]).astype(BF16)

    acc_s[...] += _dot(vt_ref[...], coef_s[...])

    @pl.when(e == pl.num_programs(1) - 1)
    def _fin():
        o_ref[...] = h2_ref[...] + acc_s[...].T


def _peer(xt, u, vt, cn, e1, rk, e2, h2):
    d, t = xt.shape
    n_exp = u.shape[0]
    tn = min(512, t)
    eb = 1024
    hk = pl.BlockSpec((PEER_HEADS, PEER_KEYS, tn), lambda i, e: (0, 0, i))
    hk16 = pl.BlockSpec((PEER_HEADS, PEER_KEYS // BF16_ROWS, BF16_ROWS, tn), lambda i, e: (0, 0, 0, i))
    rk = rk.reshape(PEER_HEADS, PEER_KEYS // BF16_ROWS, BF16_ROWS, t)
    e2 = e2.reshape(PEER_HEADS, PEER_KEYS // BF16_ROWS, BF16_ROWS, t)
    return pl.pallas_call(
        _peer_body,
        grid=(t // tn, n_exp // eb),
        in_specs=[pl.BlockSpec((d, tn), lambda i, e: (0, i)),
                  pl.BlockSpec((eb, d), lambda i, e: (e, 0)),
                  pl.BlockSpec((d, eb), lambda i, e: (0, e)),
                  hk, hk, hk16, hk16,
                  pl.BlockSpec((tn, d), lambda i, e: (i, 0))],
        out_specs=pl.BlockSpec((tn, d), lambda i, e: (i, 0)),
        out_shape=jax.ShapeDtypeStruct((t, d), F32),
        scratch_shapes=[pltpu.VMEM((eb, tn), F32),
                        pltpu.VMEM((eb, tn), BF16),
                        pltpu.VMEM((d, tn), F32),
                        pltpu.VMEM((PEER_HEADS, PEER_KEYS // BF16_ROWS, BF16_ROWS, tn), BF16),
                        pltpu.VMEM((PEER_HEADS, PEER_KEYS // BF16_ROWS, BF16_ROWS, tn), BF16)],
        compiler_params=_cparams("parallel", "arbitrary"),
    )(xt, u, vt, cn, e1, rk, e2, h2)


def _pad_lanes(v, offset):
    return jnp.zeros((1, LANES), F32).at[0, offset:offset + v.shape[0]].set(v.astype(F32))


def _layer(h, l, near_bias, far_bias, attn_norm, w_in, gdn_conv, gdn_a_log, gdn_dt_bias, gdn_out_norm,
           sc_conv, diff_q_norm, diff_k_norm, diff_lambda, diff_subln, w_out, ffn_norm,
           peer_wq, peer_keys, peer_u, peer_v, batch):
    t, d = h.shape
    s = t // batch
    lambda_init = 0.8 - 0.6 * math.exp(-0.3 * l)
    n_main = W_QKVZ + W_SCD
    wi = w_in[l]
    w_r = jnp.concatenate([wi[:, 0:W_QKVZ], wi[:, W_QKVZ + 2 * GDN_HEADS:], wi[:, W_QKVZ:W_QKVZ + 2 * GDN_HEADS],
                           jnp.zeros((d, W_BA - 2 * GDN_HEADS), wi.dtype)], axis=1).astype(BF16)
    assert w_r.shape[1] == n_main + W_BA
    qkvz, scd, ba = _inproj(h, attn_norm[l][None, :], w_r)

    prm = jnp.concatenate([_pad_lanes(gdn_a_log[l], GDN_HEADS), _pad_lanes(gdn_dt_bias[l], GDN_HEADS),
                           jnp.zeros((6, LANES), F32)], axis=0)
    y_gdn = _gdn(qkvz.reshape(batch, s, W_QKVZ), ba.reshape(batch, s, W_BA), gdn_conv[l].astype(F32), prm,
                 gdn_out_norm[l][None, :].astype(F32))

    y_att = _attn(scd.reshape(batch, s, W_SCD), sc_conv[l].astype(F32),
                  jnp.tile(diff_q_norm[l], 256 // DIFF_DQK)[None, :].astype(F32),
                  jnp.tile(diff_k_norm[l], 256 // DIFF_DQK)[None, :].astype(F32),
                  diff_lambda[l].astype(F32),
                  jnp.tile(diff_subln[l], LANES // DIFF_DV)[None, :].astype(F32),
                  near_bias, far_bias, lambda_init)

    h2, xt = _outproj(y_gdn.reshape(t, -1), y_att.reshape(t, -1), w_out[l].astype(BF16), h, ffn_norm[l][None, :])

    cn, e1, rk, e2 = _route(xt, peer_wq[l].T.astype(BF16), peer_keys[l].astype(BF16))
    return _peer(xt, peer_u[l].astype(BF16), peer_v[l].T.astype(BF16), cn, e1, rk, e2, h2)


def kernel(x, rel_bias, attn_norm, w_in, gdn_conv, gdn_a_log, gdn_dt_bias, gdn_out_norm, sc_conv,
           diff_q_norm, diff_k_norm, diff_lambda, diff_subln, w_out, ffn_norm, peer_wq, peer_keys,
           peer_u, peer_v):
    batch, s, d = x.shape
    near_bias, far_bias = _bias_tables(rel_bias)
    h = x.reshape(batch * s, d)
    for l in range(w_in.shape[0]):
        h = _layer(h, l, near_bias, far_bias, attn_norm, w_in, gdn_conv, gdn_a_log, gdn_dt_bias,
                   gdn_out_norm, sc_conv, diff_q_norm, diff_k_norm, diff_lambda, diff_subln, w_out,
                   ffn_norm, peer_wq, peer_keys, peer_u, peer_v, batch)
    return h.reshape(batch, s, d)
```

```python
import functools
import math

import jax
import jax.numpy as jnp
import numpy as np
from jax import lax
from jax.experimental import pallas as pl
from jax.experimental.pallas import tpu as pltpu

F32 = jnp.float32
BF16 = jnp.bfloat16
EPS = 1e-6
NEG = -1e30

D_MODEL = 1024
GDN_HEADS = 4
GDN_D = 128
GDN_CONV = 4
GDN_CHUNK = 64
SC_WIDTH = 256
SC_CONV = 3
DIFF_HEADS = 4
DIFF_DV = 64
DIFF_DQK = 32
ATT_BLK = 256
REL_BUCKETS = 32
REL_MAX_DIST = 128
PEER_HEADS = 8
PEER_KEYS = 128
PEER_TOPK = 16
PEER_DHALF = 128
LANES = 128
BF16_ROWS = 16
TOKEN_BLK = 512
PEER_EXPERT_BLK = 1024
VMEM_LIMIT = 56 * 1024 * 1024

W_QKVZ = 2048
W_SCD = 1536
W_BA = LANES


def _cparams(*sem):
    return pltpu.CompilerParams(dimension_semantics=sem, vmem_limit_bytes=VMEM_LIMIT)


def _nt_dot(a, b):
    return lax.dot_general(a, b, (((1,), (1,)), ((), ())), preferred_element_type=F32)


def _tn_dot(a, b):
    return lax.dot_general(a, b, (((0,), (0,)), ((), ())), preferred_element_type=F32)


def _dot(a, b):
    return jnp.dot(a, b, preferred_element_type=F32)


def _split3(x):
    hi = x.astype(BF16)
    r = x - hi.astype(F32)
    mid = r.astype(BF16)
    lo = (r - mid.astype(F32)).astype(BF16)
    return hi, mid, lo


def _inproj_body(h_ref, g_ref, w_ref, qkvz_ref, scd_ref, ba_ref):
    x = h_ref[...]
    n = x * lax.rsqrt(jnp.mean(x * x, axis=-1, keepdims=True) + EPS) * g_ref[...]
    nb = n.astype(BF16)
    qkvz_ref[...] = _dot(nb, w_ref[:, 0:W_QKVZ]).astype(BF16)
    scd_ref[...] = _dot(nb, w_ref[:, W_QKVZ:W_QKVZ + W_SCD]).astype(BF16)
    ba_ref[...] = _dot(nb, w_ref[:, W_QKVZ + W_SCD:])


def _inproj(h, gain, w):
    t, d = h.shape
    tm = min(512, t)
    nw = w.shape[1]
    return pl.pallas_call(
        _inproj_body,
        grid=(t // tm,),
        in_specs=[pl.BlockSpec((tm, d), lambda i: (i, 0)),
                  pl.BlockSpec((1, d), lambda i: (0, 0)),
                  pl.BlockSpec((d, nw), lambda i: (0, 0))],
        out_specs=[pl.BlockSpec((tm, W_QKVZ), lambda i: (i, 0)),
                   pl.BlockSpec((tm, W_SCD), lambda i: (i, 0)),
                   pl.BlockSpec((tm, W_BA), lambda i: (i, 0))],
        out_shape=[jax.ShapeDtypeStruct((t, W_QKVZ), BF16),
                   jax.ShapeDtypeStruct((t, W_SCD), BF16),
                   jax.ShapeDtypeStruct((t, W_BA), F32)],
        compiler_params=_cparams("parallel"),
    )(h, gain, w)


def _gdn_body(qkvz_ref, ba_ref, conv_ref, prm_ref, gain_ref, y_ref,
              q_s, k_s, kb_s, vb_s, gb_s, o_s, st_s, u_b, w_b, a_b, qg_b, kd_b):
    s = qkvz_ref.shape[0]
    c_sz, nh = GDN_CHUNK, GDN_HEADS
    nc, rr = s // c_sz, GDN_HEADS * GDN_CHUNK
    row = lax.broadcasted_iota(jnp.int32, (s, LANES), 0)

    ba = ba_ref[...]
    beta = jax.nn.sigmoid(ba)
    xg = ba + prm_ref[1:2, :]
    softplus = jnp.maximum(xg, 0.0) + jnp.log(1.0 + jnp.exp(-jnp.abs(xg)))
    g = -jnp.exp(prm_ref[0:1, :]) * softplus
    pos = row % c_sz
    for sh in (1, 2, 4, 8, 16, 32):
        g = g + jnp.where(pos >= sh, pltpu.roll(g, sh, 0), 0.0)

    def chunked(x):
        return x.reshape(nc, c_sz, LANES)

    for hh in range(nh):
        gb_s[:, hh] = chunked(jnp.broadcast_to(g[:, nh + hh:nh + hh + 1], (s, LANES)))

    scale = GDN_D ** -0.5
    for cb in range(3 * nh):
        x = qkvz_ref[:, cb * LANES:(cb + 1) * LANES].astype(F32)
        w = conv_ref[:, cb * LANES:(cb + 1) * LANES]
        acc = x * w[GDN_CONV - 1:GDN_CONV, :]
        for j in range(GDN_CONV - 1):
            sh = GDN_CONV - 1 - j
            acc = acc + jnp.where(row >= sh, pltpu.roll(x, sh, 0), 0.0) * w[j:j + 1, :]
        y = acc * jax.nn.sigmoid(acc)
        kind, hh = divmod(cb, nh)
        if kind < 2:
            y = y * lax.rsqrt(jnp.sum(y * y, axis=-1, keepdims=True) + EPS)
        if kind == 0:
            q_s[:, hh] = chunked(y * scale)
        elif kind == 1:
            k_s[:, hh] = chunked(y)
            kb_s[:, hh] = chunked(y * beta[:, hh:hh + 1])
        else:
            vb_s[:, hh] = chunked((y * beta[:, hh:hh + 1]).astype(BF16))

    st_s[...] = jnp.zeros_like(st_s)
    ii = lax.broadcasted_iota(jnp.int32, (rr, rr), 0)
    jj = lax.broadcasted_iota(jnp.int32, (rr, rr), 1)
    same_head = (ii // c_sz) == (jj // c_sz)
    tril = jnp.logical_and(same_head, ii >= jj)
    strict = jnp.logical_and(same_head, ii > jj)
    eye = (ii == jj).astype(F32)
    lane = lax.broadcasted_iota(jnp.int32, (rr, LANES), 1)
    pick3 = (lane < 3).astype(BF16)
    row_head = lax.broadcasted_iota(jnp.int32, (rr, LANES), 0) // c_sz

    def stacked(ref, c):
        return ref[c].reshape(rr, LANES)

    def phase_a(c, slot):
        gc = stacked(gb_s, c)
        eg = jnp.exp(gc)
        qc, kc, kb = stacked(q_s, c), stacked(k_s, c), stacked(kb_s, c)
        glast = jnp.broadcast_to(gb_s[c][:, c_sz - 1:c_sz, :], (nh, c_sz, LANES)).reshape(rr, LANES)
        hi, mid, lo = _split3(gc)
        x3 = jnp.where(lane == 0, hi, jnp.where(lane == 1, mid, jnp.where(lane == 2, lo, jnp.zeros_like(lo))))
        grow = _nt_dot(pick3, x3)
        decay = jnp.where(tril, jnp.exp(jnp.minimum(jnp.concatenate([gc, gc], axis=1) - grow, 0.0)), 0.0)
        kq = _nt_dot(jnp.concatenate([kb.astype(BF16), qc.astype(BF16)], axis=0), kc.astype(BF16))
        lower = jnp.where(strict, kq[0:rr] * decay, 0.0)
        a_in = jnp.where(tril, kq[rr:2 * rr] * decay, 0.0)
        pw = -lower
        tm = eye + pw
        pw = _dot(pw.astype(BF16), pw.astype(BF16))
        for _ in range(4):
            pwb = pw.astype(BF16)
            both = _dot(jnp.concatenate([tm.astype(BF16), pwb], axis=0), pwb)
            tm = tm + both[0:rr]
            pw = both[rr:2 * rr]
        tm = tm + _dot(tm.astype(BF16), pw.astype(BF16))
        uw = _dot(tm.astype(BF16), jnp.concatenate([stacked(vb_s, c), (kb * eg).astype(BF16)], axis=1))
        u_b[slot] = uw[:, 0:LANES]
        w_b[slot] = uw[:, LANES:2 * LANES].astype(BF16)
        a_b[slot] = a_in.astype(BF16)
        qg_b[slot] = (qc * eg).astype(BF16)
        kd_b[slot] = (kc * jnp.exp(glast - gc)).astype(BF16)

    def phase_b(c, slot):
        stb = st_s[...].astype(BF16)
        wq = _dot(jnp.concatenate([w_b[slot], qg_b[slot]], axis=0), stb)
        u = u_b[slot]

        def head_blocks(r0):
            return jnp.concatenate([wq[r0 + hh * c_sz:r0 + (hh + 1) * c_sz, hh * LANES:(hh + 1) * LANES]
                                    for hh in range(nh)], axis=0)

        vnb = (u - head_blocks(0)).astype(BF16)
        o_s[c] = (head_blocks(rr) + _dot(a_b[slot], vnb)).reshape(nh, c_sz, LANES)
        vbd = jnp.concatenate([jnp.where(row_head == hh, vnb, jnp.zeros_like(vnb)) for hh in range(nh)], axis=1)
        egl = jnp.concatenate([jnp.exp(gb_s[c][hh, c_sz - 1:c_sz, :]) for hh in range(nh)], axis=1)
        st_s[...] = st_s[...] * egl + _tn_dot(kd_b[slot], vbd)

    phase_a(0, 0)

    def pair(kk, carry):
        c = 2 * kk
        phase_b(c, 0)
        phase_a(c + 1, 1)
        phase_b(c + 1, 1)
        phase_a(jnp.minimum(c + 2, nc - 1), 0)
        return carry

    lax.fori_loop(0, nc // 2, pair, 0)

    gain = gain_ref[...]
    for hh in range(nh):
        o = o_s[:, hh].reshape(s, LANES)
        z = qkvz_ref[:, (3 * nh + hh) * LANES:(3 * nh + hh + 1) * LANES].astype(F32)
        on = o * lax.rsqrt(jnp.mean(o * o, axis=-1, keepdims=True) + EPS) * gain
        y_ref[:, hh * LANES:(hh + 1) * LANES] = (on * (z * jax.nn.sigmoid(z))).astype(BF16)


def _gdn(qkvz, ba, conv_w, prm, gain):
    b, s, _ = qkvz.shape
    hd = GDN_HEADS * GDN_D
    nc, rr = s // GDN_CHUNK, GDN_HEADS * GDN_CHUNK
    per_chunk = (nc, GDN_HEADS, GDN_CHUNK, GDN_D)
    return pl.pallas_call(
        _gdn_body,
        grid=(b,),
        in_specs=[pl.BlockSpec((None, s, W_QKVZ), lambda i: (i, 0, 0)),
                  pl.BlockSpec((None, s, W_BA), lambda i: (i, 0, 0)),
                  pl.BlockSpec((GDN_CONV, 3 * hd), lambda i: (0, 0)),
                  pl.BlockSpec((8, LANES), lambda i: (0, 0)),
                  pl.BlockSpec((1, GDN_D), lambda i: (0, 0))],
        out_specs=pl.BlockSpec((None, s, hd), lambda i: (i, 0, 0)),
        out_shape=jax.ShapeDtypeStruct((b, s, hd), BF16),
        scratch_shapes=[pltpu.VMEM(per_chunk, F32),
                        pltpu.VMEM(per_chunk, F32),
                        pltpu.VMEM(per_chunk, F32),
                        pltpu.VMEM(per_chunk, BF16),
                        pltpu.VMEM(per_chunk, F32),
                        pltpu.VMEM(per_chunk, F32),
                        pltpu.VMEM((GDN_D, hd), F32),
                        pltpu.VMEM((2, rr, GDN_D), F32),
                        pltpu.VMEM((2, rr, GDN_D), BF16),
                        pltpu.VMEM((2, rr, rr), BF16),
                        pltpu.VMEM((2, rr, GDN_D), BF16),
                        pltpu.VMEM((2, rr, GDN_D), BF16)],
        compiler_params=_cparams("parallel"),
    )(qkvz, ba, conv_w, prm, gain)


def _attn_body(far_ref, scd_ref, scw_ref, qg_ref, kg_ref, lam_ref, sub_ref, nbt_ref, y_ref,
               qt_s, kn_s, vt_s, ysc_s, qc_s, m_s, l_s, acc_s, *, lambda_init):
    i = pl.program_id(1)
    s = scd_ref.shape[0]
    blk = ATT_BLK
    o_q, o_k, o_v = 3 * SC_WIDTH, 3 * SC_WIDTH + 256, 3 * SC_WIDTH + 512

    @pl.when(i == 0)
    def _prep():
        row = lax.broadcasted_iota(jnp.int32, (s, SC_WIDTH), 0)
        gate_b = scd_ref[:, 0:SC_WIDTH].astype(F32)
        x = scd_ref[:, SC_WIDTH:2 * SC_WIDTH].astype(F32) * scd_ref[:, 2 * SC_WIDTH:3 * SC_WIDTH].astype(F32)
        w = scw_ref[...]
        acc = x * w[SC_CONV - 1:SC_CONV, :]
        for j in range(SC_CONV - 1):
            sh = SC_CONV - 1 - j
            acc = acc + jnp.where(row >= sh, pltpu.roll(x, sh, 0), 0.0) * w[j:j + 1, :]
        ysc_s[...] = (gate_b * acc).astype(BF16)
        gi = lax.broadcasted_iota(jnp.int32, (256, 256), 0) // DIFF_DQK
        gj = lax.broadcasted_iota(jnp.int32, (256, 256), 1) // DIFF_DQK
        bd = (gi == gj).astype(BF16)

        def normed(off, g_ref, sc):
            xx = scd_ref[:, off:off + 256].astype(F32)
            hi, mid, lo = _split3(xx * xx)
            ss = _dot(hi, bd) + _dot(mid, bd) + _dot(lo, bd)
            return xx * lax.rsqrt(ss * (1.0 / DIFF_DQK) + EPS) * g_ref[...] * sc

        qt_s[...] = normed(o_q, qg_ref, DIFF_DQK ** -0.5).T.astype(BF16)
        kn_s[...] = normed(o_k, kg_ref, 1.0).astype(BF16)
        vt_s[...] = scd_ref[:, o_v:o_v + 256].astype(F32).T.astype(BF16)

    lp = lam_ref[...]
    lam = (jnp.exp(jnp.sum(lp[0:1, :] * lp[1:2, :], axis=-1, keepdims=True))
           - jnp.exp(jnp.sum(lp[2:3, :] * lp[3:4, :], axis=-1, keepdims=True)) + lambda_init)

    r0 = pl.multiple_of(i * blk, blk)
    rowi = lax.broadcasted_iota(jnp.int32, (LANES, blk), 0)
    grp = rowi // DIFF_DQK

    m_s[...] = jnp.full_like(m_s, NEG)
    l_s[...] = jnp.zeros_like(l_s)
    acc_s[...] = jnp.zeros_like(acc_s)
    for hp in range(2):
        qp = qt_s[hp * LANES:(hp + 1) * LANES, pl.ds(r0, blk)]
        for sidx in range(4):
            qc_s[hp, :, sidx * blk:(sidx + 1) * blk] = jnp.where(grp == sidx, qp, jnp.zeros_like(qp))

    def block(c0, bias_of):
        for hp in range(2):
            kp = kn_s[pl.ds(c0, blk), hp * LANES:(hp + 1) * LANES]
            vt = vt_s[hp * LANES:(hp + 1) * LANES, pl.ds(c0, blk)]
            logits = _dot(kp, qc_s[hp])
            b0, b1 = bias_of(2 * hp), bias_of(2 * hp + 1)
            if b0.ndim == 2:
                bias = jnp.concatenate([b0, b0, b1, b1], axis=1)
                logits = logits + bias
            else:
                two = 2 * blk
                logits = jnp.concatenate([logits[:, 0:two] + b0, logits[:, two:2 * two] + b1], axis=1)
            m_old = m_s[hp:hp + 1, :]
            m_new = jnp.maximum(m_old, jnp.max(logits, axis=0, keepdims=True))
            alpha = jnp.exp(m_old - m_new)
            p = jnp.exp(logits - m_new)
            l_s[hp:hp + 1, :] = alpha * l_s[hp:hp + 1, :] + jnp.sum(p, axis=0, keepdims=True)
            acc_s[hp] = alpha * acc_s[hp] + _dot(vt, p.astype(BF16))
            m_s[hp:hp + 1, :] = m_new

    def far_body(kb, carry):
        block(pl.multiple_of(kb * blk, blk), lambda head: far_ref[head])
        return carry

    lax.fori_loop(0, jnp.maximum(i - 1, 0), far_body, 0)

    @pl.when(i >= 1)
    def _prev():
        block(pl.multiple_of((i - 1) * blk, blk), lambda head: nbt_ref[head, 0:blk, :])

    block(r0, lambda head: nbt_ref[head, blk:2 * blk, :])

    low = rowi < DIFF_DV
    for hp in range(2):
        outs = []
        for hh in range(2):
            c0, c1 = (2 * hh) * blk, (2 * hh + 1) * blk
            outs.append(acc_s[hp, :, c0:c0 + blk] / l_s[hp:hp + 1, c0:c0 + blk]
                        - lam * (acc_s[hp, :, c1:c1 + blk] / l_s[hp:hp + 1, c1:c1 + blk]))
        o = jnp.where(low, outs[0], outs[1])
        sq = o * o
        ss0 = jnp.sum(jnp.where(low, sq, 0.0), axis=0, keepdims=True)
        ss1 = jnp.sum(jnp.where(low, 0.0, sq), axis=0, keepdims=True)
        ms = jnp.where(low, ss0, ss1) * (1.0 / DIFF_DV)
        y = (o * lax.rsqrt(ms + EPS)).T * sub_ref[...] * (1.0 - lambda_init)
        y_ref[:, SC_WIDTH + hp * LANES:SC_WIDTH + (hp + 1) * LANES] = y.astype(BF16)
    y_ref[:, 0:SC_WIDTH] = ysc_s[pl.ds(r0, blk), :]


def _attn(scd, sc_w, q_gain, k_gain, lam_p, sub_gain, near_bias_t, far_bias, lambda_init):
    b, s, _ = scd.shape
    blk = ATT_BLK
    const2 = lambda bi, i: (0, 0)
    const3 = lambda bi, i: (0, 0, 0)
    return pl.pallas_call(
        functools.partial(_attn_body, lambda_init=lambda_init),
        grid=(b, s // blk),
        in_specs=[pl.BlockSpec(memory_space=pltpu.SMEM),
                  pl.BlockSpec((None, s, W_SCD), lambda bi, i: (bi, 0, 0)),
                  pl.BlockSpec((SC_CONV, SC_WIDTH), const2),
                  pl.BlockSpec((1, 256), const2),
                  pl.BlockSpec((1, 256), const2),
                  pl.BlockSpec((4, DIFF_DQK), const2),
                  pl.BlockSpec((1, LANES), const2),
                  pl.BlockSpec((DIFF_HEADS, 2 * blk, blk), const3)],
        out_specs=pl.BlockSpec((None, blk, 512), lambda bi, i: (bi, i, 0)),
        out_shape=jax.ShapeDtypeStruct((b, s, 512), BF16),
        scratch_shapes=[pltpu.VMEM((256, s), BF16),
                        pltpu.VMEM((s, 256), BF16),
                        pltpu.VMEM((256, s), BF16),
                        pltpu.VMEM((s, SC_WIDTH), BF16),
                        pltpu.VMEM((2, LANES, 4 * blk), BF16),
                        pltpu.VMEM((2, 4 * blk), F32),
                        pltpu.VMEM((2, 4 * blk), F32),
                        pltpu.VMEM((2, LANES, 4 * blk), F32)],
        compiler_params=_cparams("parallel", "arbitrary"),
    )(far_bias, scd, sc_w, q_gain, k_gain, lam_p, sub_gain, near_bias_t)


def _rel_bucket(rel):
    max_exact = REL_BUCKETS // 2
    n = jnp.maximum(rel, 0)
    large = max_exact + (jnp.log(jnp.maximum(n, max_exact).astype(F32) / max_exact)
                         / math.log(REL_MAX_DIST / max_exact) * (REL_BUCKETS - max_exact)).astype(jnp.int32)
    large = jnp.minimum(large, REL_BUCKETS - 1)
    return jnp.where(n < max_exact, n, large)


def _bias_tables(rel_bias):
    blk = ATT_BLK
    rel = jnp.arange(blk)[:, None] + blk - jnp.arange(2 * blk)[None, :]
    onehot = (_rel_bucket(rel)[None] == jnp.arange(REL_BUCKETS)[:, None, None]).astype(F32)
    near = jnp.einsum("brc,bh->hrc", onehot, rel_bias.astype(F32), precision=lax.Precision.HIGHEST)
    near = jnp.where(rel[None] >= 0, near, NEG)
    return jnp.swapaxes(near, 1, 2), rel_bias[REL_BUCKETS - 1].astype(F32)


def _outproj_body(yg_ref, ya_ref, w_ref, h_ref, g_ref, h2_ref, xt_ref):
    hd = yg_ref.shape[1]
    h2 = h_ref[...] + _dot(yg_ref[...], w_ref[0:hd, :]) + _dot(ya_ref[...], w_ref[hd:, :])
    h2_ref[...] = h2
    n = h2 * lax.rsqrt(jnp.mean(h2 * h2, axis=-1, keepdims=True) + EPS) * g_ref[...]
    xt_ref[...] = n.T.astype(BF16)


def _outproj(yg, ya, w, h, gain):
    t, d = h.shape
    tm = min(TOKEN_BLK, t)
    return pl.pallas_call(
        _outproj_body,
        grid=(t // tm,),
        in_specs=[pl.BlockSpec((tm, yg.shape[1]), lambda i: (i, 0)),
                  pl.BlockSpec((tm, ya.shape[1]), lambda i: (i, 0)),
                  pl.BlockSpec((d, d), lambda i: (0, 0)),
                  pl.BlockSpec((tm, d), lambda i: (i, 0)),
                  pl.BlockSpec((1, d), lambda i: (0, 0))],
        out_specs=[pl.BlockSpec((tm, d), lambda i: (i, 0)),
                   pl.BlockSpec((None, d, tm), lambda i: (i, 0, 0))],
        out_shape=[jax.ShapeDtypeStruct((t, d), F32),
                   jax.ShapeDtypeStruct((t // tm, d, tm), BF16)],
        compiler_params=_cparams("parallel"),
    )(yg, ya, w, h, gain)


def _cmpx(lst, i, j):
    a, b = lst[i], lst[j]
    lst[i] = jnp.maximum(a, b)
    lst[j] = jnp.minimum(a, b)


def _bitonic_clean(lst, lo, n):
    d = n // 2
    while d >= 1:
        for k in range(n):
            if (k // d) % 2 == 0:
                _cmpx(lst, lo + k, lo + k + d)
        d //= 2


def _sort_desc(lst, lo, n):
    if n == 1:
        return
    h = n // 2
    _sort_desc(lst, lo, h)
    _sort_desc(lst, lo + h, h)
    for k in range(h):
        _cmpx(lst, lo + k, lo + n - 1 - k)
    _bitonic_clean(lst, lo, h)
    _bitonic_clean(lst, lo + h, h)


def _merge_sublanes(lst):
    n = len(lst)
    for d in (4, 2, 1):
        lst = [jnp.maximum(lst[k], pltpu.roll(lst[n - 1 - k], d, 0)) for k in range(n)]
        _bitonic_clean(lst, 0, n)
    return lst


def _top_sorted(s):
    lst = [s[k * 8:(k + 1) * 8, :] for k in range(s.shape[0] // 8)]
    _sort_desc(lst, 0, len(lst))
    return _merge_sublanes(lst)


def _route_body(xt_ref, wqt_ref, keys_ref, cn_ref, e1_ref, rk_ref, e2_ref, q_s):
    k = PEER_TOPK
    tn = xt_ref.shape[1]
    q_s[...] = _dot(wqt_ref[...], xt_ref[...]).astype(BF16)
    sub = lax.broadcasted_iota(jnp.int32, (8, tn), 0)
    for hh in range(PEER_HEADS):
        sc = []
        for p in range(2):
            r = (hh * 2 + p) * PEER_DHALF
            sc.append(_dot(keys_ref[hh, p], q_s[r:r + PEER_DHALF, :]))
        a = _top_sorted(sc[0])
        b = _top_sorted(sc[1])
        apack, bpack = a[0], b[0]
        for r in range(1, 8):
            apack = jnp.where(sub == r, a[r], apack)
            bpack = jnp.where(sub == r, b[r], bpack)
        cand = [apack + b[i] for i in range(k)]
        extra = [a[8 + i] + bpack for i in range(k - 8)]
        for i in range(8, k):
            cand[i] = jnp.maximum(cand[i], extra[k - 1 - i])
        _bitonic_clean(cand, 0, k)
        best = _merge_sublanes(cand)
        z = jnp.zeros_like(best[0])
        for i in range(k):
            z = z + jnp.exp(best[i] - best[0])
        tau = best[k - 1]
        for r in range(PEER_KEYS // 8):
            rows = slice(r * 8, (r + 1) * 8)
            s1r, s2r = sc[0][rows, :], sc[1][rows, :]
            rank = jnp.zeros_like(s2r)
            cnt = jnp.zeros_like(s1r)
            for i in range(k):
                rank = rank + jnp.where(b[i] > s2r, 1.0, 0.0)
                cnt = cnt + jnp.where(s1r + b[i] >= tau, 1.0, 0.0)
            rk_ref[hh, rows, :] = rank.astype(BF16)
            cn_ref[hh, rows, :] = cnt
        e1_ref[hh] = jnp.exp(sc[0] - a[0][0:1, :])
        e2_ref[hh] = (jnp.exp(sc[1] - b[0][0:1, :]) / z[0:1, :]).astype(BF16)


def _route(xt, wqt, keys):
    nb, d, tn = xt.shape
    nq = wqt.shape[0]
    spec = pl.BlockSpec((None, PEER_HEADS, PEER_KEYS, tn), lambda i: (i, 0, 0, 0))
    shp = jax.ShapeDtypeStruct((nb, PEER_HEADS, PEER_KEYS, tn), BF16)
    shp32 = jax.ShapeDtypeStruct((nb, PEER_HEADS, PEER_KEYS, tn), F32)
    return pl.pallas_call(
        _route_body,
        grid=(nb,),
        in_specs=[pl.BlockSpec((None, d, tn), lambda i: (i, 0, 0)),
                  pl.BlockSpec((nq, d), lambda i: (0, 0)),
                  pl.BlockSpec((PEER_HEADS, 2, PEER_KEYS, PEER_DHALF), lambda i: (0, 0, 0, 0))],
        out_specs=[spec, spec, spec, spec],
        out_shape=[shp32, shp32, shp, shp],
        scratch_shapes=[pltpu.VMEM((nq, tn), BF16)],
        compiler_params=_cparams("parallel"),
    )(xt, wqt, keys)


def _gelu(x):
    return 0.5 * x * (1.0 + lax.erf(x * (0.5 ** 0.5)))


def _peer_body(xt_ref, u_ref, vt_ref, cn_ref, e1_ref, rk_ref, e2_ref, h2_ref, o_ref,
               hid_s, coef_s, acc_s, rk_s, e2_s):
    e = pl.program_id(1)
    eb, tn = hid_s.shape
    n_i = eb // PEER_KEYS

    @pl.when(e == 0)
    def _init():
        acc_s[...] = jnp.zeros_like(acc_s)
        rk_s[...] = rk_ref[...]
        e2_s[...] = e2_ref[...]

    hid_s[...] = _dot(u_ref[...], xt_ref[...])

    i0 = pl.multiple_of(e * n_i, n_i)
    for cb in range(tn // LANES):
        cs = slice(cb * LANES, (cb + 1) * LANES)
        for j in range(n_i):
            g = jnp.zeros((PEER_KEYS // BF16_ROWS, BF16_ROWS, LANES), BF16)
            for hh in range(PEER_HEADS):
                cnt = jnp.broadcast_to(cn_ref[hh, pl.ds(i0, n_i), cs][j:j + 1, :], (BF16_ROWS, LANES)).astype(BF16)
                ra = jnp.broadcast_to(e1_ref[hh, pl.ds(i0, n_i), cs][j:j + 1, :], (BF16_ROWS, LANES)).astype(BF16)
                e2 = e2_s[hh, :, :, cs]
                g = g + jnp.where(rk_s[hh, :, :, cs] < cnt[None], e2, jnp.zeros_like(e2)) * ra[None]
            rows = slice(j * PEER_KEYS, (j + 1) * PEER_KEYS)
            coef_s[rows, cs] = g.reshape(PEER_KEYS, LANES) * _gelu(hid_s[rows, cs]).astype(BF16)

    acc_s[...] += _dot(vt_ref[...], coef_s[...])

    @pl.when(e == pl.num_programs(1) - 1)
    def _fin():
        o_ref[...] = h2_ref[...] + acc_s[...].T


def _peer(xt, u, vt, cn, e1, rk, e2, h2):
    nb, d, tn = xt.shape
    n_blk, _, eb = vt.shape
    t = nb * tn
    hk = pl.BlockSpec((None, PEER_HEADS, PEER_KEYS, tn), lambda i, e: (i, 0, 0, 0))
    tiles = (PEER_HEADS, PEER_KEYS // BF16_ROWS, BF16_ROWS, tn)
    hk16 = pl.BlockSpec((None,) + tiles, lambda i, e: (i, 0, 0, 0, 0))
    rk = rk.reshape((nb,) + tiles)
    e2 = e2.reshape((nb,) + tiles)
    return pl.pallas_call(
        _peer_body,
        grid=(nb, n_blk),
        in_specs=[pl.BlockSpec((None, d, tn), lambda i, e: (i, 0, 0)),
                  pl.BlockSpec((eb, d), lambda i, e: (e, 0)),
                  pl.BlockSpec((None, d, eb), lambda i, e: (e, 0, 0)),
                  hk, hk, hk16, hk16,
                  pl.BlockSpec((tn, d), lambda i, e: (i, 0))],
        out_specs=pl.BlockSpec((tn, d), lambda i, e: (i, 0)),
        out_shape=jax.ShapeDtypeStruct((t, d), F32),
        scratch_shapes=[pltpu.VMEM((eb, tn), F32),
                        pltpu.VMEM((eb, tn), BF16),
                        pltpu.VMEM((d, tn), F32),
                        pltpu.VMEM(tiles, BF16),
                        pltpu.VMEM(tiles, BF16)],
        compiler_params=_cparams("parallel", "arbitrary"),
    )(xt, u, vt, cn, e1, rk, e2, h2)


def _pad_lanes(v, offset):
    return jnp.zeros((1, LANES), F32).at[0, offset:offset + v.shape[0]].set(v.astype(F32))


def _layer(h, l, near_bias, far_bias, attn_norm, w_in, gdn_conv, gdn_a_log, gdn_dt_bias, gdn_out_norm,
           sc_conv, diff_q_norm, diff_k_norm, diff_lambda, diff_subln, w_out, ffn_norm,
           peer_wq, peer_keys, peer_u, peer_v, batch):
    t, d = h.shape
    s = t // batch
    lambda_init = 0.8 - 0.6 * math.exp(-0.3 * l)
    n_main = W_QKVZ + W_SCD
    wi = w_in[l]
    w_r = jnp.concatenate([wi[:, 0:W_QKVZ], wi[:, W_QKVZ + 2 * GDN_HEADS:], wi[:, W_QKVZ:W_QKVZ + 2 * GDN_HEADS],
                           jnp.zeros((d, W_BA - 2 * GDN_HEADS), wi.dtype)], axis=1).astype(BF16)
    assert w_r.shape[1] == n_main + W_BA
    qkvz, scd, ba = _inproj(h, attn_norm[l][None, :], w_r)

    prm = jnp.concatenate([_pad_lanes(gdn_a_log[l], GDN_HEADS), _pad_lanes(gdn_dt_bias[l], GDN_HEADS),
                           jnp.zeros((6, LANES), F32)], axis=0)
    y_gdn = _gdn(qkvz.reshape(batch, s, W_QKVZ), ba.reshape(batch, s, W_BA), gdn_conv[l].astype(F32), prm,
                 gdn_out_norm[l][None, :].astype(F32))

    y_att = _attn(scd.reshape(batch, s, W_SCD), sc_conv[l].astype(F32),
                  jnp.tile(diff_q_norm[l], 256 // DIFF_DQK)[None, :].astype(F32),
                  jnp.tile(diff_k_norm[l], 256 // DIFF_DQK)[None, :].astype(F32),
                  diff_lambda[l].astype(F32),
                  jnp.tile(diff_subln[l], LANES // DIFF_DV)[None, :].astype(F32),
                  near_bias, far_bias, lambda_init)

    h2, xt = _outproj(y_gdn.reshape(t, -1), y_att.reshape(t, -1), w_out[l].astype(BF16), h, ffn_norm[l][None, :])

    cn, e1, rk, e2 = _route(xt, peer_wq[l].T.astype(BF16), peer_keys[l].astype(BF16))
    n_exp = peer_v.shape[1]
    vt = peer_v[l].reshape(n_exp // PEER_EXPERT_BLK, PEER_EXPERT_BLK, d).transpose(0, 2, 1).astype(BF16)
    return _peer(xt, peer_u[l].astype(BF16), vt, cn, e1, rk, e2, h2)


def kernel(x, rel_bias, attn_norm, w_in, gdn_conv, gdn_a_log, gdn_dt_bias, gdn_out_norm, sc_conv,
           diff_q_norm, diff_k_norm, diff_lambda, diff_subln, w_out, ffn_norm, peer_wq, peer_keys,
           peer_u, peer_v):
    batch, s, d = x.shape
    near_bias, far_bias = _bias_tables(rel_bias)
    h = x.reshape(batch * s, d)
    for l in range(w_in.shape[0]):
        h = _layer(h, l, near_bias, far_bias, attn_norm, w_in, gdn_conv, gdn_a_log, gdn_dt_bias,
                   gdn_out_norm, sc_conv, diff_q_norm, diff_k_norm, diff_lambda, diff_subln, w_out,
                   ffn_norm, peer_wq, peer_keys, peer_u, peer_v, batch)
    return h.reshape(batch, s, d)
```

```python
import functools
import math

import jax
import jax.numpy as jnp
import numpy as np
from jax import lax
from jax.experimental import pallas as pl
from jax.experimental.pallas import tpu as pltpu

F32 = jnp.float32
BF16 = jnp.bfloat16
EPS = 1e-6
NEG = -1e30

D_MODEL = 1024
GDN_HEADS = 4
GDN_D = 128
GDN_CONV = 4
GDN_CHUNK = 64
SC_WIDTH = 256
SC_CONV = 3
DIFF_HEADS = 4
DIFF_DV = 64
DIFF_DQK = 32
ATT_BLK = 256
REL_BUCKETS = 32
REL_MAX_DIST = 128
PEER_HEADS = 8
PEER_KEYS = 128
PEER_TOPK = 16
PEER_DHALF = 128
LANES = 128
BF16_ROWS = 16
TOKEN_BLK = 512
PEER_EXPERT_BLK = 2048
VMEM_LIMIT = 56 * 1024 * 1024

W_QKVZ = 2048
W_SCD = 1536
W_BA = LANES


def _cparams(*sem):
    return pltpu.CompilerParams(dimension_semantics=sem, vmem_limit_bytes=VMEM_LIMIT)


def _nt_dot(a, b):
    return lax.dot_general(a, b, (((1,), (1,)), ((), ())), preferred_element_type=F32)


def _tn_dot(a, b):
    return lax.dot_general(a, b, (((0,), (0,)), ((), ())), preferred_element_type=F32)


def _dot(a, b):
    return jnp.dot(a, b, preferred_element_type=F32)


def _split3(x):
    hi = x.astype(BF16)
    r = x - hi.astype(F32)
    mid = r.astype(BF16)
    lo = (r - mid.astype(F32)).astype(BF16)
    return hi, mid, lo


def _inproj_body(h_ref, g_ref, w_ref, qkvz_ref, scd_ref, ba_ref):
    x = h_ref[...]
    n = x * lax.rsqrt(jnp.mean(x * x, axis=-1, keepdims=True) + EPS) * g_ref[...]
    nb = n.astype(BF16)
    qkvz_ref[...] = _dot(nb, w_ref[:, 0:W_QKVZ]).astype(BF16)
    scd_ref[...] = _dot(nb, w_ref[:, W_QKVZ:W_QKVZ + W_SCD]).astype(BF16)
    ba_ref[...] = _dot(nb, w_ref[:, W_QKVZ + W_SCD:])


def _inproj(h, gain, w):
    t, d = h.shape
    tm = min(512, t)
    nw = w.shape[1]
    return pl.pallas_call(
        _inproj_body,
        grid=(t // tm,),
        in_specs=[pl.BlockSpec((tm, d), lambda i: (i, 0)),
                  pl.BlockSpec((1, d), lambda i: (0, 0)),
                  pl.BlockSpec((d, nw), lambda i: (0, 0))],
        out_specs=[pl.BlockSpec((tm, W_QKVZ), lambda i: (i, 0)),
                   pl.BlockSpec((tm, W_SCD), lambda i: (i, 0)),
                   pl.BlockSpec((tm, W_BA), lambda i: (i, 0))],
        out_shape=[jax.ShapeDtypeStruct((t, W_QKVZ), BF16),
                   jax.ShapeDtypeStruct((t, W_SCD), BF16),
                   jax.ShapeDtypeStruct((t, W_BA), F32)],
        compiler_params=_cparams("parallel"),
    )(h, gain, w)


def _gdn_body(qkvz_ref, ba_ref, conv_ref, prm_ref, gain_ref, y_ref,
              q_s, k_s, kb_s, vb_s, gb_s, o_s, st_s, u_b, w_b, a_b, qg_b, kd_b):
    s = qkvz_ref.shape[0]
    c_sz, nh = GDN_CHUNK, GDN_HEADS
    nc, rr = s // c_sz, GDN_HEADS * GDN_CHUNK
    row = lax.broadcasted_iota(jnp.int32, (s, LANES), 0)

    ba = ba_ref[...]
    beta = jax.nn.sigmoid(ba)
    xg = ba + prm_ref[1:2, :]
    softplus = jnp.maximum(xg, 0.0) + jnp.log(1.0 + jnp.exp(-jnp.abs(xg)))
    g = -jnp.exp(prm_ref[0:1, :]) * softplus
    pos = row % c_sz
    for sh in (1, 2, 4, 8, 16, 32):
        g = g + jnp.where(pos >= sh, pltpu.roll(g, sh, 0), 0.0)

    def chunked(x):
        return x.reshape(nc, c_sz, LANES)

    for hh in range(nh):
        gb_s[:, hh] = chunked(jnp.broadcast_to(g[:, nh + hh:nh + hh + 1], (s, LANES)))

    scale = GDN_D ** -0.5
    for cb in range(3 * nh):
        x = qkvz_ref[:, cb * LANES:(cb + 1) * LANES].astype(F32)
        w = conv_ref[:, cb * LANES:(cb + 1) * LANES]
        acc = x * w[GDN_CONV - 1:GDN_CONV, :]
        for j in range(GDN_CONV - 1):
            sh = GDN_CONV - 1 - j
            acc = acc + jnp.where(row >= sh, pltpu.roll(x, sh, 0), 0.0) * w[j:j + 1, :]
        y = acc * jax.nn.sigmoid(acc)
        kind, hh = divmod(cb, nh)
        if kind < 2:
            y = y * lax.rsqrt(jnp.sum(y * y, axis=-1, keepdims=True) + EPS)
        if kind == 0:
            q_s[:, hh] = chunked(y * scale)
        elif kind == 1:
            k_s[:, hh] = chunked(y)
            kb_s[:, hh] = chunked(y * beta[:, hh:hh + 1])
        else:
            vb_s[:, hh] = chunked((y * beta[:, hh:hh + 1]).astype(BF16))

    st_s[...] = jnp.zeros_like(st_s)
    ii = lax.broadcasted_iota(jnp.int32, (rr, rr), 0)
    jj = lax.broadcasted_iota(jnp.int32, (rr, rr), 1)
    same_head = (ii // c_sz) == (jj // c_sz)
    tril = jnp.logical_and(same_head, ii >= jj)
    strict = jnp.logical_and(same_head, ii > jj)
    eye = (ii == jj).astype(F32)
    lane = lax.broadcasted_iota(jnp.int32, (rr, LANES), 1)
    pick3 = (lane < 3).astype(BF16)
    row_head = lax.broadcasted_iota(jnp.int32, (rr, LANES), 0) // c_sz

    def stacked(ref, c):
        return ref[c].reshape(rr, LANES)

    def phase_a(c, slot):
        gc = stacked(gb_s, c)
        eg = jnp.exp(gc)
        qc, kc, kb = stacked(q_s, c), stacked(k_s, c), stacked(kb_s, c)
        glast = jnp.broadcast_to(gb_s[c][:, c_sz - 1:c_sz, :], (nh, c_sz, LANES)).reshape(rr, LANES)
        hi, mid, lo = _split3(gc)
        x3 = jnp.where(lane == 0, hi, jnp.where(lane == 1, mid, jnp.where(lane == 2, lo, jnp.zeros_like(lo))))
        grow = _nt_dot(pick3, x3)
        decay = jnp.where(tril, jnp.exp(jnp.minimum(jnp.concatenate([gc, gc], axis=1) - grow, 0.0)), 0.0)
        kq = _nt_dot(jnp.concatenate([kb.astype(BF16), qc.astype(BF16)], axis=0), kc.astype(BF16))
        lower = jnp.where(strict, kq[0:rr] * decay, 0.0)
        a_in = jnp.where(tril, kq[rr:2 * rr] * decay, 0.0)
        pw = -lower
        tm = eye + pw
        pw = _dot(pw.astype(BF16), pw.astype(BF16))
        for _ in range(4):
            pwb = pw.astype(BF16)
            both = _dot(jnp.concatenate([tm.astype(BF16), pwb], axis=0), pwb)
            tm = tm + both[0:rr]
            pw = both[rr:2 * rr]
        tm = tm + _dot(tm.astype(BF16), pw.astype(BF16))
        uw = _dot(tm.astype(BF16), jnp.concatenate([stacked(vb_s, c), (kb * eg).astype(BF16)], axis=1))
        u_b[slot] = uw[:, 0:LANES]
        w_b[slot] = uw[:, LANES:2 * LANES].astype(BF16)
        a_b[slot] = a_in.astype(BF16)
        qg_b[slot] = (qc * eg).astype(BF16)
        kd_b[slot] = (kc * jnp.exp(glast - gc)).astype(BF16)

    def phase_b(c, slot):
        stb = st_s[...].astype(BF16)
        wq = _dot(jnp.concatenate([w_b[slot], qg_b[slot]], axis=0), stb)
        u = u_b[slot]

        def head_blocks(r0):
            return jnp.concatenate([wq[r0 + hh * c_sz:r0 + (hh + 1) * c_sz, hh * LANES:(hh + 1) * LANES]
                                    for hh in range(nh)], axis=0)

        vnb = (u - head_blocks(0)).astype(BF16)
        o_s[c] = (head_blocks(rr) + _dot(a_b[slot], vnb)).reshape(nh, c_sz, LANES)
        vbd = jnp.concatenate([jnp.where(row_head == hh, vnb, jnp.zeros_like(vnb)) for hh in range(nh)], axis=1)
        egl = jnp.concatenate([jnp.exp(gb_s[c][hh, c_sz - 1:c_sz, :]) for hh in range(nh)], axis=1)
        st_s[...] = st_s[...] * egl + _tn_dot(kd_b[slot], vbd)

    phase_a(0, 0)

    def pair(kk, carry):
        c = 2 * kk
        phase_b(c, 0)
        phase_a(c + 1, 1)
        phase_b(c + 1, 1)
        phase_a(jnp.minimum(c + 2, nc - 1), 0)
        return carry

    lax.fori_loop(0, nc // 2, pair, 0)

    gain = gain_ref[...]
    for hh in range(nh):
        o = o_s[:, hh].reshape(s, LANES)
        z = qkvz_ref[:, (3 * nh + hh) * LANES:(3 * nh + hh + 1) * LANES].astype(F32)
        on = o * lax.rsqrt(jnp.mean(o * o, axis=-1, keepdims=True) + EPS) * gain
        y_ref[:, hh * LANES:(hh + 1) * LANES] = (on * (z * jax.nn.sigmoid(z))).astype(BF16)


def _gdn(qkvz, ba, conv_w, prm, gain):
    b, s, _ = qkvz.shape
    hd = GDN_HEADS * GDN_D
    nc, rr = s // GDN_CHUNK, GDN_HEADS * GDN_CHUNK
    per_chunk = (nc, GDN_HEADS, GDN_CHUNK, GDN_D)
    return pl.pallas_call(
        _gdn_body,
        grid=(b,),
        in_specs=[pl.BlockSpec((None, s, W_QKVZ), lambda i: (i, 0, 0)),
                  pl.BlockSpec((None, s, W_BA), lambda i: (i, 0, 0)),
                  pl.BlockSpec((GDN_CONV, 3 * hd), lambda i: (0, 0)),
                  pl.BlockSpec((8, LANES), lambda i: (0, 0)),
                  pl.BlockSpec((1, GDN_D), lambda i: (0, 0))],
        out_specs=pl.BlockSpec((None, s, hd), lambda i: (i, 0, 0)),
        out_shape=jax.ShapeDtypeStruct((b, s, hd), BF16),
        scratch_shapes=[pltpu.VMEM(per_chunk, F32),
                        pltpu.VMEM(per_chunk, F32),
                        pltpu.VMEM(per_chunk, F32),
                        pltpu.VMEM(per_chunk, BF16),
                        pltpu.VMEM(per_chunk, F32),
                        pltpu.VMEM(per_chunk, F32),
                        pltpu.VMEM((GDN_D, hd), F32),
                        pltpu.VMEM((2, rr, GDN_D), F32),
                        pltpu.VMEM((2, rr, GDN_D), BF16),
                        pltpu.VMEM((2, rr, rr), BF16),
                        pltpu.VMEM((2, rr, GDN_D), BF16),
                        pltpu.VMEM((2, rr, GDN_D), BF16)],
        compiler_params=_cparams("parallel"),
    )(qkvz, ba, conv_w, prm, gain)


def _attn_body(far_ref, scd_ref, scw_ref, qg_ref, kg_ref, lam_ref, sub_ref, nbt_ref, y_ref,
               qt_s, kn_s, vt_s, ysc_s, qc_s, m_s, l_s, acc_s, *, lambda_init):
    i = pl.program_id(1)
    s = scd_ref.shape[0]
    blk = ATT_BLK
    o_q, o_k, o_v = 3 * SC_WIDTH, 3 * SC_WIDTH + 256, 3 * SC_WIDTH + 512

    @pl.when(i == 0)
    def _prep():
        row = lax.broadcasted_iota(jnp.int32, (s, SC_WIDTH), 0)
        gate_b = scd_ref[:, 0:SC_WIDTH].astype(F32)
        x = scd_ref[:, SC_WIDTH:2 * SC_WIDTH].astype(F32) * scd_ref[:, 2 * SC_WIDTH:3 * SC_WIDTH].astype(F32)
        w = scw_ref[...]
        acc = x * w[SC_CONV - 1:SC_CONV, :]
        for j in range(SC_CONV - 1):
            sh = SC_CONV - 1 - j
            acc = acc + jnp.where(row >= sh, pltpu.roll(x, sh, 0), 0.0) * w[j:j + 1, :]
        ysc_s[...] = (gate_b * acc).astype(BF16)
        gi = lax.broadcasted_iota(jnp.int32, (256, 256), 0) // DIFF_DQK
        gj = lax.broadcasted_iota(jnp.int32, (256, 256), 1) // DIFF_DQK
        bd = (gi == gj).astype(BF16)

        def normed(off, g_ref, sc):
            xx = scd_ref[:, off:off + 256].astype(F32)
            hi, mid, lo = _split3(xx * xx)
            ss = _dot(hi, bd) + _dot(mid, bd) + _dot(lo, bd)
            return xx * lax.rsqrt(ss * (1.0 / DIFF_DQK) + EPS) * g_ref[...] * sc

        qt_s[...] = normed(o_q, qg_ref, DIFF_DQK ** -0.5).T.astype(BF16)
        kn_s[...] = normed(o_k, kg_ref, 1.0).astype(BF16)
        vt_s[...] = scd_ref[:, o_v:o_v + 256].astype(F32).T.astype(BF16)

    lp = lam_ref[...]
    lam = (jnp.exp(jnp.sum(lp[0:1, :] * lp[1:2, :], axis=-1, keepdims=True))
           - jnp.exp(jnp.sum(lp[2:3, :] * lp[3:4, :], axis=-1, keepdims=True)) + lambda_init)

    r0 = pl.multiple_of(i * blk, blk)
    rowi = lax.broadcasted_iota(jnp.int32, (LANES, blk), 0)
    grp = rowi // DIFF_DQK

    m_s[...] = jnp.full_like(m_s, NEG)
    l_s[...] = jnp.zeros_like(l_s)
    acc_s[...] = jnp.zeros_like(acc_s)
    for hp in range(2):
        qp = qt_s[hp * LANES:(hp + 1) * LANES, pl.ds(r0, blk)]
        for sidx in range(4):
            qc_s[hp, :, sidx * blk:(sidx + 1) * blk] = jnp.where(grp == sidx, qp, jnp.zeros_like(qp))

    def block(c0, bias_of):
        for hp in range(2):
            kp = kn_s[pl.ds(c0, blk), hp * LANES:(hp + 1) * LANES]
            vt = vt_s[hp * LANES:(hp + 1) * LANES, pl.ds(c0, blk)]
            logits = _dot(kp, qc_s[hp])
            b0, b1 = bias_of(2 * hp), bias_of(2 * hp + 1)
            if b0.ndim == 2:
                bias = jnp.concatenate([b0, b0, b1, b1], axis=1)
                logits = logits + bias
            else:
                two = 2 * blk
                logits = jnp.concatenate([logits[:, 0:two] + b0, logits[:, two:2 * two] + b1], axis=1)
            m_old = m_s[hp:hp + 1, :]
            m_new = jnp.maximum(m_old, jnp.max(logits, axis=0, keepdims=True))
            alpha = jnp.exp(m_old - m_new)
            p = jnp.exp(logits - m_new)
            l_s[hp:hp + 1, :] = alpha * l_s[hp:hp + 1, :] + jnp.sum(p, axis=0, keepdims=True)
            acc_s[hp] = alpha * acc_s[hp] + _dot(vt, p.astype(BF16))
            m_s[hp:hp + 1, :] = m_new

    def far_body(kb, carry):
        block(pl.multiple_of(kb * blk, blk), lambda head: far_ref[head])
        return carry

    lax.fori_loop(0, jnp.maximum(i - 1, 0), far_body, 0)

    @pl.when(i >= 1)
    def _prev():
        block(pl.multiple_of((i - 1) * blk, blk), lambda head: nbt_ref[head, 0:blk, :])

    block(r0, lambda head: nbt_ref[head, blk:2 * blk, :])

    low = rowi < DIFF_DV
    for hp in range(2):
        outs = []
        for hh in range(2):
            c0, c1 = (2 * hh) * blk, (2 * hh + 1) * blk
            outs.append(acc_s[hp, :, c0:c0 + blk] / l_s[hp:hp + 1, c0:c0 + blk]
                        - lam * (acc_s[hp, :, c1:c1 + blk] / l_s[hp:hp + 1, c1:c1 + blk]))
        o = jnp.where(low, outs[0], outs[1])
        sq = o * o
        ss0 = jnp.sum(jnp.where(low, sq, 0.0), axis=0, keepdims=True)
        ss1 = jnp.sum(jnp.where(low, 0.0, sq), axis=0, keepdims=True)
        ms = jnp.where(low, ss0, ss1) * (1.0 / DIFF_DV)
        y = (o * lax.rsqrt(ms + EPS)).T * sub_ref[...] * (1.0 - lambda_init)
        y_ref[:, SC_WIDTH + hp * LANES:SC_WIDTH + (hp + 1) * LANES] = y.astype(BF16)
    y_ref[:, 0:SC_WIDTH] = ysc_s[pl.ds(r0, blk), :]


def _attn(scd, sc_w, q_gain, k_gain, lam_p, sub_gain, near_bias_t, far_bias, lambda_init):
    b, s, _ = scd.shape
    blk = ATT_BLK
    const2 = lambda bi, i: (0, 0)
    const3 = lambda bi, i: (0, 0, 0)
    return pl.pallas_call(
        functools.partial(_attn_body, lambda_init=lambda_init),
        grid=(b, s // blk),
        in_specs=[pl.BlockSpec(memory_space=pltpu.SMEM),
                  pl.BlockSpec((None, s, W_SCD), lambda bi, i: (bi, 0, 0)),
                  pl.BlockSpec((SC_CONV, SC_WIDTH), const2),
                  pl.BlockSpec((1, 256), const2),
                  pl.BlockSpec((1, 256), const2),
                  pl.BlockSpec((4, DIFF_DQK), const2),
                  pl.BlockSpec((1, LANES), const2),
                  pl.BlockSpec((DIFF_HEADS, 2 * blk, blk), const3)],
        out_specs=pl.BlockSpec((None, blk, 512), lambda bi, i: (bi, i, 0)),
        out_shape=jax.ShapeDtypeStruct((b, s, 512), BF16),
        scratch_shapes=[pltpu.VMEM((256, s), BF16),
                        pltpu.VMEM((s, 256), BF16),
                        pltpu.VMEM((256, s), BF16),
                        pltpu.VMEM((s, SC_WIDTH), BF16),
                        pltpu.VMEM((2, LANES, 4 * blk), BF16),
                        pltpu.VMEM((2, 4 * blk), F32),
                        pltpu.VMEM((2, 4 * blk), F32),
                        pltpu.VMEM((2, LANES, 4 * blk), F32)],
        compiler_params=_cparams("parallel", "arbitrary"),
    )(far_bias, scd, sc_w, q_gain, k_gain, lam_p, sub_gain, near_bias_t)


def _rel_bucket(rel):
    max_exact = REL_BUCKETS // 2
    n = jnp.maximum(rel, 0)
    large = max_exact + (jnp.log(jnp.maximum(n, max_exact).astype(F32) / max_exact)
                         / math.log(REL_MAX_DIST / max_exact) * (REL_BUCKETS - max_exact)).astype(jnp.int32)
    large = jnp.minimum(large, REL_BUCKETS - 1)
    return jnp.where(n < max_exact, n, large)


def _bias_tables(rel_bias):
    blk = ATT_BLK
    rel = jnp.arange(blk)[:, None] + blk - jnp.arange(2 * blk)[None, :]
    onehot = (_rel_bucket(rel)[None] == jnp.arange(REL_BUCKETS)[:, None, None]).astype(F32)
    near = jnp.einsum("brc,bh->hrc", onehot, rel_bias.astype(F32), precision=lax.Precision.HIGHEST)
    near = jnp.where(rel[None] >= 0, near, NEG)
    return jnp.swapaxes(near, 1, 2), rel_bias[REL_BUCKETS - 1].astype(F32)


def _outproj_body(yg_ref, ya_ref, w_ref, h_ref, g_ref, h2_ref, xt_ref):
    hd = yg_ref.shape[1]
    h2 = h_ref[...] + _dot(yg_ref[...], w_ref[0:hd, :]) + _dot(ya_ref[...], w_ref[hd:, :])
    h2_ref[...] = h2
    n = h2 * lax.rsqrt(jnp.mean(h2 * h2, axis=-1, keepdims=True) + EPS) * g_ref[...]
    xt_ref[...] = n.T.astype(BF16)


def _outproj(yg, ya, w, h, gain):
    t, d = h.shape
    tm = min(TOKEN_BLK, t)
    return pl.pallas_call(
        _outproj_body,
        grid=(t // tm,),
        in_specs=[pl.BlockSpec((tm, yg.shape[1]), lambda i: (i, 0)),
                  pl.BlockSpec((tm, ya.shape[1]), lambda i: (i, 0)),
                  pl.BlockSpec((d, d), lambda i: (0, 0)),
                  pl.BlockSpec((tm, d), lambda i: (i, 0)),
                  pl.BlockSpec((1, d), lambda i: (0, 0))],
        out_specs=[pl.BlockSpec((tm, d), lambda i: (i, 0)),
                   pl.BlockSpec((None, d, tm), lambda i: (i, 0, 0))],
        out_shape=[jax.ShapeDtypeStruct((t, d), F32),
                   jax.ShapeDtypeStruct((t // tm, d, tm), BF16)],
        compiler_params=_cparams("parallel"),
    )(yg, ya, w, h, gain)


def _cmpx(lst, i, j):
    a, b = lst[i], lst[j]
    lst[i] = jnp.maximum(a, b)
    lst[j] = jnp.minimum(a, b)


def _bitonic_clean(lst, lo, n):
    d = n // 2
    while d >= 1:
        for k in range(n):
            if (k // d) % 2 == 0:
                _cmpx(lst, lo + k, lo + k + d)
        d //= 2


def _sort_desc(lst, lo, n):
    if n == 1:
        return
    h = n // 2
    _sort_desc(lst, lo, h)
    _sort_desc(lst, lo + h, h)
    for k in range(h):
        _cmpx(lst, lo + k, lo + n - 1 - k)
    _bitonic_clean(lst, lo, h)
    _bitonic_clean(lst, lo + h, h)


def _merge_sublanes(lst):
    n = len(lst)
    for d in (4, 2, 1):
        lst = [jnp.maximum(lst[k], pltpu.roll(lst[n - 1 - k], d, 0)) for k in range(n)]
        _bitonic_clean(lst, 0, n)
    return lst


def _top_sorted(s):
    lst = [s[k * 8:(k + 1) * 8, :] for k in range(s.shape[0] // 8)]
    _sort_desc(lst, 0, len(lst))
    return _merge_sublanes(lst)


def _count_prefix(lst, pred):
    n = len(lst)
    steps = []
    step = n // 2
    while step >= 1:
        steps.append(step)
        step //= 2

    def pick(bits, weights, index):
        if not bits:
            return lst[index]
        return jnp.where(bits[0], pick(bits[1:], weights[1:], index + weights[0]),
                         pick(bits[1:], weights[1:], index))

    bits = []
    for level, step in enumerate(steps):
        bits.append(pred(pick(bits, steps[:level], step - 1)))
    count = jnp.zeros_like(lst[0])
    for bit, step in zip(bits, steps):
        count = count + jnp.where(bit, float(step), 0.0)
    return jnp.where(pred(lst[n - 1]), float(n), count)


def _route_body(xt_ref, wqt_ref, keys_ref, cn_ref, e1_ref, rk_ref, e2_ref, q_s):
    k = PEER_TOPK
    tn = xt_ref.shape[1]
    q_s[...] = _dot(wqt_ref[...], xt_ref[...]).astype(BF16)
    sub = lax.broadcasted_iota(jnp.int32, (8, tn), 0)
    for hh in range(PEER_HEADS):
        sc = []
        for p in range(2):
            r = (hh * 2 + p) * PEER_DHALF
            sc.append(_dot(keys_ref[hh, p], q_s[r:r + PEER_DHALF, :]))
        a = _top_sorted(sc[0])
        b = _top_sorted(sc[1])
        apack, bpack = a[0], b[0]
        for r in range(1, 8):
            apack = jnp.where(sub == r, a[r], apack)
            bpack = jnp.where(sub == r, b[r], bpack)
        cand = [apack + b[i] for i in range(k)]
        extra = [a[8 + i] + bpack for i in range(k - 8)]
        for i in range(8, k):
            cand[i] = jnp.maximum(cand[i], extra[k - 1 - i])
        _bitonic_clean(cand, 0, k)
        best = _merge_sublanes(cand)
        z = jnp.zeros_like(best[0])
        for i in range(k):
            z = z + jnp.exp(best[i] - best[0])
        tau = best[k - 1]
        for r in range(PEER_KEYS // 8):
            rows = slice(r * 8, (r + 1) * 8)
            s1r, s2r = sc[0][rows, :], sc[1][rows, :]
            rk_ref[hh, rows, :] = _count_prefix(b, lambda t: t > s2r).astype(BF16)
            cn_ref[hh, rows, :] = _count_prefix(b, lambda t: s1r + t >= tau)
        e1_ref[hh] = jnp.exp(sc[0] - a[0][0:1, :])
        e2_ref[hh] = (jnp.exp(sc[1] - b[0][0:1, :]) / z[0:1, :]).astype(BF16)


def _route(xt, wqt, keys):
    nb, d, tn = xt.shape
    nq = wqt.shape[0]
    spec = pl.BlockSpec((None, PEER_HEADS, PEER_KEYS, tn), lambda i: (i, 0, 0, 0))
    shp = jax.ShapeDtypeStruct((nb, PEER_HEADS, PEER_KEYS, tn), BF16)
    shp32 = jax.ShapeDtypeStruct((nb, PEER_HEADS, PEER_KEYS, tn), F32)
    return pl.pallas_call(
        _route_body,
        grid=(nb,),
        in_specs=[pl.BlockSpec((None, d, tn), lambda i: (i, 0, 0)),
                  pl.BlockSpec((nq, d), lambda i: (0, 0)),
                  pl.BlockSpec((PEER_HEADS, 2, PEER_KEYS, PEER_DHALF), lambda i: (0, 0, 0, 0))],
        out_specs=[spec, spec, spec, spec],
        out_shape=[shp32, shp32, shp, shp],
        scratch_shapes=[pltpu.VMEM((nq, tn), BF16)],
        compiler_params=_cparams("parallel"),
    )(xt, wqt, keys)


def _gelu(x):
    return 0.5 * x * (1.0 + lax.erf(x * (0.5 ** 0.5)))


def _peer_body(xt_ref, u_ref, vt_ref, cn_ref, e1_ref, rk_ref, e2_ref, h2_ref, o_ref,
               hid_s, coef_s, acc_s, rk_s, e2_s):
    e = pl.program_id(1)
    eb, tn = hid_s.shape
    n_i = eb // PEER_KEYS

    @pl.when(e == 0)
    def _init():
        acc_s[...] = jnp.zeros_like(acc_s)
        rk_s[...] = rk_ref[...]
        e2_s[...] = e2_ref[...]

    hid_s[...] = _dot(u_ref[...], xt_ref[...])

    i0 = pl.multiple_of(e * n_i, n_i)
    for cb in range(tn // LANES):
        cs = slice(cb * LANES, (cb + 1) * LANES)
        for j in range(n_i):
            g = jnp.zeros((PEER_KEYS // BF16_ROWS, BF16_ROWS, LANES), BF16)
            for hh in range(PEER_HEADS):
                cnt = jnp.broadcast_to(cn_ref[hh, pl.ds(i0, n_i), cs][j:j + 1, :], (BF16_ROWS, LANES)).astype(BF16)
                ra = jnp.broadcast_to(e1_ref[hh, pl.ds(i0, n_i), cs][j:j + 1, :], (BF16_ROWS, LANES)).astype(BF16)
                e2 = e2_s[hh, :, :, cs]
                g = g + jnp.where(rk_s[hh, :, :, cs] < cnt[None], e2, jnp.zeros_like(e2)) * ra[None]
            rows = slice(j * PEER_KEYS, (j + 1) * PEER_KEYS)
            coef_s[rows, cs] = g.reshape(PEER_KEYS, LANES) * _gelu(hid_s[rows, cs]).astype(BF16)

    acc_s[...] += _dot(vt_ref[...], coef_s[...])

    @pl.when(e == pl.num_programs(1) - 1)
    def _fin():
        o_ref[...] = h2_ref[...] + acc_s[...].T


def _peer(xt, u, vt, cn, e1, rk, e2, h2):
    nb, d, tn = xt.shape
    n_blk, _, eb = vt.shape
    t = nb * tn
    hk = pl.BlockSpec((None, PEER_HEADS, PEER_KEYS, tn), lambda i, e: (i, 0, 0, 0))
    tiles = (PEER_HEADS, PEER_KEYS // BF16_ROWS, BF16_ROWS, tn)
    hk16 = pl.BlockSpec((None,) + tiles, lambda i, e: (i, 0, 0, 0, 0))
    rk = rk.reshape((nb,) + tiles)
    e2 = e2.reshape((nb,) + tiles)
    return pl.pallas_call(
        _peer_body,
        grid=(nb, n_blk),
        in_specs=[pl.BlockSpec((None, d, tn), lambda i, e: (i, 0, 0)),
                  pl.BlockSpec((eb, d), lambda i, e: (e, 0)),
                  pl.BlockSpec((None, d, eb), lambda i, e: (e, 0, 0)),
                  hk, hk, hk16, hk16,
                  pl.BlockSpec((tn, d), lambda i, e: (i, 0))],
        out_specs=pl.BlockSpec((tn, d), lambda i, e: (i, 0)),
        out_shape=jax.ShapeDtypeStruct((t, d), F32),
        scratch_shapes=[pltpu.VMEM((eb, tn), F32),
                        pltpu.VMEM((eb, tn), BF16),
                        pltpu.VMEM((d, tn), F32),
                        pltpu.VMEM(tiles, BF16),
                        pltpu.VMEM(tiles, BF16)],
        compiler_params=_cparams("parallel", "arbitrary"),
    )(xt, u, vt, cn, e1, rk, e2, h2)


def _pad_lanes(v, offset):
    return jnp.zeros((1, LANES), F32).at[0, offset:offset + v.shape[0]].set(v.astype(F32))


def _layer(h, l, near_bias, far_bias, attn_norm, w_in, gdn_conv, gdn_a_log, gdn_dt_bias, gdn_out_norm,
           sc_conv, diff_q_norm, diff_k_norm, diff_lambda, diff_subln, w_out, ffn_norm,
           peer_wq, peer_keys, peer_u, peer_v, batch):
    t, d = h.shape
    s = t // batch
    lambda_init = 0.8 - 0.6 * math.exp(-0.3 * l)
    n_main = W_QKVZ + W_SCD
    wi = w_in[l]
    w_r = jnp.concatenate([wi[:, 0:W_QKVZ], wi[:, W_QKVZ + 2 * GDN_HEADS:], wi[:, W_QKVZ:W_QKVZ + 2 * GDN_HEADS],
                           jnp.zeros((d, W_BA - 2 * GDN_HEADS), wi.dtype)], axis=1).astype(BF16)
    assert w_r.shape[1] == n_main + W_BA
    qkvz, scd, ba = _inproj(h, attn_norm[l][None, :], w_r)

    prm = jnp.concatenate([_pad_lanes(gdn_a_log[l], GDN_HEADS), _pad_lanes(gdn_dt_bias[l], GDN_HEADS),
                           jnp.zeros((6, LANES), F32)], axis=0)
    y_gdn = _gdn(qkvz.reshape(batch, s, W_QKVZ), ba.reshape(batch, s, W_BA), gdn_conv[l].astype(F32), prm,
                 gdn_out_norm[l][None, :].astype(F32))

    y_att = _attn(scd.reshape(batch, s, W_SCD), sc_conv[l].astype(F32),
                  jnp.tile(diff_q_norm[l], 256 // DIFF_DQK)[None, :].astype(F32),
                  jnp.tile(diff_k_norm[l], 256 // DIFF_DQK)[None, :].astype(F32),
                  diff_lambda[l].astype(F32),
                  jnp.tile(diff_subln[l], LANES // DIFF_DV)[None, :].astype(F32),
                  near_bias, far_bias, lambda_init)

    h2, xt = _outproj(y_gdn.reshape(t, -1), y_att.reshape(t, -1), w_out[l].astype(BF16), h, ffn_norm[l][None, :])

    cn, e1, rk, e2 = _route(xt, peer_wq[l].T.astype(BF16), peer_keys[l].astype(BF16))
    n_exp = peer_v.shape[1]
    vt = peer_v[l].reshape(n_exp // PEER_EXPERT_BLK, PEER_EXPERT_BLK, d).transpose(0, 2, 1).astype(BF16)
    return _peer(xt, peer_u[l].astype(BF16), vt, cn, e1, rk, e2, h2)


def kernel(x, rel_bias, attn_norm, w_in, gdn_conv, gdn_a_log, gdn_dt_bias, gdn_out_norm, sc_conv,
           diff_q_norm, diff_k_norm, diff_lambda, diff_subln, w_out, ffn_norm, peer_wq, peer_keys,
           peer_u, peer_v):
    batch, s, d = x.shape
    near_bias, far_bias = _bias_tables(rel_bias)
    h = x.reshape(batch * s, d)
    for l in range(w_in.shape[0]):
        h = _layer(h, l, near_bias, far_bias, attn_norm, w_in, gdn_conv, gdn_a_log, gdn_dt_bias,
                   gdn_out_norm, sc_conv, diff_q_norm, diff_k_norm, diff_lambda, diff_subln, w_out,
                   ffn_norm, peer_wq, peer_keys, peer_u, peer_v, batch)
    return h.reshape(batch, s, d)
```

```python
import functools
import math

import jax
import jax.numpy as jnp
import numpy as np
from jax import lax
from jax.experimental import pallas as pl
from jax.experimental.pallas import tpu as pltpu

F32 = jnp.float32
BF16 = jnp.bfloat16
EPS = 1e-6
NEG = -1e30

D_MODEL = 1024
GDN_HEADS = 4
GDN_D = 128
GDN_CONV = 4
GDN_CHUNK = 64
SC_WIDTH = 256
SC_CONV = 3
DIFF_HEADS = 4
DIFF_DV = 64
DIFF_DQK = 32
ATT_BLK = 256
REL_BUCKETS = 32
REL_MAX_DIST = 128
PEER_HEADS = 8
PEER_KEYS = 128
PEER_TOPK = 16
PEER_DHALF = 128
LANES = 128
BF16_ROWS = 16
TOKEN_BLK = 512
PEER_EXPERT_BLK = 2048
VMEM_LIMIT = 56 * 1024 * 1024

W_QKVZ = 2048
W_SCD = 1536
W_BA = LANES


def _cparams(*sem):
    return pltpu.CompilerParams(dimension_semantics=sem, vmem_limit_bytes=VMEM_LIMIT)


def _nt_dot(a, b):
    return lax.dot_general(a, b, (((1,), (1,)), ((), ())), preferred_element_type=F32)


def _tn_dot(a, b):
    return lax.dot_general(a, b, (((0,), (0,)), ((), ())), preferred_element_type=F32)


def _dot(a, b):
    return jnp.dot(a, b, preferred_element_type=F32)


def _split3(x):
    hi = x.astype(BF16)
    r = x - hi.astype(F32)
    mid = r.astype(BF16)
    lo = (r - mid.astype(F32)).astype(BF16)
    return hi, mid, lo


def _inproj_body(h_ref, g_ref, w_ref, qkvz_ref, scd_ref, ba_ref):
    x = h_ref[...]
    n = x * lax.rsqrt(jnp.mean(x * x, axis=-1, keepdims=True) + EPS) * g_ref[...]
    nb = n.astype(BF16)
    qkvz_ref[...] = _dot(nb, w_ref[:, 0:W_QKVZ]).astype(BF16)
    scd_ref[...] = _dot(nb, w_ref[:, W_QKVZ:W_QKVZ + W_SCD]).astype(BF16)
    ba_ref[...] = _dot(nb, w_ref[:, W_QKVZ + W_SCD:])


def _inproj(h, gain, w):
    t, d = h.shape
    tm = min(512, t)
    nw = w.shape[1]
    return pl.pallas_call(
        _inproj_body,
        grid=(t // tm,),
        in_specs=[pl.BlockSpec((tm, d), lambda i: (i, 0)),
                  pl.BlockSpec((1, d), lambda i: (0, 0)),
                  pl.BlockSpec((d, nw), lambda i: (0, 0))],
        out_specs=[pl.BlockSpec((tm, W_QKVZ), lambda i: (i, 0)),
                   pl.BlockSpec((tm, W_SCD), lambda i: (i, 0)),
                   pl.BlockSpec((tm, W_BA), lambda i: (i, 0))],
        out_shape=[jax.ShapeDtypeStruct((t, W_QKVZ), BF16),
                   jax.ShapeDtypeStruct((t, W_SCD), BF16),
                   jax.ShapeDtypeStruct((t, W_BA), F32)],
        compiler_params=_cparams("parallel"),
    )(h, gain, w)


def _gdn_body(qkvz_ref, ba_ref, conv_ref, prm_ref, gain_ref, y_ref,
              q_s, k_s, kb_s, vb_s, gb_s, o_s, st_s, u_b, w_b, a_b, qg_b, kd_b):
    s = qkvz_ref.shape[0]
    c_sz, nh = GDN_CHUNK, GDN_HEADS
    nc, rr = s // c_sz, GDN_HEADS * GDN_CHUNK
    row = lax.broadcasted_iota(jnp.int32, (s, LANES), 0)

    ba = ba_ref[...]
    beta = jax.nn.sigmoid(ba)
    xg = ba + prm_ref[1:2, :]
    softplus = jnp.maximum(xg, 0.0) + jnp.log(1.0 + jnp.exp(-jnp.abs(xg)))
    g = -jnp.exp(prm_ref[0:1, :]) * softplus
    pos = row % c_sz
    for sh in (1, 2, 4, 8, 16, 32):
        g = g + jnp.where(pos >= sh, pltpu.roll(g, sh, 0), 0.0)

    def chunked(x):
        return x.reshape(nc, c_sz, LANES)

    for hh in range(nh):
        gb_s[:, hh] = chunked(jnp.broadcast_to(g[:, nh + hh:nh + hh + 1], (s, LANES)))

    scale = GDN_D ** -0.5
    for cb in range(3 * nh):
        x = qkvz_ref[:, cb * LANES:(cb + 1) * LANES].astype(F32)
        w = conv_ref[:, cb * LANES:(cb + 1) * LANES]
        acc = x * w[GDN_CONV - 1:GDN_CONV, :]
        for j in range(GDN_CONV - 1):
            sh = GDN_CONV - 1 - j
            acc = acc + jnp.where(row >= sh, pltpu.roll(x, sh, 0), 0.0) * w[j:j + 1, :]
        y = acc * jax.nn.sigmoid(acc)
        kind, hh = divmod(cb, nh)
        if kind < 2:
            y = y * lax.rsqrt(jnp.sum(y * y, axis=-1, keepdims=True) + EPS)
        if kind == 0:
            q_s[:, hh] = chunked(y * scale)
        elif kind == 1:
            k_s[:, hh] = chunked(y)
            kb_s[:, hh] = chunked(y * beta[:, hh:hh + 1])
        else:
            vb_s[:, hh] = chunked((y * beta[:, hh:hh + 1]).astype(BF16))

    st_s[...] = jnp.zeros_like(st_s)
    ii = lax.broadcasted_iota(jnp.int32, (rr, rr), 0)
    jj = lax.broadcasted_iota(jnp.int32, (rr, rr), 1)
    same_head = (ii // c_sz) == (jj // c_sz)
    tril = jnp.logical_and(same_head, ii >= jj)
    strict = jnp.logical_and(same_head, ii > jj)
    eye = (ii == jj).astype(F32)
    lane = lax.broadcasted_iota(jnp.int32, (rr, LANES), 1)
    pick3 = (lane < 3).astype(BF16)
    row_head = lax.broadcasted_iota(jnp.int32, (rr, LANES), 0) // c_sz

    def stacked(ref, c):
        return ref[c].reshape(rr, LANES)

    def phase_a(c, slot):
        gc = stacked(gb_s, c)
        eg = jnp.exp(gc)
        qc, kc, kb = stacked(q_s, c), stacked(k_s, c), stacked(kb_s, c)
        glast = jnp.broadcast_to(gb_s[c][:, c_sz - 1:c_sz, :], (nh, c_sz, LANES)).reshape(rr, LANES)
        hi, mid, lo = _split3(gc)
        x3 = jnp.where(lane == 0, hi, jnp.where(lane == 1, mid, jnp.where(lane == 2, lo, jnp.zeros_like(lo))))
        grow = _nt_dot(pick3, x3)
        decay = jnp.where(tril, jnp.exp(jnp.minimum(jnp.concatenate([gc, gc], axis=1) - grow, 0.0)), 0.0)
        kq = _nt_dot(jnp.concatenate([kb.astype(BF16), qc.astype(BF16)], axis=0), kc.astype(BF16))
        lower = jnp.where(strict, kq[0:rr] * decay, 0.0)
        a_in = jnp.where(tril, kq[rr:2 * rr] * decay, 0.0)
        pw = -lower
        tm = eye + pw
        pw = _dot(pw.astype(BF16), pw.astype(BF16))
        for _ in range(4):
            pwb = pw.astype(BF16)
            both = _dot(jnp.concatenate([tm.astype(BF16), pwb], axis=0), pwb)
            tm = tm + both[0:rr]
            pw = both[rr:2 * rr]
        tm = tm + _dot(tm.astype(BF16), pw.astype(BF16))
        uw = _dot(tm.astype(BF16), jnp.concatenate([stacked(vb_s, c), (kb * eg).astype(BF16)], axis=1))
        u_b[slot] = uw[:, 0:LANES]
        w_b[slot] = uw[:, LANES:2 * LANES].astype(BF16)
        a_b[slot] = a_in.astype(BF16)
        qg_b[slot] = (qc * eg).astype(BF16)
        kd_b[slot] = (kc * jnp.exp(glast - gc)).astype(BF16)

    def phase_b(c, slot):
        stb = st_s[...].astype(BF16)
        wq = _dot(jnp.concatenate([w_b[slot], qg_b[slot]], axis=0), stb)
        u = u_b[slot]

        def head_blocks(r0):
            return jnp.concatenate([wq[r0 + hh * c_sz:r0 + (hh + 1) * c_sz, hh * LANES:(hh + 1) * LANES]
                                    for hh in range(nh)], axis=0)

        vnb = (u - head_blocks(0)).astype(BF16)
        o_s[c] = (head_blocks(rr) + _dot(a_b[slot], vnb)).reshape(nh, c_sz, LANES)
        vbd = jnp.concatenate([jnp.where(row_head == hh, vnb, jnp.zeros_like(vnb)) for hh in range(nh)], axis=1)
        egl = jnp.concatenate([jnp.exp(gb_s[c][hh, c_sz - 1:c_sz, :]) for hh in range(nh)], axis=1)
        st_s[...] = st_s[...] * egl + _tn_dot(kd_b[slot], vbd)

    phase_a(0, 0)

    def pair(kk, carry):
        c = 2 * kk
        phase_b(c, 0)
        phase_a(c + 1, 1)
        phase_b(c + 1, 1)
        phase_a(jnp.minimum(c + 2, nc - 1), 0)
        return carry

    lax.fori_loop(0, nc // 2, pair, 0)

    gain = gain_ref[...]
    for hh in range(nh):
        o = o_s[:, hh].reshape(s, LANES)
        z = qkvz_ref[:, (3 * nh + hh) * LANES:(3 * nh + hh + 1) * LANES].astype(F32)
        on = o * lax.rsqrt(jnp.mean(o * o, axis=-1, keepdims=True) + EPS) * gain
        y_ref[:, hh * LANES:(hh + 1) * LANES] = (on * (z * jax.nn.sigmoid(z))).astype(BF16)


def _gdn(qkvz, ba, conv_w, prm, gain):
    b, s, _ = qkvz.shape
    hd = GDN_HEADS * GDN_D
    nc, rr = s // GDN_CHUNK, GDN_HEADS * GDN_CHUNK
    per_chunk = (nc, GDN_HEADS, GDN_CHUNK, GDN_D)
    return pl.pallas_call(
        _gdn_body,
        grid=(b,),
        in_specs=[pl.BlockSpec((None, s, W_QKVZ), lambda i: (i, 0, 0)),
                  pl.BlockSpec((None, s, W_BA), lambda i: (i, 0, 0)),
                  pl.BlockSpec((GDN_CONV, 3 * hd), lambda i: (0, 0)),
                  pl.BlockSpec((8, LANES), lambda i: (0, 0)),
                  pl.BlockSpec((1, GDN_D), lambda i: (0, 0))],
        out_specs=pl.BlockSpec((None, s, hd), lambda i: (i, 0, 0)),
        out_shape=jax.ShapeDtypeStruct((b, s, hd), BF16),
        scratch_shapes=[pltpu.VMEM(per_chunk, F32),
                        pltpu.VMEM(per_chunk, F32),
                        pltpu.VMEM(per_chunk, F32),
                        pltpu.VMEM(per_chunk, BF16),
                        pltpu.VMEM(per_chunk, F32),
                        pltpu.VMEM(per_chunk, F32),
                        pltpu.VMEM((GDN_D, hd), F32),
                        pltpu.VMEM((2, rr, GDN_D), F32),
                        pltpu.VMEM((2, rr, GDN_D), BF16),
                        pltpu.VMEM((2, rr, rr), BF16),
                        pltpu.VMEM((2, rr, GDN_D), BF16),
                        pltpu.VMEM((2, rr, GDN_D), BF16)],
        compiler_params=_cparams("parallel"),
    )(qkvz, ba, conv_w, prm, gain)


def _attn_body(far_ref, scd_ref, scw_ref, qg_ref, kg_ref, lam_ref, sub_ref, nbt_ref, y_ref,
               qt_s, kn_s, vt_s, ysc_s, qc_s, m_s, l_s, acc_s, *, lambda_init):
    i = pl.program_id(1)
    s = scd_ref.shape[0]
    blk = ATT_BLK
    o_q, o_k, o_v = 3 * SC_WIDTH, 3 * SC_WIDTH + 256, 3 * SC_WIDTH + 512

    @pl.when(i == 0)
    def _prep():
        row = lax.broadcasted_iota(jnp.int32, (s, SC_WIDTH), 0)
        gate_b = scd_ref[:, 0:SC_WIDTH].astype(F32)
        x = scd_ref[:, SC_WIDTH:2 * SC_WIDTH].astype(F32) * scd_ref[:, 2 * SC_WIDTH:3 * SC_WIDTH].astype(F32)
        w = scw_ref[...]
        acc = x * w[SC_CONV - 1:SC_CONV, :]
        for j in range(SC_CONV - 1):
            sh = SC_CONV - 1 - j
            acc = acc + jnp.where(row >= sh, pltpu.roll(x, sh, 0), 0.0) * w[j:j + 1, :]
        ysc_s[...] = (gate_b * acc).astype(BF16)
        gi = lax.broadcasted_iota(jnp.int32, (256, 256), 0) // DIFF_DQK
        gj = lax.broadcasted_iota(jnp.int32, (256, 256), 1) // DIFF_DQK
        bd = (gi == gj).astype(BF16)

        def normed(off, g_ref, sc):
            xx = scd_ref[:, off:off + 256].astype(F32)
            hi, mid, lo = _split3(xx * xx)
            ss = _dot(hi, bd) + _dot(mid, bd) + _dot(lo, bd)
            return xx * lax.rsqrt(ss * (1.0 / DIFF_DQK) + EPS) * g_ref[...] * sc

        qt_s[...] = normed(o_q, qg_ref, DIFF_DQK ** -0.5).T.astype(BF16)
        kn_s[...] = normed(o_k, kg_ref, 1.0).astype(BF16)
        vt_s[...] = scd_ref[:, o_v:o_v + 256].astype(F32).T.astype(BF16)

    lp = lam_ref[...]
    lam = (jnp.exp(jnp.sum(lp[0:1, :] * lp[1:2, :], axis=-1, keepdims=True))
           - jnp.exp(jnp.sum(lp[2:3, :] * lp[3:4, :], axis=-1, keepdims=True)) + lambda_init)

    r0 = pl.multiple_of(i * blk, blk)
    rowi = lax.broadcasted_iota(jnp.int32, (LANES, blk), 0)
    grp = rowi // DIFF_DQK

    m_s[...] = jnp.full_like(m_s, NEG)
    l_s[...] = jnp.zeros_like(l_s)
    acc_s[...] = jnp.zeros_like(acc_s)
    for hp in range(2):
        qp = qt_s[hp * LANES:(hp + 1) * LANES, pl.ds(r0, blk)]
        for sidx in range(4):
            qc_s[hp, :, sidx * blk:(sidx + 1) * blk] = jnp.where(grp == sidx, qp, jnp.zeros_like(qp))

    def block(c0, bias_of):
        for hp in range(2):
            kp = kn_s[pl.ds(c0, blk), hp * LANES:(hp + 1) * LANES]
            vt = vt_s[hp * LANES:(hp + 1) * LANES, pl.ds(c0, blk)]
            logits = _dot(kp, qc_s[hp])
            b0, b1 = bias_of(2 * hp), bias_of(2 * hp + 1)
            if b0.ndim == 2:
                bias = jnp.concatenate([b0, b0, b1, b1], axis=1)
                logits = logits + bias
            else:
                two = 2 * blk
                logits = jnp.concatenate([logits[:, 0:two] + b0, logits[:, two:2 * two] + b1], axis=1)
            m_old = m_s[hp:hp + 1, :]
            m_new = jnp.maximum(m_old, jnp.max(logits, axis=0, keepdims=True))
            alpha = jnp.exp(m_old - m_new)
            p = jnp.exp(logits - m_new)
            l_s[hp:hp + 1, :] = alpha * l_s[hp:hp + 1, :] + jnp.sum(p, axis=0, keepdims=True)
            acc_s[hp] = alpha * acc_s[hp] + _dot(vt, p.astype(BF16))
            m_s[hp:hp + 1, :] = m_new

    def far_body(kb, carry):
        block(pl.multiple_of(kb * blk, blk), lambda head: far_ref[head])
        return carry

    lax.fori_loop(0, jnp.maximum(i - 1, 0), far_body, 0)

    @pl.when(i >= 1)
    def _prev():
        block(pl.multiple_of((i - 1) * blk, blk), lambda head: nbt_ref[head, 0:blk, :])

    block(r0, lambda head: nbt_ref[head, blk:2 * blk, :])

    low = rowi < DIFF_DV
    for hp in range(2):
        outs = []
        for hh in range(2):
            c0, c1 = (2 * hh) * blk, (2 * hh + 1) * blk
            outs.append(acc_s[hp, :, c0:c0 + blk] / l_s[hp:hp + 1, c0:c0 + blk]
                        - lam * (acc_s[hp, :, c1:c1 + blk] / l_s[hp:hp + 1, c1:c1 + blk]))
        o = jnp.where(low, outs[0], outs[1])
        sq = o * o
        ss0 = jnp.sum(jnp.where(low, sq, 0.0), axis=0, keepdims=True)
        ss1 = jnp.sum(jnp.where(low, 0.0, sq), axis=0, keepdims=True)
        ms = jnp.where(low, ss0, ss1) * (1.0 / DIFF_DV)
        y = (o * lax.rsqrt(ms + EPS)).T * sub_ref[...] * (1.0 - lambda_init)
        y_ref[:, SC_WIDTH + hp * LANES:SC_WIDTH + (hp + 1) * LANES] = y.astype(BF16)
    y_ref[:, 0:SC_WIDTH] = ysc_s[pl.ds(r0, blk), :]


def _attn(scd, sc_w, q_gain, k_gain, lam_p, sub_gain, near_bias_t, far_bias, lambda_init):
    b, s, _ = scd.shape
    blk = ATT_BLK
    const2 = lambda bi, i: (0, 0)
    const3 = lambda bi, i: (0, 0, 0)
    return pl.pallas_call(
        functools.partial(_attn_body, lambda_init=lambda_init),
        grid=(b, s // blk),
        in_specs=[pl.BlockSpec(memory_space=pltpu.SMEM),
                  pl.BlockSpec((None, s, W_SCD), lambda bi, i: (bi, 0, 0)),
                  pl.BlockSpec((SC_CONV, SC_WIDTH), const2),
                  pl.BlockSpec((1, 256), const2),
                  pl.BlockSpec((1, 256), const2),
                  pl.BlockSpec((4, DIFF_DQK), const2),
                  pl.BlockSpec((1, LANES), const2),
                  pl.BlockSpec((DIFF_HEADS, 2 * blk, blk), const3)],
        out_specs=pl.BlockSpec((None, blk, 512), lambda bi, i: (bi, i, 0)),
        out_shape=jax.ShapeDtypeStruct((b, s, 512), BF16),
        scratch_shapes=[pltpu.VMEM((256, s), BF16),
                        pltpu.VMEM((s, 256), BF16),
                        pltpu.VMEM((256, s), BF16),
                        pltpu.VMEM((s, SC_WIDTH), BF16),
                        pltpu.VMEM((2, LANES, 4 * blk), BF16),
                        pltpu.VMEM((2, 4 * blk), F32),
                        pltpu.VMEM((2, 4 * blk), F32),
                        pltpu.VMEM((2, LANES, 4 * blk), F32)],
        compiler_params=_cparams("parallel", "arbitrary"),
    )(far_bias, scd, sc_w, q_gain, k_gain, lam_p, sub_gain, near_bias_t)


def _rel_bucket(rel):
    max_exact = REL_BUCKETS // 2
    n = jnp.maximum(rel, 0)
    large = max_exact + (jnp.log(jnp.maximum(n, max_exact).astype(F32) / max_exact)
                         / math.log(REL_MAX_DIST / max_exact) * (REL_BUCKETS - max_exact)).astype(jnp.int32)
    large = jnp.minimum(large, REL_BUCKETS - 1)
    return jnp.where(n < max_exact, n, large)


def _bias_tables(rel_bias):
    blk = ATT_BLK
    rel = jnp.arange(blk)[:, None] + blk - jnp.arange(2 * blk)[None, :]
    onehot = (_rel_bucket(rel)[None] == jnp.arange(REL_BUCKETS)[:, None, None]).astype(F32)
    near = jnp.einsum("brc,bh->hrc", onehot, rel_bias.astype(F32), precision=lax.Precision.HIGHEST)
    near = jnp.where(rel[None] >= 0, near, NEG)
    return jnp.swapaxes(near, 1, 2), rel_bias[REL_BUCKETS - 1].astype(F32)


def _outproj_body(yg_ref, ya_ref, w_ref, h_ref, g_ref, h2_ref, xt_ref):
    hd = yg_ref.shape[1]
    h2 = h_ref[...] + _dot(yg_ref[...], w_ref[0:hd, :]) + _dot(ya_ref[...], w_ref[hd:, :])
    h2_ref[...] = h2
    n = h2 * lax.rsqrt(jnp.mean(h2 * h2, axis=-1, keepdims=True) + EPS) * g_ref[...]
    xt_ref[...] = n.T.astype(BF16)


def _outproj(yg, ya, w, h, gain):
    t, d = h.shape
    tm = min(TOKEN_BLK, t)
    return pl.pallas_call(
        _outproj_body,
        grid=(t // tm,),
        in_specs=[pl.BlockSpec((tm, yg.shape[1]), lambda i: (i, 0)),
                  pl.BlockSpec((tm, ya.shape[1]), lambda i: (i, 0)),
                  pl.BlockSpec((d, d), lambda i: (0, 0)),
                  pl.BlockSpec((tm, d), lambda i: (i, 0)),
                  pl.BlockSpec((1, d), lambda i: (0, 0))],
        out_specs=[pl.BlockSpec((tm, d), lambda i: (i, 0)),
                   pl.BlockSpec((None, d, tm), lambda i: (i, 0, 0))],
        out_shape=[jax.ShapeDtypeStruct((t, d), F32),
                   jax.ShapeDtypeStruct((t // tm, d, tm), BF16)],
        compiler_params=_cparams("parallel"),
    )(yg, ya, w, h, gain)


def _cmpx(lst, i, j):
    a, b = lst[i], lst[j]
    lst[i] = jnp.maximum(a, b)
    lst[j] = jnp.minimum(a, b)


def _bitonic_clean(lst, lo, n):
    d = n // 2
    while d >= 1:
        for k in range(n):
            if (k // d) % 2 == 0:
                _cmpx(lst, lo + k, lo + k + d)
        d //= 2


def _sort_desc(lst, lo, n):
    if n == 1:
        return
    h = n // 2
    _sort_desc(lst, lo, h)
    _sort_desc(lst, lo + h, h)
    for k in range(h):
        _cmpx(lst, lo + k, lo + n - 1 - k)
    _bitonic_clean(lst, lo, h)
    _bitonic_clean(lst, lo + h, h)


def _merge_sublanes(lst):
    n = len(lst)
    for d in (4, 2, 1):
        lst = [jnp.maximum(lst[k], pltpu.roll(lst[n - 1 - k], d, 0)) for k in range(n)]
        _bitonic_clean(lst, 0, n)
    return lst


def _top_sorted(s):
    lst = [s[k * 8:(k + 1) * 8, :] for k in range(s.shape[0] // 8)]
    _sort_desc(lst, 0, len(lst))
    return _merge_sublanes(lst)


def _count_prefix(lst, pred):
    n = len(lst)
    steps = []
    step = n // 2
    while step >= 1:
        steps.append(step)
        step //= 2

    def pick(bits, weights, index):
        if not bits:
            return lst[index]
        return jnp.where(bits[0], pick(bits[1:], weights[1:], index + weights[0]),
                         pick(bits[1:], weights[1:], index))

    bits = []
    for level, step in enumerate(steps):
        bits.append(pred(pick(bits, steps[:level], step - 1)))
    count = jnp.zeros_like(lst[0])
    for bit, step in zip(bits, steps):
        count = count + jnp.where(bit, float(step), 0.0)
    return jnp.where(pred(lst[n - 1]), float(n), count)


def _route_body(xt_ref, wqt_ref, keys_ref, cn_ref, e1_ref, rk_ref, e2_ref, q_s):
    k = PEER_TOPK
    tn = xt_ref.shape[1]
    q_s[...] = _dot(wqt_ref[...], xt_ref[...]).astype(BF16)
    sub = lax.broadcasted_iota(jnp.int32, (8, tn), 0)
    for hh in range(PEER_HEADS):
        sc = []
        for p in range(2):
            r = (hh * 2 + p) * PEER_DHALF
            sc.append(_dot(keys_ref[hh, p], q_s[r:r + PEER_DHALF, :]))
        a = _top_sorted(sc[0])
        b = _top_sorted(sc[1])
        apack, bpack = a[0], b[0]
        for r in range(1, 8):
            apack = jnp.where(sub == r, a[r], apack)
            bpack = jnp.where(sub == r, b[r], bpack)
        cand = [apack + b[i] for i in range(k)]
        extra = [a[8 + i] + bpack for i in range(k - 8)]
        for i in range(8, k):
            cand[i] = jnp.maximum(cand[i], extra[k - 1 - i])
        _bitonic_clean(cand, 0, k)
        best = _merge_sublanes(cand)
        z = jnp.zeros_like(best[0])
        for i in range(k):
            z = z + jnp.exp(best[i] - best[0])
        tau = best[k - 1]
        for r in range(PEER_KEYS // 8):
            rows = slice(r * 8, (r + 1) * 8)
            s1r, s2r = sc[0][rows, :], sc[1][rows, :]
            rk_ref[hh, rows, :] = _count_prefix(b, lambda t: t > s2r).astype(BF16)
            cn_ref[hh, rows, :] = _count_prefix(b, lambda t: s1r + t >= tau)
        e1_ref[hh] = jnp.exp(sc[0] - a[0][0:1, :])
        e2_ref[hh] = (jnp.exp(sc[1] - b[0][0:1, :]) / z[0:1, :]).astype(BF16)


def _route(xt, wqt, keys):
    nb, d, tn = xt.shape
    nq = wqt.shape[0]
    spec = pl.BlockSpec((None, PEER_HEADS, PEER_KEYS, tn), lambda i: (i, 0, 0, 0))
    shp = jax.ShapeDtypeStruct((nb, PEER_HEADS, PEER_KEYS, tn), BF16)
    shp32 = jax.ShapeDtypeStruct((nb, PEER_HEADS, PEER_KEYS, tn), F32)
    return pl.pallas_call(
        _route_body,
        grid=(nb,),
        in_specs=[pl.BlockSpec((None, d, tn), lambda i: (i, 0, 0)),
                  pl.BlockSpec((nq, d), lambda i: (0, 0)),
                  pl.BlockSpec((PEER_HEADS, 2, PEER_KEYS, PEER_DHALF), lambda i: (0, 0, 0, 0))],
        out_specs=[spec, spec, spec, spec],
        out_shape=[shp32, shp32, shp, shp],
        scratch_shapes=[pltpu.VMEM((nq, tn), BF16)],
        compiler_params=_cparams("parallel"),
    )(xt, wqt, keys)


def _gelu(x):
    return 0.5 * x * (1.0 + lax.erf(x * (0.5 ** 0.5)))


def _peer_body(xt_ref, u_ref, vt_ref, cn_ref, e1_ref, rk_ref, e2_ref, h2_ref, o_ref,
               hid_s, coef_s, acc_s, rk_s, e2_s):
    e = pl.program_id(1)
    n_cb, eb, _ = hid_s.shape
    n_i = eb // PEER_KEYS

    @pl.when(e == 0)
    def _init():
        acc_s[...] = jnp.zeros_like(acc_s)
        for cb in range(n_cb):
            rk_s[:, cb] = rk_ref[:, :, :, cb * LANES:(cb + 1) * LANES]
            e2_s[:, cb] = e2_ref[:, :, :, cb * LANES:(cb + 1) * LANES]

    hid = _dot(u_ref[...], xt_ref[...])
    for cb in range(n_cb):
        hid_s[cb] = hid[:, cb * LANES:(cb + 1) * LANES]

    i0 = pl.multiple_of(e * n_i, n_i)
    for cb in range(n_cb):
        cs = slice(cb * LANES, (cb + 1) * LANES)
        for j in range(n_i):
            g = jnp.zeros((PEER_KEYS // BF16_ROWS, BF16_ROWS, LANES), BF16)
            for hh in range(PEER_HEADS):
                cnt = jnp.broadcast_to(cn_ref[hh, pl.ds(i0, n_i), cs][j:j + 1, :], (BF16_ROWS, LANES)).astype(BF16)
                ra = jnp.broadcast_to(e1_ref[hh, pl.ds(i0, n_i), cs][j:j + 1, :], (BF16_ROWS, LANES)).astype(BF16)
                e2 = e2_s[hh, cb]
                g = g + jnp.where(rk_s[hh, cb] < cnt[None], e2, jnp.zeros_like(e2)) * ra[None]
            rows = slice(j * PEER_KEYS, (j + 1) * PEER_KEYS)
            coef_s[cb, rows, :] = g.reshape(PEER_KEYS, LANES) * _gelu(hid_s[cb, rows, :]).astype(BF16)

    coef = jnp.concatenate([coef_s[cb] for cb in range(n_cb)], axis=1)
    acc_s[...] += _dot(vt_ref[...], coef)

    @pl.when(e == pl.num_programs(1) - 1)
    def _fin():
        o_ref[...] = h2_ref[...] + acc_s[...].T


def _peer(xt, u, vt, cn, e1, rk, e2, h2):
    nb, d, tn = xt.shape
    n_blk, _, eb = vt.shape
    t = nb * tn
    hk = pl.BlockSpec((None, PEER_HEADS, PEER_KEYS, tn), lambda i, e: (i, 0, 0, 0))
    tiles = (PEER_HEADS, PEER_KEYS // BF16_ROWS, BF16_ROWS, tn)
    hk16 = pl.BlockSpec((None,) + tiles, lambda i, e: (i, 0, 0, 0, 0))
    n_cb = tn // LANES
    tiles_cb = (PEER_HEADS, n_cb, PEER_KEYS // BF16_ROWS, BF16_ROWS, LANES)
    rk = rk.reshape((nb,) + tiles)
    e2 = e2.reshape((nb,) + tiles)
    return pl.pallas_call(
        _peer_body,
        grid=(nb, n_blk),
        in_specs=[pl.BlockSpec((None, d, tn), lambda i, e: (i, 0, 0)),
                  pl.BlockSpec((eb, d), lambda i, e: (e, 0)),
                  pl.BlockSpec((None, d, eb), lambda i, e: (e, 0, 0)),
                  hk, hk, hk16, hk16,
                  pl.BlockSpec((tn, d), lambda i, e: (i, 0))],
        out_specs=pl.BlockSpec((tn, d), lambda i, e: (i, 0)),
        out_shape=jax.ShapeDtypeStruct((t, d), F32),
        scratch_shapes=[pltpu.VMEM((n_cb, eb, LANES), F32),
                        pltpu.VMEM((n_cb, eb, LANES), BF16),
                        pltpu.VMEM((d, tn), F32),
                        pltpu.VMEM(tiles_cb, BF16),
                        pltpu.VMEM(tiles_cb, BF16)],
        compiler_params=_cparams("parallel", "arbitrary"),
    )(xt, u, vt, cn, e1, rk, e2, h2)


def _pad_lanes(v, offset):
    return jnp.zeros((1, LANES), F32).at[0, offset:offset + v.shape[0]].set(v.astype(F32))


def _layer(h, l, near_bias, far_bias, attn_norm, w_in, gdn_conv, gdn_a_log, gdn_dt_bias, gdn_out_norm,
           sc_conv, diff_q_norm, diff_k_norm, diff_lambda, diff_subln, w_out, ffn_norm,
           peer_wq, peer_keys, peer_u, peer_v, batch):
    t, d = h.shape
    s = t // batch
    lambda_init = 0.8 - 0.6 * math.exp(-0.3 * l)
    n_main = W_QKVZ + W_SCD
    wi = w_in[l]
    w_r = jnp.concatenate([wi[:, 0:W_QKVZ], wi[:, W_QKVZ + 2 * GDN_HEADS:], wi[:, W_QKVZ:W_QKVZ + 2 * GDN_HEADS],
                           jnp.zeros((d, W_BA - 2 * GDN_HEADS), wi.dtype)], axis=1).astype(BF16)
    assert w_r.shape[1] == n_main + W_BA
    qkvz, scd, ba = _inproj(h, attn_norm[l][None, :], w_r)

    prm = jnp.concatenate([_pad_lanes(gdn_a_log[l], GDN_HEADS), _pad_lanes(gdn_dt_bias[l], GDN_HEADS),
                           jnp.zeros((6, LANES), F32)], axis=0)
    y_gdn = _gdn(qkvz.reshape(batch, s, W_QKVZ), ba.reshape(batch, s, W_BA), gdn_conv[l].astype(F32), prm,
                 gdn_out_norm[l][None, :].astype(F32))

    y_att = _attn(scd.reshape(batch, s, W_SCD), sc_conv[l].astype(F32),
                  jnp.tile(diff_q_norm[l], 256 // DIFF_DQK)[None, :].astype(F32),
                  jnp.tile(diff_k_norm[l], 256 // DIFF_DQK)[None, :].astype(F32),
                  diff_lambda[l].astype(F32),
                  jnp.tile(diff_subln[l], LANES // DIFF_DV)[None, :].astype(F32),
                  near_bias, far_bias, lambda_init)

    h2, xt = _outproj(y_gdn.reshape(t, -1), y_att.reshape(t, -1), w_out[l].astype(BF16), h, ffn_norm[l][None, :])

    cn, e1, rk, e2 = _route(xt, peer_wq[l].T.astype(BF16), peer_keys[l].astype(BF16))
    n_exp = peer_v.shape[1]
    vt = peer_v[l].reshape(n_exp // PEER_EXPERT_BLK, PEER_EXPERT_BLK, d).transpose(0, 2, 1).astype(BF16)
    return _peer(xt, peer_u[l].astype(BF16), vt, cn, e1, rk, e2, h2)


def kernel(x, rel_bias, attn_norm, w_in, gdn_conv, gdn_a_log, gdn_dt_bias, gdn_out_norm, sc_conv,
           diff_q_norm, diff_k_norm, diff_lambda, diff_subln, w_out, ffn_norm, peer_wq, peer_keys,
           peer_u, peer_v):
    batch, s, d = x.shape
    near_bias, far_bias = _bias_tables(rel_bias)
    h = x.reshape(batch * s, d)
    for l in range(w_in.shape[0]):
        h = _layer(h, l, near_bias, far_bias, attn_norm, w_in, gdn_conv, gdn_a_log, gdn_dt_bias,
                   gdn_out_norm, sc_conv, diff_q_norm, diff_k_norm, diff_lambda, diff_subln, w_out,
                   ffn_norm, peer_wq, peer_keys, peer_u, peer_v, batch)
    return h.reshape(batch, s, d)
```

```python
import functools
import math

import jax
import jax.numpy as jnp
import numpy as np
from jax import lax
from jax.experimental import pallas as pl
from jax.experimental.pallas import tpu as pltpu

F32 = jnp.float32
BF16 = jnp.bfloat16
EPS = 1e-6
NEG = -1e30

D_MODEL = 1024
GDN_HEADS = 4
GDN_D = 128
GDN_CONV = 4
GDN_CHUNK = 64
SC_WIDTH = 256
SC_CONV = 3
DIFF_HEADS = 4
DIFF_DV = 64
DIFF_DQK = 32
ATT_BLK = 256
REL_BUCKETS = 32
REL_MAX_DIST = 128
PEER_HEADS = 8
PEER_KEYS = 128
PEER_TOPK = 16
PEER_DHALF = 128
LANES = 128
BF16_ROWS = 16
TOKEN_BLK = 512
PEER_EXPERT_BLK = 2048
VMEM_LIMIT = 56 * 1024 * 1024

W_QKVZ = 2048
W_SCD = 1536
W_BA = LANES


def _cparams(*sem):
    return pltpu.CompilerParams(dimension_semantics=sem, vmem_limit_bytes=VMEM_LIMIT)


def _nt_dot(a, b):
    return lax.dot_general(a, b, (((1,), (1,)), ((), ())), preferred_element_type=F32)


def _tn_dot(a, b):
    return lax.dot_general(a, b, (((0,), (0,)), ((), ())), preferred_element_type=F32)


def _dot(a, b):
    return jnp.dot(a, b, preferred_element_type=F32)


def _split3(x):
    hi = x.astype(BF16)
    r = x - hi.astype(F32)
    mid = r.astype(BF16)
    lo = (r - mid.astype(F32)).astype(BF16)
    return hi, mid, lo


def _inproj_body(h_ref, g_ref, w_ref, qkvz_ref, scd_ref, ba_ref):
    x = h_ref[...]
    n = x * lax.rsqrt(jnp.mean(x * x, axis=-1, keepdims=True) + EPS) * g_ref[...]
    nb = n.astype(BF16)
    qkvz_ref[...] = _dot(nb, w_ref[:, 0:W_QKVZ]).astype(BF16)
    scd_ref[...] = _dot(nb, w_ref[:, W_QKVZ:W_QKVZ + W_SCD]).astype(BF16)
    ba_ref[...] = _dot(nb, w_ref[:, W_QKVZ + W_SCD:])


def _inproj(h, gain, w):
    t, d = h.shape
    tm = min(512, t)
    nw = w.shape[1]
    return pl.pallas_call(
        _inproj_body,
        grid=(t // tm,),
        in_specs=[pl.BlockSpec((tm, d), lambda i: (i, 0)),
                  pl.BlockSpec((1, d), lambda i: (0, 0)),
                  pl.BlockSpec((d, nw), lambda i: (0, 0))],
        out_specs=[pl.BlockSpec((tm, W_QKVZ), lambda i: (i, 0)),
                   pl.BlockSpec((tm, W_SCD), lambda i: (i, 0)),
                   pl.BlockSpec((tm, W_BA), lambda i: (i, 0))],
        out_shape=[jax.ShapeDtypeStruct((t, W_QKVZ), BF16),
                   jax.ShapeDtypeStruct((t, W_SCD), BF16),
                   jax.ShapeDtypeStruct((t, W_BA), F32)],
        compiler_params=_cparams("parallel"),
    )(h, gain, w)


def _gdn_body(qkvz_ref, ba_ref, conv_ref, prm_ref, gain_ref, y_ref,
              q_s, k_s, kb_s, vb_s, gb_s, o_s, st_s, u_b, w_b, a_b, qg_b, kd_b):
    s = qkvz_ref.shape[0]
    c_sz, nh = GDN_CHUNK, GDN_HEADS
    nc, rr = s // c_sz, GDN_HEADS * GDN_CHUNK
    row = lax.broadcasted_iota(jnp.int32, (s, LANES), 0)

    ba = ba_ref[...]
    beta = jax.nn.sigmoid(ba)
    xg = ba + prm_ref[1:2, :]
    softplus = jnp.maximum(xg, 0.0) + jnp.log(1.0 + jnp.exp(-jnp.abs(xg)))
    g = -jnp.exp(prm_ref[0:1, :]) * softplus
    pos = row % c_sz
    for sh in (1, 2, 4, 8, 16, 32):
        g = g + jnp.where(pos >= sh, pltpu.roll(g, sh, 0), 0.0)

    def chunked(x):
        return x.reshape(nc, c_sz, LANES)

    for hh in range(nh):
        gb_s[:, hh] = chunked(jnp.broadcast_to(g[:, nh + hh:nh + hh + 1], (s, LANES)))

    scale = GDN_D ** -0.5
    for cb in range(3 * nh):
        x = qkvz_ref[:, cb * LANES:(cb + 1) * LANES].astype(F32)
        w = conv_ref[:, cb * LANES:(cb + 1) * LANES]
        acc = x * w[GDN_CONV - 1:GDN_CONV, :]
        for j in range(GDN_CONV - 1):
            sh = GDN_CONV - 1 - j
            acc = acc + jnp.where(row >= sh, pltpu.roll(x, sh, 0), 0.0) * w[j:j + 1, :]
        y = acc * jax.nn.sigmoid(acc)
        kind, hh = divmod(cb, nh)
        if kind < 2:
            y = y * lax.rsqrt(jnp.sum(y * y, axis=-1, keepdims=True) + EPS)
        if kind == 0:
            q_s[:, hh] = chunked(y * scale)
        elif kind == 1:
            k_s[:, hh] = chunked(y)
            kb_s[:, hh] = chunked(y * beta[:, hh:hh + 1])
        else:
            vb_s[:, hh] = chunked((y * beta[:, hh:hh + 1]).astype(BF16))

    st_s[...] = jnp.zeros_like(st_s)
    ii = lax.broadcasted_iota(jnp.int32, (rr, rr), 0)
    jj = lax.broadcasted_iota(jnp.int32, (rr, rr), 1)
    same_head = (ii // c_sz) == (jj // c_sz)
    tril = jnp.logical_and(same_head, ii >= jj)
    strict = jnp.logical_and(same_head, ii > jj)
    eye = (ii == jj).astype(F32)
    lane = lax.broadcasted_iota(jnp.int32, (rr, LANES), 1)
    pick3 = (lane < 3).astype(BF16)
    row_head = lax.broadcasted_iota(jnp.int32, (rr, LANES), 0) // c_sz

    def stacked(ref, c):
        return ref[c].reshape(rr, LANES)

    def phase_a(c, slot):
        gc = stacked(gb_s, c)
        eg = jnp.exp(gc)
        qc, kc, kb = stacked(q_s, c), stacked(k_s, c), stacked(kb_s, c)
        glast = jnp.broadcast_to(gb_s[c][:, c_sz - 1:c_sz, :], (nh, c_sz, LANES)).reshape(rr, LANES)
        hi, mid, lo = _split3(gc)
        x3 = jnp.where(lane == 0, hi, jnp.where(lane == 1, mid, jnp.where(lane == 2, lo, jnp.zeros_like(lo))))
        grow = _nt_dot(pick3, x3)
        decay = jnp.where(tril, jnp.exp(jnp.minimum(jnp.concatenate([gc, gc], axis=1) - grow, 0.0)), 0.0)
        kq = _nt_dot(jnp.concatenate([kb.astype(BF16), qc.astype(BF16)], axis=0), kc.astype(BF16))
        lower = jnp.where(strict, kq[0:rr] * decay, 0.0)
        a_in = jnp.where(tril, kq[rr:2 * rr] * decay, 0.0)
        pw = -lower
        tm = eye + pw
        pw = _dot(pw.astype(BF16), pw.astype(BF16))
        for _ in range(4):
            pwb = pw.astype(BF16)
            both = _dot(jnp.concatenate([tm.astype(BF16), pwb], axis=0), pwb)
            tm = tm + both[0:rr]
            pw = both[rr:2 * rr]
        tm = tm + _dot(tm.astype(BF16), pw.astype(BF16))
        uw = _dot(tm.astype(BF16), jnp.concatenate([stacked(vb_s, c), (kb * eg).astype(BF16)], axis=1))
        u_b[slot] = uw[:, 0:LANES]
        w_b[slot] = uw[:, LANES:2 * LANES].astype(BF16)
        a_b[slot] = a_in.astype(BF16)
        qg_b[slot] = (qc * eg).astype(BF16)
        kd_b[slot] = (kc * jnp.exp(glast - gc)).astype(BF16)

    def phase_b(c, slot):
        stb = st_s[...].astype(BF16)
        wq = _dot(jnp.concatenate([w_b[slot], qg_b[slot]], axis=0), stb)
        u = u_b[slot]

        def head_blocks(r0):
            return jnp.concatenate([wq[r0 + hh * c_sz:r0 + (hh + 1) * c_sz, hh * LANES:(hh + 1) * LANES]
                                    for hh in range(nh)], axis=0)

        vnb = (u - head_blocks(0)).astype(BF16)
        o_s[c] = (head_blocks(rr) + _dot(a_b[slot], vnb)).reshape(nh, c_sz, LANES)
        vbd = jnp.concatenate([jnp.where(row_head == hh, vnb, jnp.zeros_like(vnb)) for hh in range(nh)], axis=1)
        egl = jnp.concatenate([jnp.exp(gb_s[c][hh, c_sz - 1:c_sz, :]) for hh in range(nh)], axis=1)
        st_s[...] = st_s[...] * egl + _tn_dot(kd_b[slot], vbd)

    phase_a(0, 0)

    def pair(kk, carry):
        c = 2 * kk
        phase_b(c, 0)
        phase_a(c + 1, 1)
        phase_b(c + 1, 1)
        phase_a(jnp.minimum(c + 2, nc - 1), 0)
        return carry

    lax.fori_loop(0, nc // 2, pair, 0)

    gain = gain_ref[...]
    for hh in range(nh):
        o = o_s[:, hh].reshape(s, LANES)
        z = qkvz_ref[:, (3 * nh + hh) * LANES:(3 * nh + hh + 1) * LANES].astype(F32)
        on = o * lax.rsqrt(jnp.mean(o * o, axis=-1, keepdims=True) + EPS) * gain
        y_ref[:, hh * LANES:(hh + 1) * LANES] = (on * (z * jax.nn.sigmoid(z))).astype(BF16)


def _gdn(qkvz, ba, conv_w, prm, gain):
    b, s, _ = qkvz.shape
    hd = GDN_HEADS * GDN_D
    nc, rr = s // GDN_CHUNK, GDN_HEADS * GDN_CHUNK
    per_chunk = (nc, GDN_HEADS, GDN_CHUNK, GDN_D)
    return pl.pallas_call(
        _gdn_body,
        grid=(b,),
        in_specs=[pl.BlockSpec((None, s, W_QKVZ), lambda i: (i, 0, 0)),
                  pl.BlockSpec((None, s, W_BA), lambda i: (i, 0, 0)),
                  pl.BlockSpec((GDN_CONV, 3 * hd), lambda i: (0, 0)),
                  pl.BlockSpec((8, LANES), lambda i: (0, 0)),
                  pl.BlockSpec((1, GDN_D), lambda i: (0, 0))],
        out_specs=pl.BlockSpec((None, s, hd), lambda i: (i, 0, 0)),
        out_shape=jax.ShapeDtypeStruct((b, s, hd), BF16),
        scratch_shapes=[pltpu.VMEM(per_chunk, F32),
                        pltpu.VMEM(per_chunk, F32),
                        pltpu.VMEM(per_chunk, F32),
                        pltpu.VMEM(per_chunk, BF16),
                        pltpu.VMEM(per_chunk, F32),
                        pltpu.VMEM(per_chunk, F32),
                        pltpu.VMEM((GDN_D, hd), F32),
                        pltpu.VMEM((2, rr, GDN_D), F32),
                        pltpu.VMEM((2, rr, GDN_D), BF16),
                        pltpu.VMEM((2, rr, rr), BF16),
                        pltpu.VMEM((2, rr, GDN_D), BF16),
                        pltpu.VMEM((2, rr, GDN_D), BF16)],
        compiler_params=_cparams("parallel"),
    )(qkvz, ba, conv_w, prm, gain)


def _attn_body(far_ref, scd_ref, scw_ref, qg_ref, kg_ref, lam_ref, sub_ref, nbt_ref, y_ref,
               qt_s, kn_s, vt_s, ysc_s, qc_s, m_s, l_s, acc_s, *, lambda_init):
    i = pl.program_id(1)
    s = scd_ref.shape[0]
    blk = ATT_BLK
    o_q, o_k, o_v = 3 * SC_WIDTH, 3 * SC_WIDTH + 256, 3 * SC_WIDTH + 512

    @pl.when(i == 0)
    def _prep():
        row = lax.broadcasted_iota(jnp.int32, (s, SC_WIDTH), 0)
        gate_b = scd_ref[:, 0:SC_WIDTH].astype(F32)
        x = scd_ref[:, SC_WIDTH:2 * SC_WIDTH].astype(F32) * scd_ref[:, 2 * SC_WIDTH:3 * SC_WIDTH].astype(F32)
        w = scw_ref[...]
        acc = x * w[SC_CONV - 1:SC_CONV, :]
        for j in range(SC_CONV - 1):
            sh = SC_CONV - 1 - j
            acc = acc + jnp.where(row >= sh, pltpu.roll(x, sh, 0), 0.0) * w[j:j + 1, :]
        ysc_s[...] = (gate_b * acc).astype(BF16)
        gi = lax.broadcasted_iota(jnp.int32, (256, 256), 0) // DIFF_DQK
        gj = lax.broadcasted_iota(jnp.int32, (256, 256), 1) // DIFF_DQK
        bd = (gi == gj).astype(BF16)

        def normed(off, g_ref, sc):
            xx = scd_ref[:, off:off + 256].astype(F32)
            hi, mid, lo = _split3(xx * xx)
            ss = _dot(hi, bd) + _dot(mid, bd) + _dot(lo, bd)
            return xx * lax.rsqrt(ss * (1.0 / DIFF_DQK) + EPS) * g_ref[...] * sc

        qt_s[...] = normed(o_q, qg_ref, DIFF_DQK ** -0.5).T.astype(BF16)
        kn_s[...] = normed(o_k, kg_ref, 1.0).astype(BF16)
        vt_s[...] = scd_ref[:, o_v:o_v + 256].astype(F32).T.astype(BF16)

    lp = lam_ref[...]
    lam = (jnp.exp(jnp.sum(lp[0:1, :] * lp[1:2, :], axis=-1, keepdims=True))
           - jnp.exp(jnp.sum(lp[2:3, :] * lp[3:4, :], axis=-1, keepdims=True)) + lambda_init)

    r0 = pl.multiple_of(i * blk, blk)
    rowi = lax.broadcasted_iota(jnp.int32, (LANES, blk), 0)
    grp = rowi // DIFF_DQK

    m_s[...] = jnp.full_like(m_s, NEG)
    l_s[...] = jnp.zeros_like(l_s)
    acc_s[...] = jnp.zeros_like(acc_s)
    for hp in range(2):
        qp = qt_s[hp * LANES:(hp + 1) * LANES, pl.ds(r0, blk)]
        for sidx in range(4):
            qc_s[hp, :, sidx * blk:(sidx + 1) * blk] = jnp.where(grp == sidx, qp, jnp.zeros_like(qp))

    def block(c0, bias_of):
        for hp in range(2):
            kp = kn_s[pl.ds(c0, blk), hp * LANES:(hp + 1) * LANES]
            vt = vt_s[hp * LANES:(hp + 1) * LANES, pl.ds(c0, blk)]
            logits = _dot(kp, qc_s[hp])
            b0, b1 = bias_of(2 * hp), bias_of(2 * hp + 1)
            if b0.ndim == 2:
                bias = jnp.concatenate([b0, b0, b1, b1], axis=1)
                logits = logits + bias
            else:
                two = 2 * blk
                logits = jnp.concatenate([logits[:, 0:two] + b0, logits[:, two:2 * two] + b1], axis=1)
            m_old = m_s[hp:hp + 1, :]
            m_new = jnp.maximum(m_old, jnp.max(logits, axis=0, keepdims=True))
            alpha = jnp.exp(m_old - m_new)
            p = jnp.exp(logits - m_new)
            l_s[hp:hp + 1, :] = alpha * l_s[hp:hp + 1, :] + jnp.sum(p, axis=0, keepdims=True)
            acc_s[hp] = alpha * acc_s[hp] + _dot(vt, p.astype(BF16))
            m_s[hp:hp + 1, :] = m_new

    def far_body(kb, carry):
        block(pl.multiple_of(kb * blk, blk), lambda head: far_ref[head])
        return carry

    lax.fori_loop(0, jnp.maximum(i - 1, 0), far_body, 0)

    @pl.when(i >= 1)
    def _prev():
        block(pl.multiple_of((i - 1) * blk, blk), lambda head: nbt_ref[head, 0:blk, :])

    block(r0, lambda head: nbt_ref[head, blk:2 * blk, :])

    low = rowi < DIFF_DV
    for hp in range(2):
        outs = []
        for hh in range(2):
            c0, c1 = (2 * hh) * blk, (2 * hh + 1) * blk
            outs.append(acc_s[hp, :, c0:c0 + blk] / l_s[hp:hp + 1, c0:c0 + blk]
                        - lam * (acc_s[hp, :, c1:c1 + blk] / l_s[hp:hp + 1, c1:c1 + blk]))
        o = jnp.where(low, outs[0], outs[1])
        sq = o * o
        ss0 = jnp.sum(jnp.where(low, sq, 0.0), axis=0, keepdims=True)
        ss1 = jnp.sum(jnp.where(low, 0.0, sq), axis=0, keepdims=True)
        ms = jnp.where(low, ss0, ss1) * (1.0 / DIFF_DV)
        y = (o * lax.rsqrt(ms + EPS)).T * sub_ref[...] * (1.0 - lambda_init)
        y_ref[:, SC_WIDTH + hp * LANES:SC_WIDTH + (hp + 1) * LANES] = y.astype(BF16)
    y_ref[:, 0:SC_WIDTH] = ysc_s[pl.ds(r0, blk), :]


def _attn(scd, sc_w, q_gain, k_gain, lam_p, sub_gain, near_bias_t, far_bias, lambda_init):
    b, s, _ = scd.shape
    blk = ATT_BLK
    const2 = lambda bi, i: (0, 0)
    const3 = lambda bi, i: (0, 0, 0)
    return pl.pallas_call(
        functools.partial(_attn_body, lambda_init=lambda_init),
        grid=(b, s // blk),
        in_specs=[pl.BlockSpec(memory_space=pltpu.SMEM),
                  pl.BlockSpec((None, s, W_SCD), lambda bi, i: (bi, 0, 0)),
                  pl.BlockSpec((SC_CONV, SC_WIDTH), const2),
                  pl.BlockSpec((1, 256), const2),
                  pl.BlockSpec((1, 256), const2),
                  pl.BlockSpec((4, DIFF_DQK), const2),
                  pl.BlockSpec((1, LANES), const2),
                  pl.BlockSpec((DIFF_HEADS, 2 * blk, blk), const3)],
        out_specs=pl.BlockSpec((None, blk, 512), lambda bi, i: (bi, i, 0)),
        out_shape=jax.ShapeDtypeStruct((b, s, 512), BF16),
        scratch_shapes=[pltpu.VMEM((256, s), BF16),
                        pltpu.VMEM((s, 256), BF16),
                        pltpu.VMEM((256, s), BF16),
                        pltpu.VMEM((s, SC_WIDTH), BF16),
                        pltpu.VMEM((2, LANES, 4 * blk), BF16),
                        pltpu.VMEM((2, 4 * blk), F32),
                        pltpu.VMEM((2, 4 * blk), F32),
                        pltpu.VMEM((2, LANES, 4 * blk), F32)],
        compiler_params=_cparams("parallel", "arbitrary"),
    )(far_bias, scd, sc_w, q_gain, k_gain, lam_p, sub_gain, near_bias_t)


def _rel_bucket(rel):
    max_exact = REL_BUCKETS // 2
    n = jnp.maximum(rel, 0)
    large = max_exact + (jnp.log(jnp.maximum(n, max_exact).astype(F32) / max_exact)
                         / math.log(REL_MAX_DIST / max_exact) * (REL_BUCKETS - max_exact)).astype(jnp.int32)
    large = jnp.minimum(large, REL_BUCKETS - 1)
    return jnp.where(n < max_exact, n, large)


def _bias_tables(rel_bias):
    blk = ATT_BLK
    rel = jnp.arange(blk)[:, None] + blk - jnp.arange(2 * blk)[None, :]
    onehot = (_rel_bucket(rel)[None] == jnp.arange(REL_BUCKETS)[:, None, None]).astype(F32)
    near = jnp.einsum("brc,bh->hrc", onehot, rel_bias.astype(F32), precision=lax.Precision.HIGHEST)
    near = jnp.where(rel[None] >= 0, near, NEG)
    return jnp.swapaxes(near, 1, 2), rel_bias[REL_BUCKETS - 1].astype(F32)


def _outproj_body(yg_ref, ya_ref, w_ref, h_ref, g_ref, h2_ref, xt_ref):
    hd = yg_ref.shape[1]
    h2 = h_ref[...] + _dot(yg_ref[...], w_ref[0:hd, :]) + _dot(ya_ref[...], w_ref[hd:, :])
    h2_ref[...] = h2
    n = h2 * lax.rsqrt(jnp.mean(h2 * h2, axis=-1, keepdims=True) + EPS) * g_ref[...]
    xt_ref[...] = n.T.astype(BF16)


def _outproj(yg, ya, w, h, gain):
    t, d = h.shape
    tm = min(TOKEN_BLK, t)
    return pl.pallas_call(
        _outproj_body,
        grid=(t // tm,),
        in_specs=[pl.BlockSpec((tm, yg.shape[1]), lambda i: (i, 0)),
                  pl.BlockSpec((tm, ya.shape[1]), lambda i: (i, 0)),
                  pl.BlockSpec((d, d), lambda i: (0, 0)),
                  pl.BlockSpec((tm, d), lambda i: (i, 0)),
                  pl.BlockSpec((1, d), lambda i: (0, 0))],
        out_specs=[pl.BlockSpec((tm, d), lambda i: (i, 0)),
                   pl.BlockSpec((None, d, tm), lambda i: (i, 0, 0))],
        out_shape=[jax.ShapeDtypeStruct((t, d), F32),
                   jax.ShapeDtypeStruct((t // tm, d, tm), BF16)],
        compiler_params=_cparams("parallel"),
    )(yg, ya, w, h, gain)


def _cmpx(lst, i, j):
    a, b = lst[i], lst[j]
    lst[i] = jnp.maximum(a, b)
    lst[j] = jnp.minimum(a, b)


def _bitonic_clean(lst, lo, n):
    d = n // 2
    while d >= 1:
        for k in range(n):
            if (k // d) % 2 == 0:
                _cmpx(lst, lo + k, lo + k + d)
        d //= 2


def _sort_desc(lst, lo, n):
    if n == 1:
        return
    h = n // 2
    _sort_desc(lst, lo, h)
    _sort_desc(lst, lo + h, h)
    for k in range(h):
        _cmpx(lst, lo + k, lo + n - 1 - k)
    _bitonic_clean(lst, lo, h)
    _bitonic_clean(lst, lo + h, h)


def _merge_sublanes(lst):
    n = len(lst)
    for d in (4, 2, 1):
        lst = [jnp.maximum(lst[k], pltpu.roll(lst[n - 1 - k], d, 0)) for k in range(n)]
        _bitonic_clean(lst, 0, n)
    return lst


def _top_sorted(s):
    lst = [s[k * 8:(k + 1) * 8, :] for k in range(s.shape[0] // 8)]
    _sort_desc(lst, 0, len(lst))
    return _merge_sublanes(lst)


def _count_prefix(lst, pred):
    n = len(lst)
    steps = []
    step = n // 2
    while step >= 1:
        steps.append(step)
        step //= 2

    def pick(bits, weights, index):
        if not bits:
            return lst[index]
        return jnp.where(bits[0], pick(bits[1:], weights[1:], index + weights[0]),
                         pick(bits[1:], weights[1:], index))

    bits = []
    for level, step in enumerate(steps):
        bits.append(pred(pick(bits, steps[:level], step - 1)))
    count = jnp.zeros_like(lst[0])
    for bit, step in zip(bits, steps):
        count = count + jnp.where(bit, float(step), 0.0)
    return jnp.where(pred(lst[n - 1]), float(n), count)


def _route_body(xt_ref, wqt_ref, keys_ref, cn_ref, e1_ref, rk_ref, e2_ref, q_s):
    k = PEER_TOPK
    tn = xt_ref.shape[1]
    q_s[...] = _dot(wqt_ref[...], xt_ref[...]).astype(BF16)
    sub = lax.broadcasted_iota(jnp.int32, (8, tn), 0)
    for hh in range(PEER_HEADS):
        sc = []
        for p in range(2):
            r = (hh * 2 + p) * PEER_DHALF
            sc.append(_dot(keys_ref[hh, p], q_s[r:r + PEER_DHALF, :]))
        a = _top_sorted(sc[0])
        b = _top_sorted(sc[1])
        apack, bpack = a[0], b[0]
        for r in range(1, 8):
            apack = jnp.where(sub == r, a[r], apack)
            bpack = jnp.where(sub == r, b[r], bpack)
        cand = [apack + b[i] for i in range(k)]
        extra = [a[8 + i] + bpack for i in range(k - 8)]
        for i in range(8, k):
            cand[i] = jnp.maximum(cand[i], extra[k - 1 - i])
        _bitonic_clean(cand, 0, k)
        best = _merge_sublanes(cand)
        z = jnp.zeros_like(best[0])
        for i in range(k):
            z = z + jnp.exp(best[i] - best[0])
        tau = best[k - 1]
        for r in range(PEER_KEYS // 8):
            rows = slice(r * 8, (r + 1) * 8)
            s1r, s2r = sc[0][rows, :], sc[1][rows, :]
            rk_ref[hh, rows, :] = _count_prefix(b, lambda t: t > s2r).astype(BF16)
            cn_ref[hh, rows, :] = _count_prefix(b, lambda t: s1r + t >= tau)
        e1_ref[hh] = jnp.exp(sc[0] - a[0][0:1, :])
        e2_ref[hh] = (jnp.exp(sc[1] - b[0][0:1, :]) / z[0:1, :]).astype(BF16)


def _route(xt, wqt, keys):
    nb, d, tn = xt.shape
    nq = wqt.shape[0]
    spec = pl.BlockSpec((None, PEER_HEADS, PEER_KEYS, tn), lambda i: (i, 0, 0, 0))
    shp = jax.ShapeDtypeStruct((nb, PEER_HEADS, PEER_KEYS, tn), BF16)
    shp32 = jax.ShapeDtypeStruct((nb, PEER_HEADS, PEER_KEYS, tn), F32)
    return pl.pallas_call(
        _route_body,
        grid=(nb,),
        in_specs=[pl.BlockSpec((None, d, tn), lambda i: (i, 0, 0)),
                  pl.BlockSpec((nq, d), lambda i: (0, 0)),
                  pl.BlockSpec((PEER_HEADS, 2, PEER_KEYS, PEER_DHALF), lambda i: (0, 0, 0, 0))],
        out_specs=[spec, spec, spec, spec],
        out_shape=[shp32, shp32, shp, shp],
        scratch_shapes=[pltpu.VMEM((nq, tn), BF16)],
        compiler_params=_cparams("parallel"),
    )(xt, wqt, keys)


def _gelu(x):
    return 0.5 * x * (1.0 + lax.erf(x * (0.5 ** 0.5)))


def _peer_body(xt_ref, u_ref, vt_ref, cn_ref, e1_ref, rk_ref, e2_ref, h2_ref, o_ref,
               hid_s, coef_s, acc_s, rk_s, e2_s):
    e = pl.program_id(1)
    eb, tn = hid_s.shape
    n_i = eb // PEER_KEYS

    @pl.when(e == 0)
    def _init():
        acc_s[...] = jnp.zeros_like(acc_s)
        for cb in range(tn // LANES):
            rk_s[:, cb] = rk_ref[:, :, :, cb * LANES:(cb + 1) * LANES]
            e2_s[:, cb] = e2_ref[:, :, :, cb * LANES:(cb + 1) * LANES]

    hid_s[...] = _dot(u_ref[...], xt_ref[...])

    i0 = pl.multiple_of(e * n_i, n_i)
    for cb in range(tn // LANES):
        cs = slice(cb * LANES, (cb + 1) * LANES)
        for j in range(n_i):
            g = jnp.zeros((PEER_KEYS // BF16_ROWS, BF16_ROWS, LANES), BF16)
            for hh in range(PEER_HEADS):
                cnt = jnp.broadcast_to(cn_ref[hh, pl.ds(i0, n_i), cs][j:j + 1, :], (BF16_ROWS, LANES)).astype(BF16)
                ra = jnp.broadcast_to(e1_ref[hh, pl.ds(i0, n_i), cs][j:j + 1, :], (BF16_ROWS, LANES)).astype(BF16)
                e2 = e2_s[hh, cb]
                g = g + jnp.where(rk_s[hh, cb] < cnt[None], e2, jnp.zeros_like(e2)) * ra[None]
            rows = slice(j * PEER_KEYS, (j + 1) * PEER_KEYS)
            coef_s[rows, cs] = g.reshape(PEER_KEYS, LANES) * _gelu(hid_s[rows, cs]).astype(BF16)

    acc_s[...] += _dot(vt_ref[...], coef_s[...])

    @pl.when(e == pl.num_programs(1) - 1)
    def _fin():
        o_ref[...] = h2_ref[...] + acc_s[...].T


def _peer(xt, u, vt, cn, e1, rk, e2, h2):
    nb, d, tn = xt.shape
    n_blk, _, eb = vt.shape
    t = nb * tn
    hk = pl.BlockSpec((None, PEER_HEADS, PEER_KEYS, tn), lambda i, e: (i, 0, 0, 0))
    tiles = (PEER_HEADS, PEER_KEYS // BF16_ROWS, BF16_ROWS, tn)
    hk16 = pl.BlockSpec((None,) + tiles, lambda i, e: (i, 0, 0, 0, 0))
    tiles_cb = (PEER_HEADS, tn // LANES, PEER_KEYS // BF16_ROWS, BF16_ROWS, LANES)
    rk = rk.reshape((nb,) + tiles)
    e2 = e2.reshape((nb,) + tiles)
    return pl.pallas_call(
        _peer_body,
        grid=(nb, n_blk),
        in_specs=[pl.BlockSpec((None, d, tn), lambda i, e: (i, 0, 0)),
                  pl.BlockSpec((eb, d), lambda i, e: (e, 0)),
                  pl.BlockSpec((None, d, eb), lambda i, e: (e, 0, 0)),
                  hk, hk, hk16, hk16,
                  pl.BlockSpec((tn, d), lambda i, e: (i, 0))],
        out_specs=pl.BlockSpec((tn, d), lambda i, e: (i, 0)),
        out_shape=jax.ShapeDtypeStruct((t, d), F32),
        scratch_shapes=[pltpu.VMEM((eb, tn), F32),
                        pltpu.VMEM((eb, tn), BF16),
                        pltpu.VMEM((d, tn), F32),
                        pltpu.VMEM(tiles_cb, BF16),
                        pltpu.VMEM(tiles_cb, BF16)],
        compiler_params=_cparams("parallel", "arbitrary"),
    )(xt, u, vt, cn, e1, rk, e2, h2)


def _pad_lanes(v, offset):
    return jnp.zeros((1, LANES), F32).at[0, offset:offset + v.shape[0]].set(v.astype(F32))


def _layer(h, l, near_bias, far_bias, attn_norm, w_in, gdn_conv, gdn_a_log, gdn_dt_bias, gdn_out_norm,
           sc_conv, diff_q_norm, diff_k_norm, diff_lambda, diff_subln, w_out, ffn_norm,
           peer_wq, peer_keys, peer_u, peer_v, batch):
    t, d = h.shape
    s = t // batch
    lambda_init = 0.8 - 0.6 * math.exp(-0.3 * l)
    n_main = W_QKVZ + W_SCD
    wi = w_in[l]
    w_r = jnp.concatenate([wi[:, 0:W_QKVZ], wi[:, W_QKVZ + 2 * GDN_HEADS:], wi[:, W_QKVZ:W_QKVZ + 2 * GDN_HEADS],
                           jnp.zeros((d, W_BA - 2 * GDN_HEADS), wi.dtype)], axis=1).astype(BF16)
    assert w_r.shape[1] == n_main + W_BA
    qkvz, scd, ba = _inproj(h, attn_norm[l][None, :], w_r)

    prm = jnp.concatenate([_pad_lanes(gdn_a_log[l], GDN_HEADS), _pad_lanes(gdn_dt_bias[l], GDN_HEADS),
                           jnp.zeros((6, LANES), F32)], axis=0)
    y_gdn = _gdn(qkvz.reshape(batch, s, W_QKVZ), ba.reshape(batch, s, W_BA), gdn_conv[l].astype(F32), prm,
                 gdn_out_norm[l][None, :].astype(F32))

    y_att = _attn(scd.reshape(batch, s, W_SCD), sc_conv[l].astype(F32),
                  jnp.tile(diff_q_norm[l], 256 // DIFF_DQK)[None, :].astype(F32),
                  jnp.tile(diff_k_norm[l], 256 // DIFF_DQK)[None, :].astype(F32),
                  diff_lambda[l].astype(F32),
                  jnp.tile(diff_subln[l], LANES // DIFF_DV)[None, :].astype(F32),
                  near_bias, far_bias, lambda_init)

    h2, xt = _outproj(y_gdn.reshape(t, -1), y_att.reshape(t, -1), w_out[l].astype(BF16), h, ffn_norm[l][None, :])

    cn, e1, rk, e2 = _route(xt, peer_wq[l].T.astype(BF16), peer_keys[l].astype(BF16))
    n_exp = peer_v.shape[1]
    vt = peer_v[l].reshape(n_exp // PEER_EXPERT_BLK, PEER_EXPERT_BLK, d).transpose(0, 2, 1).astype(BF16)
    return _peer(xt, peer_u[l].astype(BF16), vt, cn, e1, rk, e2, h2)


def kernel(x, rel_bias, attn_norm, w_in, gdn_conv, gdn_a_log, gdn_dt_bias, gdn_out_norm, sc_conv,
           diff_q_norm, diff_k_norm, diff_lambda, diff_subln, w_out, ffn_norm, peer_wq, peer_keys,
           peer_u, peer_v):
    batch, s, d = x.shape
    near_bias, far_bias = _bias_tables(rel_bias)
    h = x.reshape(batch * s, d)
    for l in range(w_in.shape[0]):
        h = _layer(h, l, near_bias, far_bias, attn_norm, w_in, gdn_conv, gdn_a_log, gdn_dt_bias,
                   gdn_out_norm, sc_conv, diff_q_norm, diff_k_norm, diff_lambda, diff_subln, w_out,
                   ffn_norm, peer_wq, peer_keys, peer_u, peer_v, batch)
    return h.reshape(batch, s, d)
```

```python
import functools
import math

import jax
import jax.numpy as jnp
from jax import lax
from jax.experimental import pallas as pl
from jax.experimental.pallas import tpu as pltpu

F32 = jnp.float32
BF16 = jnp.bfloat16
EPS = 1e-6
NEG = -1e30

D_MODEL = 1024
GDN_HEADS = 4
GDN_D = 128
GDN_CONV = 4
GDN_CHUNK = 64
SC_WIDTH = 256
SC_CONV = 3
DIFF_HEADS = 4
DIFF_DV = 64
DIFF_DQK = 32
ATT_BLK = 256
REL_BUCKETS = 32
REL_MAX_DIST = 128
PEER_HEADS = 8
PEER_KEYS = 128
PEER_TOPK = 16
PEER_DHALF = 128
LANES = 128
BF16_ROWS = 16
TOKEN_BLK = 512
PEER_EXPERT_BLK = 2048
VMEM_LIMIT = 56 * 1024 * 1024

W_QKVZ = 2048
W_SCD = 1536
W_BA = LANES


def _cparams(*sem):
    return pltpu.CompilerParams(dimension_semantics=sem, vmem_limit_bytes=VMEM_LIMIT)


def _nt_dot(a, b):
    return lax.dot_general(a, b, (((1,), (1,)), ((), ())), preferred_element_type=F32)


def _tn_dot(a, b):
    return lax.dot_general(a, b, (((0,), (0,)), ((), ())), preferred_element_type=F32)


def _dot(a, b):
    return jnp.dot(a, b, preferred_element_type=F32)


def _split3(x):
    hi = x.astype(BF16)
    r = x - hi.astype(F32)
    mid = r.astype(BF16)
    lo = (r - mid.astype(F32)).astype(BF16)
    return hi, mid, lo


def _inproj_body(h_ref, g_ref, w_ref, qkvz_ref, scd_ref, ba_ref):
    x = h_ref[...]
    n = x * lax.rsqrt(jnp.mean(x * x, axis=-1, keepdims=True) + EPS) * g_ref[...]
    nb = n.astype(BF16)
    qkvz_ref[...] = _dot(nb, w_ref[:, 0:W_QKVZ]).astype(BF16)
    scd_ref[...] = _dot(nb, w_ref[:, W_QKVZ:W_QKVZ + W_SCD]).astype(BF16)
    ba_ref[...] = _dot(nb, w_ref[:, W_QKVZ + W_SCD:])


def _inproj(h, gain, w):
    t, d = h.shape
    tm = min(TOKEN_BLK, t)
    nw = w.shape[1]
    return pl.pallas_call(
        _inproj_body,
        grid=(t // tm,),
        in_specs=[pl.BlockSpec((tm, d), lambda i: (i, 0)),
                  pl.BlockSpec((1, d), lambda i: (0, 0)),
                  pl.BlockSpec((d, nw), lambda i: (0, 0))],
        out_specs=[pl.BlockSpec((tm, W_QKVZ), lambda i: (i, 0)),
                   pl.BlockSpec((tm, W_SCD), lambda i: (i, 0)),
                   pl.BlockSpec((tm, W_BA), lambda i: (i, 0))],
        out_shape=[jax.ShapeDtypeStruct((t, W_QKVZ), BF16),
                   jax.ShapeDtypeStruct((t, W_SCD), BF16),
                   jax.ShapeDtypeStruct((t, W_BA), F32)],
        compiler_params=_cparams("parallel"),
    )(h, gain, w)


def _gdn_body(qkvz_ref, ba_ref, conv_ref, prm_ref, gain_ref, y_ref,
              q_s, k_s, kb_s, vb_s, gb_s, o_s, st_s, u_b, w_b, a_b, qg_b, kd_b):
    s = qkvz_ref.shape[0]
    c_sz, nh = GDN_CHUNK, GDN_HEADS
    nc, rr = s // c_sz, GDN_HEADS * GDN_CHUNK
    row = lax.broadcasted_iota(jnp.int32, (s, LANES), 0)

    ba = ba_ref[...]
    beta = jax.nn.sigmoid(ba)
    xg = ba + prm_ref[1:2, :]
    softplus = jnp.maximum(xg, 0.0) + jnp.log(1.0 + jnp.exp(-jnp.abs(xg)))
    g = -jnp.exp(prm_ref[0:1, :]) * softplus
    pos = row % c_sz
    for sh in (1, 2, 4, 8, 16, 32):
        g = g + jnp.where(pos >= sh, pltpu.roll(g, sh, 0), 0.0)

    def chunked(x):
        return x.reshape(nc, c_sz, LANES)

    for hh in range(nh):
        gb_s[:, hh] = chunked(jnp.broadcast_to(g[:, nh + hh:nh + hh + 1], (s, LANES)))

    scale = GDN_D ** -0.5
    for cb in range(3 * nh):
        x = qkvz_ref[:, cb * LANES:(cb + 1) * LANES].astype(F32)
        w = conv_ref[:, cb * LANES:(cb + 1) * LANES]
        acc = x * w[GDN_CONV - 1:GDN_CONV, :]
        for j in range(GDN_CONV - 1):
            sh = GDN_CONV - 1 - j
            acc = acc + jnp.where(row >= sh, pltpu.roll(x, sh, 0), 0.0) * w[j:j + 1, :]
        y = acc * jax.nn.sigmoid(acc)
        kind, hh = divmod(cb, nh)
        if kind < 2:
            y = y * lax.rsqrt(jnp.sum(y * y, axis=-1, keepdims=True) + EPS)
        if kind == 0:
            q_s[:, hh] = chunked(y * scale)
        elif kind == 1:
            k_s[:, hh] = chunked(y)
            kb_s[:, hh] = chunked(y * beta[:, hh:hh + 1])
        else:
            vb_s[:, hh] = chunked((y * beta[:, hh:hh + 1]).astype(BF16))

    st_s[...] = jnp.zeros_like(st_s)
    ii = lax.broadcasted_iota(jnp.int32, (rr, rr), 0)
    jj = lax.broadcasted_iota(jnp.int32, (rr, rr), 1)
    same_head = (ii // c_sz) == (jj // c_sz)
    tril = jnp.logical_and(same_head, ii >= jj)
    strict = jnp.logical_and(same_head, ii > jj)
    eye = (ii == jj).astype(F32)
    lane = lax.broadcasted_iota(jnp.int32, (rr, LANES), 1)
    pick3 = (lane < 3).astype(BF16)
    row_head = lax.broadcasted_iota(jnp.int32, (rr, LANES), 0) // c_sz

    def stacked(ref, c):
        return ref[c].reshape(rr, LANES)

    def phase_a(c, slot):
        gc = stacked(gb_s, c)
        eg = jnp.exp(gc)
        qc, kc, kb = stacked(q_s, c), stacked(k_s, c), stacked(kb_s, c)
        glast = jnp.broadcast_to(gb_s[c][:, c_sz - 1:c_sz, :], (nh, c_sz, LANES)).reshape(rr, LANES)
        hi, mid, lo = _split3(gc)
        x3 = jnp.where(lane == 0, hi, jnp.where(lane == 1, mid, jnp.where(lane == 2, lo, jnp.zeros_like(lo))))
        grow = _nt_dot(pick3, x3)
        decay = jnp.where(tril, jnp.exp(jnp.minimum(jnp.concatenate([gc, gc], axis=1) - grow, 0.0)), 0.0)
        kq = _nt_dot(jnp.concatenate([kb.astype(BF16), qc.astype(BF16)], axis=0), kc.astype(BF16))
        lower = jnp.where(strict, kq[0:rr] * decay, 0.0)
        a_in = jnp.where(tril, kq[rr:2 * rr] * decay, 0.0)
        pw = -lower
        tm = eye + pw
        pw = _dot(pw.astype(BF16), pw.astype(BF16))
        for _ in range(4):
            pwb = pw.astype(BF16)
            both = _dot(jnp.concatenate([tm.astype(BF16), pwb], axis=0), pwb)
            tm = tm + both[0:rr]
            pw = both[rr:2 * rr]
        tm = tm + _dot(tm.astype(BF16), pw.astype(BF16))
        uw = _dot(tm.astype(BF16), jnp.concatenate([stacked(vb_s, c), (kb * eg).astype(BF16)], axis=1))
        u_b[slot] = uw[:, 0:LANES]
        w_b[slot] = uw[:, LANES:2 * LANES].astype(BF16)
        a_b[slot] = a_in.astype(BF16)
        qg_b[slot] = (qc * eg).astype(BF16)
        kd_b[slot] = (kc * jnp.exp(glast - gc)).astype(BF16)

    def phase_b(c, slot):
        stb = st_s[...].astype(BF16)
        wq = _dot(jnp.concatenate([w_b[slot], qg_b[slot]], axis=0), stb)
        u = u_b[slot]

        def head_blocks(r0):
            return jnp.concatenate([wq[r0 + hh * c_sz:r0 + (hh + 1) * c_sz, hh * LANES:(hh + 1) * LANES]
                                    for hh in range(nh)], axis=0)

        vnb = (u - head_blocks(0)).astype(BF16)
        o_s[c] = (head_blocks(rr) + _dot(a_b[slot], vnb)).reshape(nh, c_sz, LANES)
        vbd = jnp.concatenate([jnp.where(row_head == hh, vnb, jnp.zeros_like(vnb)) for hh in range(nh)], axis=1)
        egl = jnp.concatenate([jnp.exp(gb_s[c][hh, c_sz - 1:c_sz, :]) for hh in range(nh)], axis=1)
        st_s[...] = st_s[...] * egl + _tn_dot(kd_b[slot], vbd)

    phase_a(0, 0)

    def pair(kk, carry):
        c = 2 * kk
        phase_b(c, 0)
        phase_a(c + 1, 1)
        phase_b(c + 1, 1)
        phase_a(jnp.minimum(c + 2, nc - 1), 0)
        return carry

    lax.fori_loop(0, nc // 2, pair, 0)

    gain = gain_ref[...]
    for hh in range(nh):
        o = o_s[:, hh].reshape(s, LANES)
        z = qkvz_ref[:, (3 * nh + hh) * LANES:(3 * nh + hh + 1) * LANES].astype(F32)
        on = o * lax.rsqrt(jnp.mean(o * o, axis=-1, keepdims=True) + EPS) * gain
        y_ref[:, hh * LANES:(hh + 1) * LANES] = (on * (z * jax.nn.sigmoid(z))).astype(BF16)


def _gdn(qkvz, ba, conv_w, prm, gain):
    b, s, _ = qkvz.shape
    hd = GDN_HEADS * GDN_D
    nc, rr = s // GDN_CHUNK, GDN_HEADS * GDN_CHUNK
    per_chunk = (nc, GDN_HEADS, GDN_CHUNK, GDN_D)
    return pl.pallas_call(
        _gdn_body,
        grid=(b,),
        in_specs=[pl.BlockSpec((None, s, W_QKVZ), lambda i: (i, 0, 0)),
                  pl.BlockSpec((None, s, W_BA), lambda i: (i, 0, 0)),
                  pl.BlockSpec((GDN_CONV, 3 * hd), lambda i: (0, 0)),
                  pl.BlockSpec((8, LANES), lambda i: (0, 0)),
                  pl.BlockSpec((1, GDN_D), lambda i: (0, 0))],
        out_specs=pl.BlockSpec((None, s, hd), lambda i: (i, 0, 0)),
        out_shape=jax.ShapeDtypeStruct((b, s, hd), BF16),
        scratch_shapes=[pltpu.VMEM(per_chunk, F32),
                        pltpu.VMEM(per_chunk, F32),
                        pltpu.VMEM(per_chunk, F32),
                        pltpu.VMEM(per_chunk, BF16),
                        pltpu.VMEM(per_chunk, F32),
                        pltpu.VMEM(per_chunk, F32),
                        pltpu.VMEM((GDN_D, hd), F32),
                        pltpu.VMEM((2, rr, GDN_D), F32),
                        pltpu.VMEM((2, rr, GDN_D), BF16),
                        pltpu.VMEM((2, rr, rr), BF16),
                        pltpu.VMEM((2, rr, GDN_D), BF16),
                        pltpu.VMEM((2, rr, GDN_D), BF16)],
        compiler_params=_cparams("parallel"),
    )(qkvz, ba, conv_w, prm, gain)


def _attn_body(far_ref, scd_ref, scw_ref, qg_ref, kg_ref, lam_ref, sub_ref, nbt_ref, y_ref,
               qt_s, kn_s, vt_s, ysc_s, qc_s, m_s, l_s, acc_s, *, lambda_init):
    i = pl.program_id(1)
    s = scd_ref.shape[0]
    blk = ATT_BLK
    o_q, o_k, o_v = 3 * SC_WIDTH, 3 * SC_WIDTH + 256, 3 * SC_WIDTH + 512

    @pl.when(i == 0)
    def _prep():
        row = lax.broadcasted_iota(jnp.int32, (s, SC_WIDTH), 0)
        gate_b = scd_ref[:, 0:SC_WIDTH].astype(F32)
        x = scd_ref[:, SC_WIDTH:2 * SC_WIDTH].astype(F32) * scd_ref[:, 2 * SC_WIDTH:3 * SC_WIDTH].astype(F32)
        w = scw_ref[...]
        acc = x * w[SC_CONV - 1:SC_CONV, :]
        for j in range(SC_CONV - 1):
            sh = SC_CONV - 1 - j
            acc = acc + jnp.where(row >= sh, pltpu.roll(x, sh, 0), 0.0) * w[j:j + 1, :]
        ysc_s[...] = (gate_b * acc).astype(BF16)
        gi = lax.broadcasted_iota(jnp.int32, (256, 256), 0) // DIFF_DQK
        gj = lax.broadcasted_iota(jnp.int32, (256, 256), 1) // DIFF_DQK
        bd = (gi == gj).astype(BF16)

        def normed(off, g_ref, sc):
            xx = scd_ref[:, off:off + 256].astype(F32)
            hi, mid, lo = _split3(xx * xx)
            ss = _dot(hi, bd) + _dot(mid, bd) + _dot(lo, bd)
            return xx * lax.rsqrt(ss * (1.0 / DIFF_DQK) + EPS) * g_ref[...] * sc

        qt_s[...] = normed(o_q, qg_ref, DIFF_DQK ** -0.5).T.astype(BF16)
        kn_s[...] = normed(o_k, kg_ref, 1.0).astype(BF16)
        vt_s[...] = scd_ref[:, o_v:o_v + 256].astype(F32).T.astype(BF16)

    lp = lam_ref[...]
    lam = (jnp.exp(jnp.sum(lp[0:1, :] * lp[1:2, :], axis=-1, keepdims=True))
           - jnp.exp(jnp.sum(lp[2:3, :] * lp[3:4, :], axis=-1, keepdims=True)) + lambda_init)

    r0 = pl.multiple_of(i * blk, blk)
    rowi = lax.broadcasted_iota(jnp.int32, (LANES, blk), 0)
    grp = rowi // DIFF_DQK

    m_s[...] = jnp.full_like(m_s, NEG)
    l_s[...] = jnp.zeros_like(l_s)
    acc_s[...] = jnp.zeros_like(acc_s)
    for hp in range(2):
        qp = qt_s[hp * LANES:(hp + 1) * LANES, pl.ds(r0, blk)]
        for sidx in range(4):
            qc_s[hp, :, sidx * blk:(sidx + 1) * blk] = jnp.where(grp == sidx, qp, jnp.zeros_like(qp))

    def block(c0, bias_of):
        for hp in range(2):
            kp = kn_s[pl.ds(c0, blk), hp * LANES:(hp + 1) * LANES]
            vt = vt_s[hp * LANES:(hp + 1) * LANES, pl.ds(c0, blk)]
            logits = _dot(kp, qc_s[hp])
            b0, b1 = bias_of(2 * hp), bias_of(2 * hp + 1)
            if b0.ndim == 2:
                bias = jnp.concatenate([b0, b0, b1, b1], axis=1)
                logits = logits + bias
            else:
                two = 2 * blk
                logits = jnp.concatenate([logits[:, 0:two] + b0, logits[:, two:2 * two] + b1], axis=1)
            m_old = m_s[hp:hp + 1, :]
            m_new = jnp.maximum(m_old, jnp.max(logits, axis=0, keepdims=True))
            alpha = jnp.exp(m_old - m_new)
            p = jnp.exp(logits - m_new)
            l_s[hp:hp + 1, :] = alpha * l_s[hp:hp + 1, :] + jnp.sum(p, axis=0, keepdims=True)
            acc_s[hp] = alpha * acc_s[hp] + _dot(vt, p.astype(BF16))
            m_s[hp:hp + 1, :] = m_new

    def far_body(kb, carry):
        block(pl.multiple_of(kb * blk, blk), lambda head: far_ref[head])
        return carry

    lax.fori_loop(0, jnp.maximum(i - 1, 0), far_body, 0)

    @pl.when(i >= 1)
    def _prev():
        block(pl.multiple_of((i - 1) * blk, blk), lambda head: nbt_ref[head, 0:blk, :])

    block(r0, lambda head: nbt_ref[head, blk:2 * blk, :])

    low = rowi < DIFF_DV
    for hp in range(2):
        outs = []
        for hh in range(2):
            c0, c1 = (2 * hh) * blk, (2 * hh + 1) * blk
            outs.append(acc_s[hp, :, c0:c0 + blk] / l_s[hp:hp + 1, c0:c0 + blk]
                        - lam * (acc_s[hp, :, c1:c1 + blk] / l_s[hp:hp + 1, c1:c1 + blk]))
        o = jnp.where(low, outs[0], outs[1])
        sq = o * o
        ss0 = jnp.sum(jnp.where(low, sq, 0.0), axis=0, keepdims=True)
        ss1 = jnp.sum(jnp.where(low, 0.0, sq), axis=0, keepdims=True)
        ms = jnp.where(low, ss0, ss1) * (1.0 / DIFF_DV)
        y = (o * lax.rsqrt(ms + EPS)).T * sub_ref[...] * (1.0 - lambda_init)
        y_ref[:, SC_WIDTH + hp * LANES:SC_WIDTH + (hp + 1) * LANES] = y.astype(BF16)
    y_ref[:, 0:SC_WIDTH] = ysc_s[pl.ds(r0, blk), :]


def _attn(scd, sc_w, q_gain, k_gain, lam_p, sub_gain, near_bias_t, far_bias, lambda_init):
    b, s, _ = scd.shape
    blk = ATT_BLK
    const2 = lambda bi, i: (0, 0)
    const3 = lambda bi, i: (0, 0, 0)
    return pl.pallas_call(
        functools.partial(_attn_body, lambda_init=lambda_init),
        grid=(b, s // blk),
        in_specs=[pl.BlockSpec(memory_space=pltpu.SMEM),
                  pl.BlockSpec((None, s, W_SCD), lambda bi, i: (bi, 0, 0)),
                  pl.BlockSpec((SC_CONV, SC_WIDTH), const2),
                  pl.BlockSpec((1, 256), const2),
                  pl.BlockSpec((1, 256), const2),
                  pl.BlockSpec((4, DIFF_DQK), const2),
                  pl.BlockSpec((1, LANES), const2),
                  pl.BlockSpec((DIFF_HEADS, 2 * blk, blk), const3)],
        out_specs=pl.BlockSpec((None, blk, 512), lambda bi, i: (bi, i, 0)),
        out_shape=jax.ShapeDtypeStruct((b, s, 512), BF16),
        scratch_shapes=[pltpu.VMEM((256, s), BF16),
                        pltpu.VMEM((s, 256), BF16),
                        pltpu.VMEM((256, s), BF16),
                        pltpu.VMEM((s, SC_WIDTH), BF16),
                        pltpu.VMEM((2, LANES, 4 * blk), BF16),
                        pltpu.VMEM((2, 4 * blk), F32),
                        pltpu.VMEM((2, 4 * blk), F32),
                        pltpu.VMEM((2, LANES, 4 * blk), F32)],
        compiler_params=_cparams("parallel", "arbitrary"),
    )(far_bias, scd, sc_w, q_gain, k_gain, lam_p, sub_gain, near_bias_t)


def _rel_bucket(rel):
    max_exact = REL_BUCKETS // 2
    n = jnp.maximum(rel, 0)
    large = max_exact + (jnp.log(jnp.maximum(n, max_exact).astype(F32) / max_exact)
                         / math.log(REL_MAX_DIST / max_exact) * (REL_BUCKETS - max_exact)).astype(jnp.int32)
    large = jnp.minimum(large, REL_BUCKETS - 1)
    return jnp.where(n < max_exact, n, large)


def _bias_tables(rel_bias):
    blk = ATT_BLK
    rel = jnp.arange(blk)[:, None] + blk - jnp.arange(2 * blk)[None, :]
    onehot = (_rel_bucket(rel)[None] == jnp.arange(REL_BUCKETS)[:, None, None]).astype(F32)
    near = jnp.einsum("brc,bh->hrc", onehot, rel_bias.astype(F32), precision=lax.Precision.HIGHEST)
    near = jnp.where(rel[None] >= 0, near, NEG)
    return jnp.swapaxes(near, 1, 2), rel_bias[REL_BUCKETS - 1].astype(F32)


def _outproj_body(yg_ref, ya_ref, w_ref, h_ref, g_ref, h2_ref, xt_ref):
    hd = yg_ref.shape[1]
    h2 = h_ref[...] + _dot(yg_ref[...], w_ref[0:hd, :]) + _dot(ya_ref[...], w_ref[hd:, :])
    h2_ref[...] = h2
    n = h2 * lax.rsqrt(jnp.mean(h2 * h2, axis=-1, keepdims=True) + EPS) * g_ref[...]
    xt_ref[...] = n.T.astype(BF16)


def _outproj(yg, ya, w, h, gain):
    t, d = h.shape
    tm = min(TOKEN_BLK, t)
    return pl.pallas_call(
        _outproj_body,
        grid=(t // tm,),
        in_specs=[pl.BlockSpec((tm, yg.shape[1]), lambda i: (i, 0)),
                  pl.BlockSpec((tm, ya.shape[1]), lambda i: (i, 0)),
                  pl.BlockSpec((d, d), lambda i: (0, 0)),
                  pl.BlockSpec((tm, d), lambda i: (i, 0)),
                  pl.BlockSpec((1, d), lambda i: (0, 0))],
        out_specs=[pl.BlockSpec((tm, d), lambda i: (i, 0)),
                   pl.BlockSpec((None, d, tm), lambda i: (i, 0, 0))],
        out_shape=[jax.ShapeDtypeStruct((t, d), F32),
                   jax.ShapeDtypeStruct((t // tm, d, tm), BF16)],
        compiler_params=_cparams("parallel"),
    )(yg, ya, w, h, gain)


def _cmpx(lst, i, j):
    a, b = lst[i], lst[j]
    lst[i] = jnp.maximum(a, b)
    lst[j] = jnp.minimum(a, b)


def _bitonic_clean(lst, lo, n):
    d = n // 2
    while d >= 1:
        for k in range(n):
            if (k // d) % 2 == 0:
                _cmpx(lst, lo + k, lo + k + d)
        d //= 2


def _sort_desc(lst, lo, n):
    if n == 1:
        return
    h = n // 2
    _sort_desc(lst, lo, h)
    _sort_desc(lst, lo + h, h)
    for k in range(h):
        _cmpx(lst, lo + k, lo + n - 1 - k)
    _bitonic_clean(lst, lo, h)
    _bitonic_clean(lst, lo + h, h)


def _merge_sublanes(lst):
    n = len(lst)
    for d in (4, 2, 1):
        lst = [jnp.maximum(lst[k], pltpu.roll(lst[n - 1 - k], d, 0)) for k in range(n)]
        _bitonic_clean(lst, 0, n)
    return lst


def _top_sorted(s):
    lst = [s[k * 8:(k + 1) * 8, :] for k in range(s.shape[0] // 8)]
    _sort_desc(lst, 0, len(lst))
    return _merge_sublanes(lst)


def _count_prefix(lst, pred):
    n = len(lst)
    steps = []
    step = n // 2
    while step >= 1:
        steps.append(step)
        step //= 2

    def pick(bits, weights, index):
        if not bits:
            return lst[index]
        return jnp.where(bits[0], pick(bits[1:], weights[1:], index + weights[0]),
                         pick(bits[1:], weights[1:], index))

    bits = []
    for level, step in enumerate(steps):
        bits.append(pred(pick(bits, steps[:level], step - 1)))
    count = jnp.zeros_like(lst[0])
    for bit, step in zip(bits, steps):
        count = count + jnp.where(bit, float(step), 0.0)
    return jnp.where(pred(lst[n - 1]), float(n), count)


def _route_body(xt_ref, wqt_ref, keys_ref, cn_ref, e1_ref, rk_ref, e2_ref, q_s):
    k = PEER_TOPK
    tn = xt_ref.shape[1]
    q_s[...] = _dot(wqt_ref[...], xt_ref[...]).astype(BF16)
    sub = lax.broadcasted_iota(jnp.int32, (8, tn), 0)
    for hh in range(PEER_HEADS):
        sc = []
        for p in range(2):
            r = (hh * 2 + p) * PEER_DHALF
            sc.append(_dot(keys_ref[hh, p], q_s[r:r + PEER_DHALF, :]))
        a = _top_sorted(sc[0])
        b = _top_sorted(sc[1])
        apack, bpack = a[0], b[0]
        for r in range(1, 8):
            apack = jnp.where(sub == r, a[r], apack)
            bpack = jnp.where(sub == r, b[r], bpack)
        cand = [apack + b[i] for i in range(k)]
        extra = [a[8 + i] + bpack for i in range(k - 8)]
        for i in range(8, k):
            cand[i] = jnp.maximum(cand[i], extra[k - 1 - i])
        _bitonic_clean(cand, 0, k)
        best = _merge_sublanes(cand)
        z = jnp.zeros_like(best[0])
        for i in range(k):
            z = z + jnp.exp(best[i] - best[0])
        tau = best[k - 1]
        for r in range(PEER_KEYS // 8):
            rows = slice(r * 8, (r + 1) * 8)
            s1r, s2r = sc[0][rows, :], sc[1][rows, :]
            rk_ref[hh, rows, :] = _count_prefix(b, lambda t: t > s2r).astype(BF16)
            cn_ref[hh, rows, :] = _count_prefix(b, lambda t: s1r + t >= tau)
        e1_ref[hh] = jnp.exp(sc[0] - a[0][0:1, :])
        e2_ref[hh] = (jnp.exp(sc[1] - b[0][0:1, :]) * (0.5 / z[0:1, :])).astype(BF16)


def _route(xt, wqt, keys):
    nb, d, tn = xt.shape
    nq = wqt.shape[0]
    spec = pl.BlockSpec((None, PEER_HEADS, PEER_KEYS, tn), lambda i: (i, 0, 0, 0))
    shp = jax.ShapeDtypeStruct((nb, PEER_HEADS, PEER_KEYS, tn), BF16)
    shp32 = jax.ShapeDtypeStruct((nb, PEER_HEADS, PEER_KEYS, tn), F32)
    return pl.pallas_call(
        _route_body,
        grid=(nb,),
        in_specs=[pl.BlockSpec((None, d, tn), lambda i: (i, 0, 0)),
                  pl.BlockSpec((nq, d), lambda i: (0, 0)),
                  pl.BlockSpec((PEER_HEADS, 2, PEER_KEYS, PEER_DHALF), lambda i: (0, 0, 0, 0))],
        out_specs=[spec, spec, spec, spec],
        out_shape=[shp32, shp32, shp, shp],
        scratch_shapes=[pltpu.VMEM((nq, tn), BF16)],
        compiler_params=_cparams("parallel"),
    )(xt, wqt, keys)


def _gelu_x2(x):
    return x * (1.0 + lax.erf(x * (0.5 ** 0.5)))


def _peer_body(xt_ref, u_ref, vt_ref, cn_ref, e1_ref, rk_ref, e2_ref, h2_ref, o_ref,
               hid_s, coef_s, acc_s, rk_s, e2_s):
    e = pl.program_id(1)
    eb, tn = hid_s.shape
    n_i = eb // PEER_KEYS

    @pl.when(e == 0)
    def _init():
        acc_s[...] = jnp.zeros_like(acc_s)
        rk_s[...] = rk_ref[...]
        e2_s[...] = e2_ref[...]

    hid_s[...] = _dot(u_ref[...], xt_ref[...])

    i0 = pl.multiple_of(e * n_i, n_i)
    for cb in range(tn // LANES):
        cs = slice(cb * LANES, (cb + 1) * LANES)
        for j in range(n_i):
            g = jnp.zeros((PEER_KEYS // BF16_ROWS, BF16_ROWS, LANES), BF16)
            for hh in range(PEER_HEADS):
                cnt = jnp.broadcast_to(cn_ref[hh, pl.ds(i0, n_i), cs][j:j + 1, :], (BF16_ROWS, LANES)).astype(BF16)
                ra = jnp.broadcast_to(e1_ref[hh, pl.ds(i0, n_i), cs][j:j + 1, :], (BF16_ROWS, LANES)).astype(BF16)
                e2 = e2_s[hh, :, :, cs]
                g = g + jnp.where(rk_s[hh, :, :, cs] < cnt[None], e2, jnp.zeros_like(e2)) * ra[None]
            rows = slice(j * PEER_KEYS, (j + 1) * PEER_KEYS)
            coef_s[rows, cs] = g.reshape(PEER_KEYS, LANES) * _gelu_x2(hid_s[rows, cs]).astype(BF16)

    acc_s[...] += _dot(vt_ref[...], coef_s[...])

    @pl.when(e == pl.num_programs(1) - 1)
    def _fin():
        o_ref[...] = h2_ref[...] + acc_s[...].T


def _peer(xt, u, vt, cn, e1, rk, e2, h2):
    nb, d, tn = xt.shape
    n_blk, _, eb = vt.shape
    t = nb * tn
    hk = pl.BlockSpec((None, PEER_HEADS, PEER_KEYS, tn), lambda i, e: (i, 0, 0, 0))
    tiles = (PEER_HEADS, PEER_KEYS // BF16_ROWS, BF16_ROWS, tn)
    hk16 = pl.BlockSpec((None,) + tiles, lambda i, e: (i, 0, 0, 0, 0))
    rk = rk.reshape((nb,) + tiles)
    e2 = e2.reshape((nb,) + tiles)
    return pl.pallas_call(
        _peer_body,
        grid=(nb, n_blk),
        in_specs=[pl.BlockSpec((None, d, tn), lambda i, e: (i, 0, 0)),
                  pl.BlockSpec((eb, d), lambda i, e: (e, 0)),
                  pl.BlockSpec((None, d, eb), lambda i, e: (e, 0, 0)),
                  hk, hk, hk16, hk16,
                  pl.BlockSpec((tn, d), lambda i, e: (i, 0))],
        out_specs=pl.BlockSpec((tn, d), lambda i, e: (i, 0)),
        out_shape=jax.ShapeDtypeStruct((t, d), F32),
        scratch_shapes=[pltpu.VMEM((eb, tn), F32),
                        pltpu.VMEM((eb, tn), BF16),
                        pltpu.VMEM((d, tn), F32),
                        pltpu.VMEM(tiles, BF16),
                        pltpu.VMEM(tiles, BF16)],
        compiler_params=_cparams("parallel", "arbitrary"),
    )(xt, u, vt, cn, e1, rk, e2, h2)


def _pad_lanes(v, offset):
    return jnp.zeros((1, LANES), F32).at[0, offset:offset + v.shape[0]].set(v.astype(F32))


def _layer(h, l, near_bias, far_bias, attn_norm, w_in, gdn_conv, gdn_a_log, gdn_dt_bias, gdn_out_norm,
           sc_conv, diff_q_norm, diff_k_norm, diff_lambda, diff_subln, w_out, ffn_norm,
           peer_wq, peer_keys, peer_u, peer_v, batch):
    t, d = h.shape
    s = t // batch
    lambda_init = 0.8 - 0.6 * math.exp(-0.3 * l)
    n_main = W_QKVZ + W_SCD
    wi = w_in[l]
    w_r = jnp.concatenate([wi[:, 0:W_QKVZ], wi[:, W_QKVZ + 2 * GDN_HEADS:], wi[:, W_QKVZ:W_QKVZ + 2 * GDN_HEADS],
                           jnp.zeros((d, W_BA - 2 * GDN_HEADS), wi.dtype)], axis=1).astype(BF16)
    assert w_r.shape[1] == n_main + W_BA
    qkvz, scd, ba = _inproj(h, attn_norm[l][None, :], w_r)

    prm = jnp.concatenate([_pad_lanes(gdn_a_log[l], GDN_HEADS), _pad_lanes(gdn_dt_bias[l], GDN_HEADS),
                           jnp.zeros((6, LANES), F32)], axis=0)
    y_gdn = _gdn(qkvz.reshape(batch, s, W_QKVZ), ba.reshape(batch, s, W_BA), gdn_conv[l].astype(F32), prm,
                 gdn_out_norm[l][None, :].astype(F32))

    y_att = _attn(scd.reshape(batch, s, W_SCD), sc_conv[l].astype(F32),
                  jnp.tile(diff_q_norm[l], 256 // DIFF_DQK)[None, :].astype(F32),
                  jnp.tile(diff_k_norm[l], 256 // DIFF_DQK)[None, :].astype(F32),
                  diff_lambda[l].astype(F32),
                  jnp.tile(diff_subln[l], LANES // DIFF_DV)[None, :].astype(F32),
                  near_bias, far_bias, lambda_init)

    h2, xt = _outproj(y_gdn.reshape(t, -1), y_att.reshape(t, -1), w_out[l].astype(BF16), h, ffn_norm[l][None, :])

    cn, e1, rk, e2 = _route(xt, peer_wq[l].T.astype(BF16), peer_keys[l].astype(BF16))
    n_exp = peer_v.shape[1]
    vt = peer_v[l].reshape(n_exp // PEER_EXPERT_BLK, PEER_EXPERT_BLK, d).transpose(0, 2, 1).astype(BF16)
    return _peer(xt, peer_u[l].astype(BF16), vt, cn, e1, rk, e2, h2)


def kernel(x, rel_bias, attn_norm, w_in, gdn_conv, gdn_a_log, gdn_dt_bias, gdn_out_norm, sc_conv,
           diff_q_norm, diff_k_norm, diff_lambda, diff_subln, w_out, ffn_norm, peer_wq, peer_keys,
           peer_u, peer_v):
    batch, s, d = x.shape
    near_bias, far_bias = _bias_tables(rel_bias)
    h = x.reshape(batch * s, d)
    for l in range(w_in.shape[0]):
        h = _layer(h, l, near_bias, far_bias, attn_norm, w_in, gdn_conv, gdn_a_log, gdn_dt_bias,
                   gdn_out_norm, sc_conv, diff_q_norm, diff_k_norm, diff_lambda, diff_subln, w_out,
                   ffn_norm, peer_wq, peer_keys, peer_u, peer_v, batch)
    return h.reshape(batch, s, d)
```

```python
import functools
import math

import jax
import jax.numpy as jnp
from jax import lax
from jax.experimental import pallas as pl
from jax.experimental.pallas import tpu as pltpu

F32 = jnp.float32
BF16 = jnp.bfloat16
EPS = 1e-6
RSQRT2 = 0.5 ** 0.5
NEG = -1e30

D_MODEL = 1024
GDN_HEADS = 4
GDN_D = 128
GDN_CONV = 4
GDN_CHUNK = 64
SC_WIDTH = 256
SC_CONV = 3
DIFF_HEADS = 4
DIFF_DV = 64
DIFF_DQK = 32
ATT_BLK = 256
REL_BUCKETS = 32
REL_MAX_DIST = 128
PEER_HEADS = 8
PEER_KEYS = 128
PEER_TOPK = 16
PEER_DHALF = 128
LANES = 128
BF16_ROWS = 16
TOKEN_BLK = 512
PEER_EXPERT_BLK = 2048
VMEM_LIMIT = 56 * 1024 * 1024

W_QKVZ = 2048
W_SCD = 1536
W_BA = LANES


def _cparams(*sem):
    return pltpu.CompilerParams(dimension_semantics=sem, vmem_limit_bytes=VMEM_LIMIT)


def _nt_dot(a, b):
    return lax.dot_general(a, b, (((1,), (1,)), ((), ())), preferred_element_type=F32)


def _tn_dot(a, b):
    return lax.dot_general(a, b, (((0,), (0,)), ((), ())), preferred_element_type=F32)


def _dot(a, b):
    return jnp.dot(a, b, preferred_element_type=F32)


def _split3(x):
    hi = x.astype(BF16)
    r = x - hi.astype(F32)
    mid = r.astype(BF16)
    lo = (r - mid.astype(F32)).astype(BF16)
    return hi, mid, lo


def _inproj_body(h_ref, g_ref, w_ref, qkvz_ref, scd_ref, ba_ref):
    x = h_ref[...]
    n = x * lax.rsqrt(jnp.mean(x * x, axis=-1, keepdims=True) + EPS) * g_ref[...]
    nb = n.astype(BF16)
    qkvz_ref[...] = _dot(nb, w_ref[:, 0:W_QKVZ]).astype(BF16)
    scd_ref[...] = _dot(nb, w_ref[:, W_QKVZ:W_QKVZ + W_SCD]).astype(BF16)
    ba_ref[...] = _dot(nb, w_ref[:, W_QKVZ + W_SCD:])


def _inproj(h, gain, w):
    t, d = h.shape
    tm = min(TOKEN_BLK, t)
    nw = w.shape[1]
    return pl.pallas_call(
        _inproj_body,
        grid=(t // tm,),
        in_specs=[pl.BlockSpec((tm, d), lambda i: (i, 0)),
                  pl.BlockSpec((1, d), lambda i: (0, 0)),
                  pl.BlockSpec((d, nw), lambda i: (0, 0))],
        out_specs=[pl.BlockSpec((tm, W_QKVZ), lambda i: (i, 0)),
                   pl.BlockSpec((tm, W_SCD), lambda i: (i, 0)),
                   pl.BlockSpec((tm, W_BA), lambda i: (i, 0))],
        out_shape=[jax.ShapeDtypeStruct((t, W_QKVZ), BF16),
                   jax.ShapeDtypeStruct((t, W_SCD), BF16),
                   jax.ShapeDtypeStruct((t, W_BA), F32)],
        compiler_params=_cparams("parallel"),
    )(h, gain, w)


def _gdn_body(qkvz_ref, ba_ref, conv_ref, prm_ref, gain_ref, y_ref,
              q_s, k_s, kb_s, vb_s, gb_s, o_s, st_s, u_b, w_b, a_b, qg_b, kd_b):
    s = qkvz_ref.shape[0]
    c_sz, nh = GDN_CHUNK, GDN_HEADS
    nc, rr = s // c_sz, GDN_HEADS * GDN_CHUNK
    row = lax.broadcasted_iota(jnp.int32, (s, LANES), 0)

    ba = ba_ref[...]
    beta = jax.nn.sigmoid(ba)
    xg = ba + prm_ref[1:2, :]
    softplus = jnp.maximum(xg, 0.0) + jnp.log(1.0 + jnp.exp(-jnp.abs(xg)))
    g = -jnp.exp(prm_ref[0:1, :]) * softplus
    pos = row % c_sz
    for sh in (1, 2, 4, 8, 16, 32):
        g = g + jnp.where(pos >= sh, pltpu.roll(g, sh, 0), 0.0)

    def chunked(x):
        return x.reshape(nc, c_sz, LANES)

    for hh in range(nh):
        gb_s[:, hh] = chunked(jnp.broadcast_to(g[:, nh + hh:nh + hh + 1], (s, LANES)))

    scale = GDN_D ** -0.5
    for cb in range(3 * nh):
        x = qkvz_ref[:, cb * LANES:(cb + 1) * LANES].astype(F32)
        w = conv_ref[:, cb * LANES:(cb + 1) * LANES]
        acc = x * w[GDN_CONV - 1:GDN_CONV, :]
        for j in range(GDN_CONV - 1):
            sh = GDN_CONV - 1 - j
            acc = acc + jnp.where(row >= sh, pltpu.roll(x, sh, 0), 0.0) * w[j:j + 1, :]
        y = acc * jax.nn.sigmoid(acc)
        kind, hh = divmod(cb, nh)
        if kind < 2:
            y = y * lax.rsqrt(jnp.sum(y * y, axis=-1, keepdims=True) + EPS)
        if kind == 0:
            q_s[:, hh] = chunked(y * scale)
        elif kind == 1:
            k_s[:, hh] = chunked(y)
            kb_s[:, hh] = chunked(y * beta[:, hh:hh + 1])
        else:
            vb_s[:, hh] = chunked((y * beta[:, hh:hh + 1]).astype(BF16))

    st_s[...] = jnp.zeros_like(st_s)
    ii = lax.broadcasted_iota(jnp.int32, (rr, rr), 0)
    jj = lax.broadcasted_iota(jnp.int32, (rr, rr), 1)
    same_head = (ii // c_sz) == (jj // c_sz)
    tril = jnp.logical_and(same_head, ii >= jj)
    strict = jnp.logical_and(same_head, ii > jj)
    eye = (ii == jj).astype(F32)
    lane = lax.broadcasted_iota(jnp.int32, (rr, LANES), 1)
    pick3 = (lane < 3).astype(BF16)
    row_head = lax.broadcasted_iota(jnp.int32, (rr, LANES), 0) // c_sz

    def stacked(ref, c):
        return ref[c].reshape(rr, LANES)

    def phase_a(c, slot):
        gc = stacked(gb_s, c)
        eg = jnp.exp(gc)
        qc, kc, kb = stacked(q_s, c), stacked(k_s, c), stacked(kb_s, c)
        glast = jnp.broadcast_to(gb_s[c][:, c_sz - 1:c_sz, :], (nh, c_sz, LANES)).reshape(rr, LANES)
        hi, mid, lo = _split3(gc)
        x3 = jnp.where(lane == 0, hi, jnp.where(lane == 1, mid, jnp.where(lane == 2, lo, jnp.zeros_like(lo))))
        grow = _nt_dot(pick3, x3)
        decay = jnp.where(tril, jnp.exp(jnp.minimum(jnp.concatenate([gc, gc], axis=1) - grow, 0.0)), 0.0)
        kq = _nt_dot(jnp.concatenate([kb.astype(BF16), qc.astype(BF16)], axis=0), kc.astype(BF16))
        lower = jnp.where(strict, kq[0:rr] * decay, 0.0)
        a_in = jnp.where(tril, kq[rr:2 * rr] * decay, 0.0)
        pw = -lower
        tm = eye + pw
        pw = _dot(pw.astype(BF16), pw.astype(BF16))
        for _ in range(4):
            pwb = pw.astype(BF16)
            both = _dot(jnp.concatenate([tm.astype(BF16), pwb], axis=0), pwb)
            tm = tm + both[0:rr]
            pw = both[rr:2 * rr]
        tm = tm + _dot(tm.astype(BF16), pw.astype(BF16))
        uw = _dot(tm.astype(BF16), jnp.concatenate([stacked(vb_s, c), (kb * eg).astype(BF16)], axis=1))
        u_b[slot] = uw[:, 0:LANES]
        w_b[slot] = uw[:, LANES:2 * LANES].astype(BF16)
        a_b[slot] = a_in.astype(BF16)
        qg_b[slot] = (qc * eg).astype(BF16)
        kd_b[slot] = (kc * jnp.exp(glast - gc)).astype(BF16)

    def phase_b(c, slot):
        stb = st_s[...].astype(BF16)
        wq = _dot(jnp.concatenate([w_b[slot], qg_b[slot]], axis=0), stb)
        u = u_b[slot]

        def head_blocks(r0):
            return jnp.concatenate([wq[r0 + hh * c_sz:r0 + (hh + 1) * c_sz, hh * LANES:(hh + 1) * LANES]
                                    for hh in range(nh)], axis=0)

        vnb = (u - head_blocks(0)).astype(BF16)
        o_s[c] = (head_blocks(rr) + _dot(a_b[slot], vnb)).reshape(nh, c_sz, LANES)
        vbd = jnp.concatenate([jnp.where(row_head == hh, vnb, jnp.zeros_like(vnb)) for hh in range(nh)], axis=1)
        egl = jnp.concatenate([jnp.exp(gb_s[c][hh, c_sz - 1:c_sz, :]) for hh in range(nh)], axis=1)
        st_s[...] = st_s[...] * egl + _tn_dot(kd_b[slot], vbd)

    phase_a(0, 0)

    def pair(kk, carry):
        c = 2 * kk
        phase_b(c, 0)
        phase_a(c + 1, 1)
        phase_b(c + 1, 1)
        phase_a(jnp.minimum(c + 2, nc - 1), 0)
        return carry

    lax.fori_loop(0, nc // 2, pair, 0)

    gain = gain_ref[...]
    for hh in range(nh):
        o = o_s[:, hh].reshape(s, LANES)
        z = qkvz_ref[:, (3 * nh + hh) * LANES:(3 * nh + hh + 1) * LANES].astype(F32)
        on = o * lax.rsqrt(jnp.mean(o * o, axis=-1, keepdims=True) + EPS) * gain
        y_ref[:, hh * LANES:(hh + 1) * LANES] = (on * (z * jax.nn.sigmoid(z))).astype(BF16)


def _gdn(qkvz, ba, conv_w, prm, gain):
    b, s, _ = qkvz.shape
    hd = GDN_HEADS * GDN_D
    nc, rr = s // GDN_CHUNK, GDN_HEADS * GDN_CHUNK
    per_chunk = (nc, GDN_HEADS, GDN_CHUNK, GDN_D)
    return pl.pallas_call(
        _gdn_body,
        grid=(b,),
        in_specs=[pl.BlockSpec((None, s, W_QKVZ), lambda i: (i, 0, 0)),
                  pl.BlockSpec((None, s, W_BA), lambda i: (i, 0, 0)),
                  pl.BlockSpec((GDN_CONV, 3 * hd), lambda i: (0, 0)),
                  pl.BlockSpec((8, LANES), lambda i: (0, 0)),
                  pl.BlockSpec((1, GDN_D), lambda i: (0, 0))],
        out_specs=pl.BlockSpec((None, s, hd), lambda i: (i, 0, 0)),
        out_shape=jax.ShapeDtypeStruct((b, s, hd), BF16),
        scratch_shapes=[pltpu.VMEM(per_chunk, F32),
                        pltpu.VMEM(per_chunk, F32),
                        pltpu.VMEM(per_chunk, F32),
                        pltpu.VMEM(per_chunk, BF16),
                        pltpu.VMEM(per_chunk, F32),
                        pltpu.VMEM(per_chunk, F32),
                        pltpu.VMEM((GDN_D, hd), F32),
                        pltpu.VMEM((2, rr, GDN_D), F32),
                        pltpu.VMEM((2, rr, GDN_D), BF16),
                        pltpu.VMEM((2, rr, rr), BF16),
                        pltpu.VMEM((2, rr, GDN_D), BF16),
                        pltpu.VMEM((2, rr, GDN_D), BF16)],
        compiler_params=_cparams("parallel"),
    )(qkvz, ba, conv_w, prm, gain)


def _attn_body(far_ref, scd_ref, scw_ref, qg_ref, kg_ref, lam_ref, sub_ref, nbt_ref, y_ref,
               qt_s, kn_s, vt_s, ysc_s, qc_s, m_s, l_s, acc_s, *, lambda_init):
    i = pl.program_id(1)
    s = scd_ref.shape[0]
    blk = ATT_BLK
    o_q, o_k, o_v = 3 * SC_WIDTH, 3 * SC_WIDTH + 256, 3 * SC_WIDTH + 512

    @pl.when(i == 0)
    def _prep():
        row = lax.broadcasted_iota(jnp.int32, (s, SC_WIDTH), 0)
        gate_b = scd_ref[:, 0:SC_WIDTH].astype(F32)
        x = scd_ref[:, SC_WIDTH:2 * SC_WIDTH].astype(F32) * scd_ref[:, 2 * SC_WIDTH:3 * SC_WIDTH].astype(F32)
        w = scw_ref[...]
        acc = x * w[SC_CONV - 1:SC_CONV, :]
        for j in range(SC_CONV - 1):
            sh = SC_CONV - 1 - j
            acc = acc + jnp.where(row >= sh, pltpu.roll(x, sh, 0), 0.0) * w[j:j + 1, :]
        ysc_s[...] = (gate_b * acc).astype(BF16)
        gi = lax.broadcasted_iota(jnp.int32, (256, 256), 0) // DIFF_DQK
        gj = lax.broadcasted_iota(jnp.int32, (256, 256), 1) // DIFF_DQK
        bd = (gi == gj).astype(BF16)

        def normed(off, g_ref, sc):
            xx = scd_ref[:, off:off + 256].astype(F32)
            hi, mid, lo = _split3(xx * xx)
            ss = _dot(hi, bd) + _dot(mid, bd) + _dot(lo, bd)
            return xx * lax.rsqrt(ss * (1.0 / DIFF_DQK) + EPS) * g_ref[...] * sc

        qt_s[...] = normed(o_q, qg_ref, DIFF_DQK ** -0.5).T.astype(BF16)
        kn_s[...] = normed(o_k, kg_ref, 1.0).astype(BF16)
        vt_s[...] = scd_ref[:, o_v:o_v + 256].astype(F32).T.astype(BF16)

    lp = lam_ref[...]
    lam = (jnp.exp(jnp.sum(lp[0:1, :] * lp[1:2, :], axis=-1, keepdims=True))
           - jnp.exp(jnp.sum(lp[2:3, :] * lp[3:4, :], axis=-1, keepdims=True)) + lambda_init)

    r0 = pl.multiple_of(i * blk, blk)
    rowi = lax.broadcasted_iota(jnp.int32, (LANES, blk), 0)
    grp = rowi // DIFF_DQK

    m_s[...] = jnp.full_like(m_s, NEG)
    l_s[...] = jnp.zeros_like(l_s)
    acc_s[...] = jnp.zeros_like(acc_s)
    for hp in range(2):
        qp = qt_s[hp * LANES:(hp + 1) * LANES, pl.ds(r0, blk)]
        for sidx in range(4):
            qc_s[hp, :, sidx * blk:(sidx + 1) * blk] = jnp.where(grp == sidx, qp, jnp.zeros_like(qp))

    def block(c0, bias_of):
        for hp in range(2):
            kp = kn_s[pl.ds(c0, blk), hp * LANES:(hp + 1) * LANES]
            vt = vt_s[hp * LANES:(hp + 1) * LANES, pl.ds(c0, blk)]
            logits = _dot(kp, qc_s[hp])
            b0, b1 = bias_of(2 * hp), bias_of(2 * hp + 1)
            if b0.ndim == 2:
                bias = jnp.concatenate([b0, b0, b1, b1], axis=1)
                logits = logits + bias
            else:
                two = 2 * blk
                logits = jnp.concatenate([logits[:, 0:two] + b0, logits[:, two:2 * two] + b1], axis=1)
            m_old = m_s[hp:hp + 1, :]
            m_new = jnp.maximum(m_old, jnp.max(logits, axis=0, keepdims=True))
            alpha = jnp.exp(m_old - m_new)
            p = jnp.exp(logits - m_new)
            l_s[hp:hp + 1, :] = alpha * l_s[hp:hp + 1, :] + jnp.sum(p, axis=0, keepdims=True)
            acc_s[hp] = alpha * acc_s[hp] + _dot(vt, p.astype(BF16))
            m_s[hp:hp + 1, :] = m_new

    def far_body(kb, carry):
        block(pl.multiple_of(kb * blk, blk), lambda head: far_ref[head])
        return carry

    lax.fori_loop(0, jnp.maximum(i - 1, 0), far_body, 0)

    @pl.when(i >= 1)
    def _prev():
        block(pl.multiple_of((i - 1) * blk, blk), lambda head: nbt_ref[head, 0:blk, :])

    block(r0, lambda head: nbt_ref[head, blk:2 * blk, :])

    low = rowi < DIFF_DV
    for hp in range(2):
        outs = []
        for hh in range(2):
            c0, c1 = (2 * hh) * blk, (2 * hh + 1) * blk
            outs.append(acc_s[hp, :, c0:c0 + blk] / l_s[hp:hp + 1, c0:c0 + blk]
                        - lam * (acc_s[hp, :, c1:c1 + blk] / l_s[hp:hp + 1, c1:c1 + blk]))
        o = jnp.where(low, outs[0], outs[1])
        sq = o * o
        ss0 = jnp.sum(jnp.where(low, sq, 0.0), axis=0, keepdims=True)
        ss1 = jnp.sum(jnp.where(low, 0.0, sq), axis=0, keepdims=True)
        ms = jnp.where(low, ss0, ss1) * (1.0 / DIFF_DV)
        y = (o * lax.rsqrt(ms + EPS)).T * sub_ref[...] * (1.0 - lambda_init)
        y_ref[:, SC_WIDTH + hp * LANES:SC_WIDTH + (hp + 1) * LANES] = y.astype(BF16)
    y_ref[:, 0:SC_WIDTH] = ysc_s[pl.ds(r0, blk), :]


def _attn(scd, sc_w, q_gain, k_gain, lam_p, sub_gain, near_bias_t, far_bias, lambda_init):
    b, s, _ = scd.shape
    blk = ATT_BLK
    const2 = lambda bi, i: (0, 0)
    const3 = lambda bi, i: (0, 0, 0)
    return pl.pallas_call(
        functools.partial(_attn_body, lambda_init=lambda_init),
        grid=(b, s // blk),
        in_specs=[pl.BlockSpec(memory_space=pltpu.SMEM),
                  pl.BlockSpec((None, s, W_SCD), lambda bi, i: (bi, 0, 0)),
                  pl.BlockSpec((SC_CONV, SC_WIDTH), const2),
                  pl.BlockSpec((1, 256), const2),
                  pl.BlockSpec((1, 256), const2),
                  pl.BlockSpec((4, DIFF_DQK), const2),
                  pl.BlockSpec((1, LANES), const2),
                  pl.BlockSpec((DIFF_HEADS, 2 * blk, blk), const3)],
        out_specs=pl.BlockSpec((None, blk, 512), lambda bi, i: (bi, i, 0)),
        out_shape=jax.ShapeDtypeStruct((b, s, 512), BF16),
        scratch_shapes=[pltpu.VMEM((256, s), BF16),
                        pltpu.VMEM((s, 256), BF16),
                        pltpu.VMEM((256, s), BF16),
                        pltpu.VMEM((s, SC_WIDTH), BF16),
                        pltpu.VMEM((2, LANES, 4 * blk), BF16),
                        pltpu.VMEM((2, 4 * blk), F32),
                        pltpu.VMEM((2, 4 * blk), F32),
                        pltpu.VMEM((2, LANES, 4 * blk), F32)],
        compiler_params=_cparams("parallel", "arbitrary"),
    )(far_bias, scd, sc_w, q_gain, k_gain, lam_p, sub_gain, near_bias_t)


def _rel_bucket(rel):
    max_exact = REL_BUCKETS // 2
    n = jnp.maximum(rel, 0)
    large = max_exact + (jnp.log(jnp.maximum(n, max_exact).astype(F32) / max_exact)
                         / math.log(REL_MAX_DIST / max_exact) * (REL_BUCKETS - max_exact)).astype(jnp.int32)
    large = jnp.minimum(large, REL_BUCKETS - 1)
    return jnp.where(n < max_exact, n, large)


def _bias_tables(rel_bias):
    blk = ATT_BLK
    rel = jnp.arange(blk)[:, None] + blk - jnp.arange(2 * blk)[None, :]
    onehot = (_rel_bucket(rel)[None] == jnp.arange(REL_BUCKETS)[:, None, None]).astype(F32)
    near = jnp.einsum("brc,bh->hrc", onehot, rel_bias.astype(F32), precision=lax.Precision.HIGHEST)
    near = jnp.where(rel[None] >= 0, near, NEG)
    return jnp.swapaxes(near, 1, 2), rel_bias[REL_BUCKETS - 1].astype(F32)


def _outproj_body(yg_ref, ya_ref, w_ref, h_ref, g_ref, h2_ref, xt_ref):
    hd = yg_ref.shape[1]
    h2 = h_ref[...] + _dot(yg_ref[...], w_ref[0:hd, :]) + _dot(ya_ref[...], w_ref[hd:, :])
    h2_ref[...] = h2
    n = h2 * lax.rsqrt(jnp.mean(h2 * h2, axis=-1, keepdims=True) + EPS) * g_ref[...]
    xt_ref[...] = n.T.astype(BF16)


def _outproj(yg, ya, w, h, gain):
    t, d = h.shape
    tm = min(TOKEN_BLK, t)
    return pl.pallas_call(
        _outproj_body,
        grid=(t // tm,),
        in_specs=[pl.BlockSpec((tm, yg.shape[1]), lambda i: (i, 0)),
                  pl.BlockSpec((tm, ya.shape[1]), lambda i: (i, 0)),
                  pl.BlockSpec((d, d), lambda i: (0, 0)),
                  pl.BlockSpec((tm, d), lambda i: (i, 0)),
                  pl.BlockSpec((1, d), lambda i: (0, 0))],
        out_specs=[pl.BlockSpec((tm, d), lambda i: (i, 0)),
                   pl.BlockSpec((None, d, tm), lambda i: (i, 0, 0))],
        out_shape=[jax.ShapeDtypeStruct((t, d), F32),
                   jax.ShapeDtypeStruct((t // tm, d, tm), BF16)],
        compiler_params=_cparams("parallel"),
    )(yg, ya, w, h, gain)


def _cmpx(lst, i, j):
    a, b = lst[i], lst[j]
    lst[i] = jnp.maximum(a, b)
    lst[j] = jnp.minimum(a, b)


def _bitonic_clean(lst, lo, n):
    d = n // 2
    while d >= 1:
        for k in range(n):
            if (k // d) % 2 == 0:
                _cmpx(lst, lo + k, lo + k + d)
        d //= 2


def _sort_desc(lst, lo, n):
    if n == 1:
        return
    h = n // 2
    _sort_desc(lst, lo, h)
    _sort_desc(lst, lo + h, h)
    for k in range(h):
        _cmpx(lst, lo + k, lo + n - 1 - k)
    _bitonic_clean(lst, lo, h)
    _bitonic_clean(lst, lo + h, h)


def _merge_sublanes(lst):
    n = len(lst)
    for d in (4, 2, 1):
        lst = [jnp.maximum(lst[k], pltpu.roll(lst[n - 1 - k], d, 0)) for k in range(n)]
        _bitonic_clean(lst, 0, n)
    return lst


def _top_sorted(s):
    lst = [s[k * 8:(k + 1) * 8, :] for k in range(s.shape[0] // 8)]
    _sort_desc(lst, 0, len(lst))
    return _merge_sublanes(lst)


def _count_prefix(lst, pred):
    n = len(lst)
    steps = []
    step = n // 2
    while step >= 1:
        steps.append(step)
        step //= 2

    def pick(bits, weights, index):
        if not bits:
            return lst[index]
        return jnp.where(bits[0], pick(bits[1:], weights[1:], index + weights[0]),
                         pick(bits[1:], weights[1:], index))

    bits = []
    for level, step in enumerate(steps):
        bits.append(pred(pick(bits, steps[:level], step - 1)))
    count = jnp.zeros_like(lst[0])
    for bit, step in zip(bits, steps):
        count = count + jnp.where(bit, float(step), 0.0)
    return jnp.where(pred(lst[n - 1]), float(n), count)


def _route_body(xt_ref, wqt_ref, keys_ref, cn_ref, e1_ref, rk_ref, e2_ref, q_s):
    k = PEER_TOPK
    tn = xt_ref.shape[1]
    q_s[...] = _dot(wqt_ref[...], xt_ref[...]).astype(BF16)
    sub = lax.broadcasted_iota(jnp.int32, (8, tn), 0)
    for hh in range(PEER_HEADS):
        sc = []
        for p in range(2):
            r = (hh * 2 + p) * PEER_DHALF
            sc.append(_dot(keys_ref[hh, p], q_s[r:r + PEER_DHALF, :]))
        a = _top_sorted(sc[0])
        b = _top_sorted(sc[1])
        apack, bpack = a[0], b[0]
        for r in range(1, 8):
            apack = jnp.where(sub == r, a[r], apack)
            bpack = jnp.where(sub == r, b[r], bpack)
        cand = [apack + b[i] for i in range(k)]
        extra = [a[8 + i] + bpack for i in range(k - 8)]
        for i in range(8, k):
            cand[i] = jnp.maximum(cand[i], extra[k - 1 - i])
        _bitonic_clean(cand, 0, k)
        best = _merge_sublanes(cand)
        z = jnp.zeros_like(best[0])
        for i in range(k):
            z = z + jnp.exp(best[i] - best[0])
        tau = best[k - 1]
        for r in range(PEER_KEYS // 8):
            rows = slice(r * 8, (r + 1) * 8)
            s1r, s2r = sc[0][rows, :], sc[1][rows, :]
            rk_ref[hh, rows, :] = _count_prefix(b, lambda t: t > s2r).astype(BF16)
            cn_ref[hh, rows, :] = _count_prefix(b, lambda t: s1r + t >= tau)
        e1_ref[hh] = jnp.exp(sc[0] - a[0][0:1, :])
        e2_ref[hh] = (jnp.exp(sc[1] - b[0][0:1, :]) * (RSQRT2 / z[0:1, :])).astype(BF16)


def _route(xt, wqt, keys):
    nb, d, tn = xt.shape
    nq = wqt.shape[0]
    spec = pl.BlockSpec((None, PEER_HEADS, PEER_KEYS, tn), lambda i: (i, 0, 0, 0))
    shp = jax.ShapeDtypeStruct((nb, PEER_HEADS, PEER_KEYS, tn), BF16)
    shp32 = jax.ShapeDtypeStruct((nb, PEER_HEADS, PEER_KEYS, tn), F32)
    return pl.pallas_call(
        _route_body,
        grid=(nb,),
        in_specs=[pl.BlockSpec((None, d, tn), lambda i: (i, 0, 0)),
                  pl.BlockSpec((nq, d), lambda i: (0, 0)),
                  pl.BlockSpec((PEER_HEADS, 2, PEER_KEYS, PEER_DHALF), lambda i: (0, 0, 0, 0))],
        out_specs=[spec, spec, spec, spec],
        out_shape=[shp32, shp32, shp, shp],
        scratch_shapes=[pltpu.VMEM((nq, tn), BF16)],
        compiler_params=_cparams("parallel"),
    )(xt, wqt, keys)


def _gelu_core(t):
    return t * (1.0 + lax.erf(t))


def _peer_body(xt_ref, u_ref, vt_ref, cn_ref, e1_ref, rk_ref, e2_ref, h2_ref, o_ref,
               hid_s, coef_s, acc_s, rk_s, e2_s):
    e = pl.program_id(1)
    eb, tn = hid_s.shape
    n_i = eb // PEER_KEYS

    @pl.when(e == 0)
    def _init():
        acc_s[...] = jnp.zeros_like(acc_s)
        rk_s[...] = rk_ref[...]
        e2_s[...] = e2_ref[...]

    hid_s[...] = _dot(u_ref[...], xt_ref[...])

    i0 = pl.multiple_of(e * n_i, n_i)
    for cb in range(tn // LANES):
        cs = slice(cb * LANES, (cb + 1) * LANES)
        for j in range(n_i):
            g = jnp.zeros((PEER_KEYS // BF16_ROWS, BF16_ROWS, LANES), BF16)
            for hh in range(PEER_HEADS):
                cnt = jnp.broadcast_to(cn_ref[hh, pl.ds(i0, n_i), cs][j:j + 1, :], (BF16_ROWS, LANES)).astype(BF16)
                ra = jnp.broadcast_to(e1_ref[hh, pl.ds(i0, n_i), cs][j:j + 1, :], (BF16_ROWS, LANES)).astype(BF16)
                e2 = e2_s[hh, :, :, cs]
                g = g + jnp.where(rk_s[hh, :, :, cs] < cnt[None], e2, jnp.zeros_like(e2)) * ra[None]
            rows = slice(j * PEER_KEYS, (j + 1) * PEER_KEYS)
            coef_s[rows, cs] = g.reshape(PEER_KEYS, LANES) * _gelu_core(hid_s[rows, cs]).astype(BF16)

    acc_s[...] += _dot(vt_ref[...], coef_s[...])

    @pl.when(e == pl.num_programs(1) - 1)
    def _fin():
        o_ref[...] = h2_ref[...] + acc_s[...].T


def _peer(xt, u, vt, cn, e1, rk, e2, h2):
    nb, d, tn = xt.shape
    n_blk, _, eb = vt.shape
    t = nb * tn
    hk = pl.BlockSpec((None, PEER_HEADS, PEER_KEYS, tn), lambda i, e: (i, 0, 0, 0))
    tiles = (PEER_HEADS, PEER_KEYS // BF16_ROWS, BF16_ROWS, tn)
    hk16 = pl.BlockSpec((None,) + tiles, lambda i, e: (i, 0, 0, 0, 0))
    rk = rk.reshape((nb,) + tiles)
    e2 = e2.reshape((nb,) + tiles)
    return pl.pallas_call(
        _peer_body,
        grid=(nb, n_blk),
        in_specs=[pl.BlockSpec((None, d, tn), lambda i, e: (i, 0, 0)),
                  pl.BlockSpec((eb, d), lambda i, e: (e, 0)),
                  pl.BlockSpec((None, d, eb), lambda i, e: (e, 0, 0)),
                  hk, hk, hk16, hk16,
                  pl.BlockSpec((tn, d), lambda i, e: (i, 0))],
        out_specs=pl.BlockSpec((tn, d), lambda i, e: (i, 0)),
        out_shape=jax.ShapeDtypeStruct((t, d), F32),
        scratch_shapes=[pltpu.VMEM((eb, tn), F32),
                        pltpu.VMEM((eb, tn), BF16),
                        pltpu.VMEM((d, tn), F32),
                        pltpu.VMEM(tiles, BF16),
                        pltpu.VMEM(tiles, BF16)],
        compiler_params=_cparams("parallel", "arbitrary"),
    )(xt, u, vt, cn, e1, rk, e2, h2)


def _pad_lanes(v, offset):
    return jnp.zeros((1, LANES), F32).at[0, offset:offset + v.shape[0]].set(v.astype(F32))


def _layer(h, l, near_bias, far_bias, attn_norm, w_in, gdn_conv, gdn_a_log, gdn_dt_bias, gdn_out_norm,
           sc_conv, diff_q_norm, diff_k_norm, diff_lambda, diff_subln, w_out, ffn_norm,
           peer_wq, peer_keys, peer_u, peer_v, batch):
    t, d = h.shape
    s = t // batch
    lambda_init = 0.8 - 0.6 * math.exp(-0.3 * l)
    n_main = W_QKVZ + W_SCD
    wi = w_in[l]
    w_r = jnp.concatenate([wi[:, 0:W_QKVZ], wi[:, W_QKVZ + 2 * GDN_HEADS:], wi[:, W_QKVZ:W_QKVZ + 2 * GDN_HEADS],
                           jnp.zeros((d, W_BA - 2 * GDN_HEADS), wi.dtype)], axis=1).astype(BF16)
    assert w_r.shape[1] == n_main + W_BA
    qkvz, scd, ba = _inproj(h, attn_norm[l][None, :], w_r)

    prm = jnp.concatenate([_pad_lanes(gdn_a_log[l], GDN_HEADS), _pad_lanes(gdn_dt_bias[l], GDN_HEADS),
                           jnp.zeros((6, LANES), F32)], axis=0)
    y_gdn = _gdn(qkvz.reshape(batch, s, W_QKVZ), ba.reshape(batch, s, W_BA), gdn_conv[l].astype(F32), prm,
                 gdn_out_norm[l][None, :].astype(F32))

    y_att = _attn(scd.reshape(batch, s, W_SCD), sc_conv[l].astype(F32),
                  jnp.tile(diff_q_norm[l], 256 // DIFF_DQK)[None, :].astype(F32),
                  jnp.tile(diff_k_norm[l], 256 // DIFF_DQK)[None, :].astype(F32),
                  diff_lambda[l].astype(F32),
                  jnp.tile(diff_subln[l], LANES // DIFF_DV)[None, :].astype(F32),
                  near_bias, far_bias, lambda_init)

    h2, xt = _outproj(y_gdn.reshape(t, -1), y_att.reshape(t, -1), w_out[l].astype(BF16), h, ffn_norm[l][None, :])

    cn, e1, rk, e2 = _route(xt, peer_wq[l].T.astype(BF16), peer_keys[l].astype(BF16))
    n_exp = peer_v.shape[1]
    vt = peer_v[l].reshape(n_exp // PEER_EXPERT_BLK, PEER_EXPERT_BLK, d).transpose(0, 2, 1).astype(BF16)
    return _peer(xt, (peer_u[l] * RSQRT2).astype(BF16), vt, cn, e1, rk, e2, h2)


def kernel(x, rel_bias, attn_norm, w_in, gdn_conv, gdn_a_log, gdn_dt_bias, gdn_out_norm, sc_conv,
           diff_q_norm, diff_k_norm, diff_lambda, diff_subln, w_out, ffn_norm, peer_wq, peer_keys,
           peer_u, peer_v):
    batch, s, d = x.shape
    near_bias, far_bias = _bias_tables(rel_bias)
    h = x.reshape(batch * s, d)
    for l in range(w_in.shape[0]):
        h = _layer(h, l, near_bias, far_bias, attn_norm, w_in, gdn_conv, gdn_a_log, gdn_dt_bias,
                   gdn_out_norm, sc_conv, diff_q_norm, diff_k_norm, diff_lambda, diff_subln, w_out,
                   ffn_norm, peer_wq, peer_keys, peer_u, peer_v, batch)
    return h.reshape(batch, s, d)
```

```python
import functools
import math

import jax
import jax.numpy as jnp
from jax import lax
from jax.experimental import pallas as pl
from jax.experimental.pallas import tpu as pltpu

F32 = jnp.float32
BF16 = jnp.bfloat16
EPS = 1e-6
RSQRT2 = 0.5 ** 0.5
NEG = -1e30

D_MODEL = 1024
GDN_HEADS = 4
GDN_D = 128
GDN_CONV = 4
GDN_CHUNK = 64
SC_WIDTH = 256
SC_CONV = 3
DIFF_HEADS = 4
DIFF_DV = 64
DIFF_DQK = 32
ATT_BLK = 256
REL_BUCKETS = 32
REL_MAX_DIST = 128
PEER_HEADS = 8
PEER_KEYS = 128
PEER_TOPK = 16
PEER_DHALF = 128
LANES = 128
BF16_ROWS = 16
TOKEN_BLK = 512
PEER_EXPERT_BLK = 2048
VMEM_LIMIT = 56 * 1024 * 1024

W_QKVZ = 2048
W_SCD = 1536
W_BA = LANES


def _cparams(*sem):
    return pltpu.CompilerParams(dimension_semantics=sem, vmem_limit_bytes=VMEM_LIMIT)


def _nt_dot(a, b):
    return lax.dot_general(a, b, (((1,), (1,)), ((), ())), preferred_element_type=F32)


def _tn_dot(a, b):
    return lax.dot_general(a, b, (((0,), (0,)), ((), ())), preferred_element_type=F32)


def _dot(a, b):
    return jnp.dot(a, b, preferred_element_type=F32)


def _split3(x):
    hi = x.astype(BF16)
    r = x - hi.astype(F32)
    mid = r.astype(BF16)
    lo = (r - mid.astype(F32)).astype(BF16)
    return hi, mid, lo


def _inproj_body(h_ref, g_ref, w_ref, qkvz_ref, scd_ref, ba_ref):
    x = h_ref[...]
    n = x * lax.rsqrt(jnp.mean(x * x, axis=-1, keepdims=True) + EPS) * g_ref[...]
    nb = n.astype(BF16)
    qkvz_ref[...] = _dot(nb, w_ref[:, 0:W_QKVZ]).astype(BF16)
    scd_ref[...] = _dot(nb, w_ref[:, W_QKVZ:W_QKVZ + W_SCD]).astype(BF16)
    ba_ref[...] = _dot(nb, w_ref[:, W_QKVZ + W_SCD:])


def _inproj(h, gain, w):
    t, d = h.shape
    tm = min(TOKEN_BLK, t)
    nw = w.shape[1]
    return pl.pallas_call(
        _inproj_body,
        grid=(t // tm,),
        in_specs=[pl.BlockSpec((tm, d), lambda i: (i, 0)),
                  pl.BlockSpec((1, d), lambda i: (0, 0)),
                  pl.BlockSpec((d, nw), lambda i: (0, 0))],
        out_specs=[pl.BlockSpec((tm, W_QKVZ), lambda i: (i, 0)),
                   pl.BlockSpec((tm, W_SCD), lambda i: (i, 0)),
                   pl.BlockSpec((tm, W_BA), lambda i: (i, 0))],
        out_shape=[jax.ShapeDtypeStruct((t, W_QKVZ), BF16),
                   jax.ShapeDtypeStruct((t, W_SCD), BF16),
                   jax.ShapeDtypeStruct((t, W_BA), F32)],
        compiler_params=_cparams("parallel"),
    )(h, gain, w)


def _gdn_body(qkvz_ref, ba_ref, conv_ref, prm_ref, gain_ref, y_ref,
              q_s, k_s, kb_s, vb_s, gb_s, o_s, st_s, u_b, w_b, a_b, qg_b, kd_b):
    s = qkvz_ref.shape[0]
    c_sz, nh = GDN_CHUNK, GDN_HEADS
    nc, rr = s // c_sz, GDN_HEADS * GDN_CHUNK
    row = lax.broadcasted_iota(jnp.int32, (s, LANES), 0)

    ba = ba_ref[...]
    beta = jax.nn.sigmoid(ba)
    xg = ba + prm_ref[1:2, :]
    softplus = jnp.maximum(xg, 0.0) + jnp.log(1.0 + jnp.exp(-jnp.abs(xg)))
    g = -jnp.exp(prm_ref[0:1, :]) * softplus
    pos = row % c_sz
    for sh in (1, 2, 4, 8, 16, 32):
        g = g + jnp.where(pos >= sh, pltpu.roll(g, sh, 0), 0.0)

    def chunked(x):
        return x.reshape(nc, c_sz, LANES)

    for hh in range(nh):
        gb_s[:, hh] = chunked(jnp.broadcast_to(g[:, nh + hh:nh + hh + 1], (s, LANES)))

    scale = GDN_D ** -0.5
    for cb in range(3 * nh):
        x = qkvz_ref[:, cb * LANES:(cb + 1) * LANES].astype(F32)
        w = conv_ref[:, cb * LANES:(cb + 1) * LANES]
        acc = x * w[GDN_CONV - 1:GDN_CONV, :]
        for j in range(GDN_CONV - 1):
            sh = GDN_CONV - 1 - j
            acc = acc + jnp.where(row >= sh, pltpu.roll(x, sh, 0), 0.0) * w[j:j + 1, :]
        y = acc * jax.nn.sigmoid(acc)
        kind, hh = divmod(cb, nh)
        if kind < 2:
            y = y * lax.rsqrt(jnp.sum(y * y, axis=-1, keepdims=True) + EPS)
        if kind == 0:
            q_s[:, hh] = chunked(y * scale)
        elif kind == 1:
            k_s[:, hh] = chunked(y)
            kb_s[:, hh] = chunked(y * beta[:, hh:hh + 1])
        else:
            vb_s[:, hh] = chunked((y * beta[:, hh:hh + 1]).astype(BF16))

    st_s[...] = jnp.zeros_like(st_s)
    ii = lax.broadcasted_iota(jnp.int32, (rr, rr), 0)
    jj = lax.broadcasted_iota(jnp.int32, (rr, rr), 1)
    same_head = (ii // c_sz) == (jj // c_sz)
    tril = jnp.logical_and(same_head, ii >= jj)
    strict = jnp.logical_and(same_head, ii > jj)
    eye = (ii == jj).astype(F32)
    lane = lax.broadcasted_iota(jnp.int32, (rr, LANES), 1)
    pick3 = (lane < 3).astype(BF16)
    row_head = lax.broadcasted_iota(jnp.int32, (rr, LANES), 0) // c_sz

    def stacked(ref, c):
        return ref[c].reshape(rr, LANES)

    def phase_a(c, slot):
        gc = stacked(gb_s, c)
        eg = jnp.exp(gc)
        qc, kc, kb = stacked(q_s, c), stacked(k_s, c), stacked(kb_s, c)
        glast = jnp.broadcast_to(gb_s[c][:, c_sz - 1:c_sz, :], (nh, c_sz, LANES)).reshape(rr, LANES)
        hi, mid, lo = _split3(gc)
        x3 = jnp.where(lane == 0, hi, jnp.where(lane == 1, mid, jnp.where(lane == 2, lo, jnp.zeros_like(lo))))
        grow = _nt_dot(pick3, x3)
        decay = jnp.where(tril, jnp.exp(jnp.minimum(jnp.concatenate([gc, gc], axis=1) - grow, 0.0)), 0.0)
        kq = _nt_dot(jnp.concatenate([kb.astype(BF16), qc.astype(BF16)], axis=0), kc.astype(BF16))
        lower = jnp.where(strict, kq[0:rr] * decay, 0.0)
        a_in = jnp.where(tril, kq[rr:2 * rr] * decay, 0.0)
        pw = -lower
        tm = eye + pw
        pw = _dot(pw.astype(BF16), pw.astype(BF16))
        for _ in range(4):
            pwb = pw.astype(BF16)
            both = _dot(jnp.concatenate([tm.astype(BF16), pwb], axis=0), pwb)
            tm = tm + both[0:rr]
            pw = both[rr:2 * rr]
        tm = tm + _dot(tm.astype(BF16), pw.astype(BF16))
        uw = _dot(tm.astype(BF16), jnp.concatenate([stacked(vb_s, c), (kb * eg).astype(BF16)], axis=1))
        u_b[slot] = uw[:, 0:LANES]
        w_b[slot] = uw[:, LANES:2 * LANES].astype(BF16)
        a_b[slot] = a_in.astype(BF16)
        qg_b[slot] = (qc * eg).astype(BF16)
        kd_b[slot] = (kc * jnp.exp(glast - gc)).astype(BF16)

    def phase_b(c, slot):
        stb = st_s[...].astype(BF16)
        wq = _dot(jnp.concatenate([w_b[slot], qg_b[slot]], axis=0), stb)
        u = u_b[slot]

        def head_blocks(r0):
            return jnp.concatenate([wq[r0 + hh * c_sz:r0 + (hh + 1) * c_sz, hh * LANES:(hh + 1) * LANES]
                                    for hh in range(nh)], axis=0)

        vnb = (u - head_blocks(0)).astype(BF16)
        o_s[c] = (head_blocks(rr) + _dot(a_b[slot], vnb)).reshape(nh, c_sz, LANES)
        vbd = jnp.concatenate([jnp.where(row_head == hh, vnb, jnp.zeros_like(vnb)) for hh in range(nh)], axis=1)
        egl = jnp.concatenate([jnp.exp(gb_s[c][hh, c_sz - 1:c_sz, :]) for hh in range(nh)], axis=1)
        st_s[...] = st_s[...] * egl + _tn_dot(kd_b[slot], vbd)

    phase_a(0, 0)

    def pair(kk, carry):
        c = 2 * kk
        phase_b(c, 0)
        phase_a(c + 1, 1)
        phase_b(c + 1, 1)
        phase_a(jnp.minimum(c + 2, nc - 1), 0)
        return carry

    lax.fori_loop(0, nc // 2, pair, 0)

    gain = gain_ref[...]
    for hh in range(nh):
        o = o_s[:, hh].reshape(s, LANES)
        z = qkvz_ref[:, (3 * nh + hh) * LANES:(3 * nh + hh + 1) * LANES].astype(F32)
        on = o * lax.rsqrt(jnp.mean(o * o, axis=-1, keepdims=True) + EPS) * gain
        y_ref[:, hh * LANES:(hh + 1) * LANES] = (on * (z * jax.nn.sigmoid(z))).astype(BF16)


def _gdn(qkvz, ba, conv_w, prm, gain):
    b, s, _ = qkvz.shape
    hd = GDN_HEADS * GDN_D
    nc, rr = s // GDN_CHUNK, GDN_HEADS * GDN_CHUNK
    per_chunk = (nc, GDN_HEADS, GDN_CHUNK, GDN_D)
    return pl.pallas_call(
        _gdn_body,
        grid=(b,),
        in_specs=[pl.BlockSpec((None, s, W_QKVZ), lambda i: (i, 0, 0)),
                  pl.BlockSpec((None, s, W_BA), lambda i: (i, 0, 0)),
                  pl.BlockSpec((GDN_CONV, 3 * hd), lambda i: (0, 0)),
                  pl.BlockSpec((8, LANES), lambda i: (0, 0)),
                  pl.BlockSpec((1, GDN_D), lambda i: (0, 0))],
        out_specs=pl.BlockSpec((None, s, hd), lambda i: (i, 0, 0)),
        out_shape=jax.ShapeDtypeStruct((b, s, hd), BF16),
        scratch_shapes=[pltpu.VMEM(per_chunk, F32),
                        pltpu.VMEM(per_chunk, F32),
                        pltpu.VMEM(per_chunk, F32),
                        pltpu.VMEM(per_chunk, BF16),
                        pltpu.VMEM(per_chunk, F32),
                        pltpu.VMEM(per_chunk, F32),
                        pltpu.VMEM((GDN_D, hd), F32),
                        pltpu.VMEM((2, rr, GDN_D), F32),
                        pltpu.VMEM((2, rr, GDN_D), BF16),
                        pltpu.VMEM((2, rr, rr), BF16),
                        pltpu.VMEM((2, rr, GDN_D), BF16),
                        pltpu.VMEM((2, rr, GDN_D), BF16)],
        compiler_params=_cparams("parallel"),
    )(qkvz, ba, conv_w, prm, gain)


def _attn_body(far_ref, scd_ref, scw_ref, qg_ref, kg_ref, lam_ref, sub_ref, nbt_ref, y_ref,
               qt_s, kn_s, vt_s, ysc_s, qc_s, m_s, l_s, acc_s, *, lambda_init):
    i = pl.program_id(1)
    s = scd_ref.shape[0]
    blk = ATT_BLK
    o_q, o_k, o_v = 3 * SC_WIDTH, 3 * SC_WIDTH + 256, 3 * SC_WIDTH + 512

    @pl.when(i == 0)
    def _prep():
        row = lax.broadcasted_iota(jnp.int32, (s, SC_WIDTH), 0)
        gate_b = scd_ref[:, 0:SC_WIDTH].astype(F32)
        x = scd_ref[:, SC_WIDTH:2 * SC_WIDTH].astype(F32) * scd_ref[:, 2 * SC_WIDTH:3 * SC_WIDTH].astype(F32)
        w = scw_ref[...]
        acc = x * w[SC_CONV - 1:SC_CONV, :]
        for j in range(SC_CONV - 1):
            sh = SC_CONV - 1 - j
            acc = acc + jnp.where(row >= sh, pltpu.roll(x, sh, 0), 0.0) * w[j:j + 1, :]
        ysc_s[...] = (gate_b * acc).astype(BF16)
        gi = lax.broadcasted_iota(jnp.int32, (256, 256), 0) // DIFF_DQK
        gj = lax.broadcasted_iota(jnp.int32, (256, 256), 1) // DIFF_DQK
        bd = (gi == gj).astype(BF16)

        def normed(off, g_ref, sc):
            xx = scd_ref[:, off:off + 256].astype(F32)
            hi, mid, lo = _split3(xx * xx)
            ss = _dot(hi, bd) + _dot(mid, bd) + _dot(lo, bd)
            return xx * lax.rsqrt(ss * (1.0 / DIFF_DQK) + EPS) * g_ref[...] * sc

        qt_s[...] = normed(o_q, qg_ref, DIFF_DQK ** -0.5).T.astype(BF16)
        kn_s[...] = normed(o_k, kg_ref, 1.0).astype(BF16)
        vt_s[...] = scd_ref[:, o_v:o_v + 256].astype(F32).T.astype(BF16)

    lp = lam_ref[...]
    lam = (jnp.exp(jnp.sum(lp[0:1, :] * lp[1:2, :], axis=-1, keepdims=True))
           - jnp.exp(jnp.sum(lp[2:3, :] * lp[3:4, :], axis=-1, keepdims=True)) + lambda_init)

    r0 = pl.multiple_of(i * blk, blk)
    rowi = lax.broadcasted_iota(jnp.int32, (LANES, blk), 0)
    grp = rowi // DIFF_DQK

    m_s[...] = jnp.full_like(m_s, NEG)
    l_s[...] = jnp.zeros_like(l_s)
    acc_s[...] = jnp.zeros_like(acc_s)
    for hp in range(2):
        qp = qt_s[hp * LANES:(hp + 1) * LANES, pl.ds(r0, blk)]
        for sidx in range(4):
            qc_s[hp, :, sidx * blk:(sidx + 1) * blk] = jnp.where(grp == sidx, qp, jnp.zeros_like(qp))

    def block(c0, bias_of):
        for hp in range(2):
            kp = kn_s[pl.ds(c0, blk), hp * LANES:(hp + 1) * LANES]
            vt = vt_s[hp * LANES:(hp + 1) * LANES, pl.ds(c0, blk)]
            logits = _dot(kp, qc_s[hp])
            b0, b1 = bias_of(2 * hp), bias_of(2 * hp + 1)
            if b0.ndim == 2:
                bias = jnp.concatenate([b0, b0, b1, b1], axis=1)
                logits = logits + bias
            else:
                two = 2 * blk
                logits = jnp.concatenate([logits[:, 0:two] + b0, logits[:, two:2 * two] + b1], axis=1)
            m_old = m_s[hp:hp + 1, :]
            m_new = jnp.maximum(m_old, jnp.max(logits, axis=0, keepdims=True))
            alpha = jnp.exp(m_old - m_new)
            p = jnp.exp(logits - m_new)
            l_s[hp:hp + 1, :] = alpha * l_s[hp:hp + 1, :] + jnp.sum(p, axis=0, keepdims=True)
            acc_s[hp] = alpha * acc_s[hp] + _dot(vt, p.astype(BF16))
            m_s[hp:hp + 1, :] = m_new

    def far_body(kb, carry):
        block(pl.multiple_of(kb * blk, blk), lambda head: far_ref[head])
        return carry

    lax.fori_loop(0, jnp.maximum(i - 1, 0), far_body, 0)

    @pl.when(i >= 1)
    def _prev():
        block(pl.multiple_of((i - 1) * blk, blk), lambda head: nbt_ref[head, 0:blk, :])

    block(r0, lambda head: nbt_ref[head, blk:2 * blk, :])

    low = rowi < DIFF_DV
    for hp in range(2):
        outs = []
        for hh in range(2):
            c0, c1 = (2 * hh) * blk, (2 * hh + 1) * blk
            outs.append(acc_s[hp, :, c0:c0 + blk] / l_s[hp:hp + 1, c0:c0 + blk]
                        - lam * (acc_s[hp, :, c1:c1 + blk] / l_s[hp:hp + 1, c1:c1 + blk]))
        o = jnp.where(low, outs[0], outs[1])
        sq = o * o
        ss0 = jnp.sum(jnp.where(low, sq, 0.0), axis=0, keepdims=True)
        ss1 = jnp.sum(jnp.where(low, 0.0, sq), axis=0, keepdims=True)
        ms = jnp.where(low, ss0, ss1) * (1.0 / DIFF_DV)
        y = (o * lax.rsqrt(ms + EPS)).T * sub_ref[...] * (1.0 - lambda_init)
        y_ref[:, SC_WIDTH + hp * LANES:SC_WIDTH + (hp + 1) * LANES] = y.astype(BF16)
    y_ref[:, 0:SC_WIDTH] = ysc_s[pl.ds(r0, blk), :]


def _attn(scd, sc_w, q_gain, k_gain, lam_p, sub_gain, near_bias_t, far_bias, lambda_init):
    b, s, _ = scd.shape
    blk = ATT_BLK
    const2 = lambda bi, i: (0, 0)
    const3 = lambda bi, i: (0, 0, 0)
    return pl.pallas_call(
        functools.partial(_attn_body, lambda_init=lambda_init),
        grid=(b, s // blk),
        in_specs=[pl.BlockSpec(memory_space=pltpu.SMEM),
                  pl.BlockSpec((None, s, W_SCD), lambda bi, i: (bi, 0, 0)),
                  pl.BlockSpec((SC_CONV, SC_WIDTH), const2),
                  pl.BlockSpec((1, 256), const2),
                  pl.BlockSpec((1, 256), const2),
                  pl.BlockSpec((4, DIFF_DQK), const2),
                  pl.BlockSpec((1, LANES), const2),
                  pl.BlockSpec((DIFF_HEADS, 2 * blk, blk), const3)],
        out_specs=pl.BlockSpec((None, blk, 512), lambda bi, i: (bi, i, 0)),
        out_shape=jax.ShapeDtypeStruct((b, s, 512), BF16),
        scratch_shapes=[pltpu.VMEM((256, s), BF16),
                        pltpu.VMEM((s, 256), BF16),
                        pltpu.VMEM((256, s), BF16),
                        pltpu.VMEM((s, SC_WIDTH), BF16),
                        pltpu.VMEM((2, LANES, 4 * blk), BF16),
                        pltpu.VMEM((2, 4 * blk), F32),
                        pltpu.VMEM((2, 4 * blk), F32),
                        pltpu.VMEM((2, LANES, 4 * blk), F32)],
        compiler_params=_cparams("parallel", "arbitrary"),
    )(far_bias, scd, sc_w, q_gain, k_gain, lam_p, sub_gain, near_bias_t)


def _rel_bucket(rel):
    max_exact = REL_BUCKETS // 2
    n = jnp.maximum(rel, 0)
    large = max_exact + (jnp.log(jnp.maximum(n, max_exact).astype(F32) / max_exact)
                         / math.log(REL_MAX_DIST / max_exact) * (REL_BUCKETS - max_exact)).astype(jnp.int32)
    large = jnp.minimum(large, REL_BUCKETS - 1)
    return jnp.where(n < max_exact, n, large)


def _bias_tables(rel_bias):
    blk = ATT_BLK
    rel = jnp.arange(blk)[:, None] + blk - jnp.arange(2 * blk)[None, :]
    onehot = (_rel_bucket(rel)[None] == jnp.arange(REL_BUCKETS)[:, None, None]).astype(F32)
    near = jnp.einsum("brc,bh->hrc", onehot, rel_bias.astype(F32), precision=lax.Precision.HIGHEST)
    near = jnp.where(rel[None] >= 0, near, NEG)
    return jnp.swapaxes(near, 1, 2), rel_bias[REL_BUCKETS - 1].astype(F32)


def _outproj_body(yg_ref, ya_ref, w_ref, h_ref, g_ref, h2_ref, xt_ref):
    hd = yg_ref.shape[1]
    h2 = h_ref[...] + _dot(yg_ref[...], w_ref[0:hd, :]) + _dot(ya_ref[...], w_ref[hd:, :])
    h2_ref[...] = h2
    n = h2 * lax.rsqrt(jnp.mean(h2 * h2, axis=-1, keepdims=True) + EPS) * g_ref[...]
    xt_ref[...] = n.T.astype(BF16)


def _outproj(yg, ya, w, h, gain):
    t, d = h.shape
    tm = min(TOKEN_BLK, t)
    return pl.pallas_call(
        _outproj_body,
        grid=(t // tm,),
        in_specs=[pl.BlockSpec((tm, yg.shape[1]), lambda i: (i, 0)),
                  pl.BlockSpec((tm, ya.shape[1]), lambda i: (i, 0)),
                  pl.BlockSpec((d, d), lambda i: (0, 0)),
                  pl.BlockSpec((tm, d), lambda i: (i, 0)),
                  pl.BlockSpec((1, d), lambda i: (0, 0))],
        out_specs=[pl.BlockSpec((tm, d), lambda i: (i, 0)),
                   pl.BlockSpec((None, d, tm), lambda i: (i, 0, 0))],
        out_shape=[jax.ShapeDtypeStruct((t, d), F32),
                   jax.ShapeDtypeStruct((t // tm, d, tm), BF16)],
        compiler_params=_cparams("parallel"),
    )(yg, ya, w, h, gain)


def _cmpx(lst, i, j):
    a, b = lst[i], lst[j]
    lst[i] = jnp.maximum(a, b)
    lst[j] = jnp.minimum(a, b)


def _bitonic_clean(lst, lo, n):
    d = n // 2
    while d >= 1:
        for k in range(n):
            if (k // d) % 2 == 0:
                _cmpx(lst, lo + k, lo + k + d)
        d //= 2


def _sort_desc(lst, lo, n):
    if n == 1:
        return
    h = n // 2
    _sort_desc(lst, lo, h)
    _sort_desc(lst, lo + h, h)
    for k in range(h):
        _cmpx(lst, lo + k, lo + n - 1 - k)
    _bitonic_clean(lst, lo, h)
    _bitonic_clean(lst, lo + h, h)


def _merge_sublanes(lst):
    n = len(lst)
    for d in (4, 2, 1):
        lst = [jnp.maximum(lst[k], pltpu.roll(lst[n - 1 - k], d, 0)) for k in range(n)]
        _bitonic_clean(lst, 0, n)
    return lst


def _top_sorted(s):
    lst = [s[k * 8:(k + 1) * 8, :] for k in range(s.shape[0] // 8)]
    _sort_desc(lst, 0, len(lst))
    return _merge_sublanes(lst)


def _count_prefix(lst, pred):
    n = len(lst)
    steps = []
    step = n // 2
    while step >= 1:
        steps.append(step)
        step //= 2

    def pick(bits, weights, index):
        if not bits:
            return lst[index]
        return jnp.where(bits[0], pick(bits[1:], weights[1:], index + weights[0]),
                         pick(bits[1:], weights[1:], index))

    bits = []
    for level, step in enumerate(steps):
        bits.append(pred(pick(bits, steps[:level], step - 1)))
    count = jnp.zeros_like(lst[0])
    for bit, step in zip(bits, steps):
        count = count + jnp.where(bit, float(step), 0.0)
    return jnp.where(pred(lst[n - 1]), float(n), count)


def _route_body(xt_ref, wqt_ref, keys_ref, cn_ref, e1_ref, rk_ref, e2_ref, q_s):
    k = PEER_TOPK
    tn = xt_ref.shape[1]
    q_s[...] = _dot(wqt_ref[...], xt_ref[...]).astype(BF16)
    sub = lax.broadcasted_iota(jnp.int32, (8, tn), 0)
    for hh in range(PEER_HEADS):
        sc = []
        for p in range(2):
            r = (hh * 2 + p) * PEER_DHALF
            sc.append(_dot(keys_ref[hh, p], q_s[r:r + PEER_DHALF, :]))
        a = _top_sorted(sc[0])
        b = _top_sorted(sc[1])
        apack, bpack = a[0], b[0]
        for r in range(1, 8):
            apack = jnp.where(sub == r, a[r], apack)
            bpack = jnp.where(sub == r, b[r], bpack)
        cand = [apack + b[i] for i in range(k)]
        extra = [a[8 + i] + bpack for i in range(k - 8)]
        for i in range(8, k):
            cand[i] = jnp.maximum(cand[i], extra[k - 1 - i])
        _bitonic_clean(cand, 0, k)
        best = _merge_sublanes(cand)
        z = jnp.zeros_like(best[0])
        for i in range(k):
            z = z + jnp.exp(best[i] - best[0])
        tau = best[k - 1]
        for r in range(PEER_KEYS // 8):
            rows = slice(r * 8, (r + 1) * 8)
            s1r, s2r = sc[0][rows, :], sc[1][rows, :]
            rk_ref[hh, rows, :] = _count_prefix(b, lambda t: t > s2r).astype(BF16)
            cn_ref[hh, rows, :] = _count_prefix(b, lambda t: s1r + t >= tau)
        e1_ref[hh] = jnp.exp(sc[0] - a[0][0:1, :])
        e2_ref[hh] = (jnp.exp(sc[1] - b[0][0:1, :]) * (RSQRT2 / z[0:1, :])).astype(BF16)


def _route(xt, wqt, keys):
    nb, d, tn = xt.shape
    nq = wqt.shape[0]
    spec = pl.BlockSpec((None, PEER_HEADS, PEER_KEYS, tn), lambda i: (i, 0, 0, 0))
    shp = jax.ShapeDtypeStruct((nb, PEER_HEADS, PEER_KEYS, tn), BF16)
    shp32 = jax.ShapeDtypeStruct((nb, PEER_HEADS, PEER_KEYS, tn), F32)
    return pl.pallas_call(
        _route_body,
        grid=(nb,),
        in_specs=[pl.BlockSpec((None, d, tn), lambda i: (i, 0, 0)),
                  pl.BlockSpec((nq, d), lambda i: (0, 0)),
                  pl.BlockSpec((PEER_HEADS, 2, PEER_KEYS, PEER_DHALF), lambda i: (0, 0, 0, 0))],
        out_specs=[spec, spec, spec, spec],
        out_shape=[shp32, shp32, shp, shp],
        scratch_shapes=[pltpu.VMEM((nq, tn), BF16)],
        compiler_params=_cparams("parallel"),
    )(xt, wqt, keys)


def _gelu_core(t):
    return t * (1.0 + lax.erf(t))


def _peer_body(xt_ref, u_ref, vt_ref, cn_ref, e1_ref, rk_ref, e2_ref, h2_ref, o_ref,
               hid_s, coef_s, acc_s, rk_s, e2_s):
    e = pl.program_id(1)
    eb, tn = hid_s.shape
    n_i = eb // PEER_KEYS

    @pl.when(e == 0)
    def _init():
        acc_s[...] = jnp.zeros_like(acc_s)
        rk_s[...] = rk_ref[...]
        e2_s[...] = e2_ref[...]

    hid_s[...] = _dot(u_ref[...], xt_ref[...])

    i0 = pl.multiple_of(e * n_i, n_i)
    for cb in range(tn // LANES):
        cs = slice(cb * LANES, (cb + 1) * LANES)
        for j in range(n_i):
            g = None
            for hh in range(PEER_HEADS):
                cnt = jnp.broadcast_to(cn_ref[hh, pl.ds(i0, n_i), cs][j:j + 1, :], (BF16_ROWS, LANES)).astype(BF16)
                ra = jnp.broadcast_to(e1_ref[hh, pl.ds(i0, n_i), cs][j:j + 1, :], (BF16_ROWS, LANES)).astype(BF16)
                e2 = e2_s[hh, :, :, cs]
                term = jnp.where(rk_s[hh, :, :, cs] < cnt[None], e2, jnp.zeros_like(e2)) * ra[None]
                g = term if g is None else g + term
            rows = slice(j * PEER_KEYS, (j + 1) * PEER_KEYS)
            coef_s[rows, cs] = g.reshape(PEER_KEYS, LANES) * _gelu_core(hid_s[rows, cs]).astype(BF16)

    acc_s[...] += _dot(vt_ref[...], coef_s[...])

    @pl.when(e == pl.num_programs(1) - 1)
    def _fin():
        o_ref[...] = h2_ref[...] + acc_s[...].T


def _peer(xt, u, vt, cn, e1, rk, e2, h2):
    nb, d, tn = xt.shape
    n_blk, _, eb = vt.shape
    t = nb * tn
    hk = pl.BlockSpec((None, PEER_HEADS, PEER_KEYS, tn), lambda i, e: (i, 0, 0, 0))
    tiles = (PEER_HEADS, PEER_KEYS // BF16_ROWS, BF16_ROWS, tn)
    hk16 = pl.BlockSpec((None,) + tiles, lambda i, e: (i, 0, 0, 0, 0))
    rk = rk.reshape((nb,) + tiles)
    e2 = e2.reshape((nb,) + tiles)
    return pl.pallas_call(
        _peer_body,
        grid=(nb, n_blk),
        in_specs=[pl.BlockSpec((None, d, tn), lambda i, e: (i, 0, 0)),
                  pl.BlockSpec((eb, d), lambda i, e: (e, 0)),
                  pl.BlockSpec((None, d, eb), lambda i, e: (e, 0, 0)),
                  hk, hk, hk16, hk16,
                  pl.BlockSpec((tn, d), lambda i, e: (i, 0))],
        out_specs=pl.BlockSpec((tn, d), lambda i, e: (i, 0)),
        out_shape=jax.ShapeDtypeStruct((t, d), F32),
        scratch_shapes=[pltpu.VMEM((eb, tn), F32),
                        pltpu.VMEM((eb, tn), BF16),
                        pltpu.VMEM((d, tn), F32),
                        pltpu.VMEM(tiles, BF16),
                        pltpu.VMEM(tiles, BF16)],
        compiler_params=_cparams("parallel", "arbitrary"),
    )(xt, u, vt, cn, e1, rk, e2, h2)


def _pad_lanes(v, offset):
    return jnp.zeros((1, LANES), F32).at[0, offset:offset + v.shape[0]].set(v.astype(F32))


def _layer(h, l, near_bias, far_bias, attn_norm, w_in, gdn_conv, gdn_a_log, gdn_dt_bias, gdn_out_norm,
           sc_conv, diff_q_norm, diff_k_norm, diff_lambda, diff_subln, w_out, ffn_norm,
           peer_wq, peer_keys, peer_u, peer_v, batch):
    t, d = h.shape
    s = t // batch
    lambda_init = 0.8 - 0.6 * math.exp(-0.3 * l)
    n_main = W_QKVZ + W_SCD
    wi = w_in[l]
    w_r = jnp.concatenate([wi[:, 0:W_QKVZ], wi[:, W_QKVZ + 2 * GDN_HEADS:], wi[:, W_QKVZ:W_QKVZ + 2 * GDN_HEADS],
                           jnp.zeros((d, W_BA - 2 * GDN_HEADS), wi.dtype)], axis=1).astype(BF16)
    assert w_r.shape[1] == n_main + W_BA
    qkvz, scd, ba = _inproj(h, attn_norm[l][None, :], w_r)

    prm = jnp.concatenate([_pad_lanes(gdn_a_log[l], GDN_HEADS), _pad_lanes(gdn_dt_bias[l], GDN_HEADS),
                           jnp.zeros((6, LANES), F32)], axis=0)
    y_gdn = _gdn(qkvz.reshape(batch, s, W_QKVZ), ba.reshape(batch, s, W_BA), gdn_conv[l].astype(F32), prm,
                 gdn_out_norm[l][None, :].astype(F32))

    y_att = _attn(scd.reshape(batch, s, W_SCD), sc_conv[l].astype(F32),
                  jnp.tile(diff_q_norm[l], 256 // DIFF_DQK)[None, :].astype(F32),
                  jnp.tile(diff_k_norm[l], 256 // DIFF_DQK)[None, :].astype(F32),
                  diff_lambda[l].astype(F32),
                  jnp.tile(diff_subln[l], LANES // DIFF_DV)[None, :].astype(F32),
                  near_bias, far_bias, lambda_init)

    h2, xt = _outproj(y_gdn.reshape(t, -1), y_att.reshape(t, -1), w_out[l].astype(BF16), h, ffn_norm[l][None, :])

    cn, e1, rk, e2 = _route(xt, peer_wq[l].T.astype(BF16), peer_keys[l].astype(BF16))
    n_exp = peer_v.shape[1]
    vt = peer_v[l].reshape(n_exp // PEER_EXPERT_BLK, PEER_EXPERT_BLK, d).transpose(0, 2, 1).astype(BF16)
    return _peer(xt, (peer_u[l] * RSQRT2).astype(BF16), vt, cn, e1, rk, e2, h2)


def kernel(x, rel_bias, attn_norm, w_in, gdn_conv, gdn_a_log, gdn_dt_bias, gdn_out_norm, sc_conv,
           diff_q_norm, diff_k_norm, diff_lambda, diff_subln, w_out, ffn_norm, peer_wq, peer_keys,
           peer_u, peer_v):
    batch, s, d = x.shape
    near_bias, far_bias = _bias_tables(rel_bias)
    h = x.reshape(batch * s, d)
    for l in range(w_in.shape[0]):
        h = _layer(h, l, near_bias, far_bias, attn_norm, w_in, gdn_conv, gdn_a_log, gdn_dt_bias,
                   gdn_out_norm, sc_conv, diff_q_norm, diff_k_norm, diff_lambda, diff_subln, w_out,
                   ffn_norm, peer_wq, peer_keys, peer_u, peer_v, batch)
    return h.reshape(batch, s, d)
```
